```python
import math
import jax, jax.numpy as jnp
from jax import lax
import numpy as np

D_MODEL = 1024
BATCH = 8
SEQ = 2048
DEPTH = 2
DEC_BATCH = 128
DEC_SEQ = 8
PAST_LEN = 16384
PAGE_SIZE = 128

N_MIXERS = 2
N_RET_LAYERS = (DEPTH + 1) // 2
N_SWA_LAYERS = DEPTH // 2

RET_HEADS = 4
RET_DK = D_MODEL // RET_HEADS
RET_DV = 2 * D_MODEL // RET_HEADS
RET_CHUNK = 128
ROPE_BASE = 10000.0

SWA_HEADS = 16
SWA_KV_HEADS = 2
SWA_HEAD_DIM = 64
SWA_GROUP = SWA_HEADS // SWA_KV_HEADS
WINDOW = 128
SWA_QKV_WIDTH = (SWA_HEADS + 2 * SWA_KV_HEADS) * SWA_HEAD_DIM

D_FF = -(-8 * D_MODEL // (3 * 256)) * 256
RMS_EPS = 1e-6
GN_EPS = 1e-6
NEG_INF = -1e30

kernel_name = "hybrid_retention_swa_sink_decoder_step"


def rms_norm(x, g):
    xf = x.astype(jnp.float32)
    y = xf * lax.rsqrt(jnp.mean(xf * xf, axis=-1, keepdims=True) + RMS_EPS)
    return (y * g.astype(jnp.float32)).astype(x.dtype)


def swiglu(x, w1, w3, w2):
    return (jax.nn.silu(x @ w1) * (x @ w3)) @ w2


def rotate(x, pos):
    half = x.shape[-1] // 2
    inv = 1.0 / (ROPE_BASE ** (jnp.arange(half, dtype=jnp.float32) / half))
    ang = pos.astype(jnp.float32)[:, None] * inv[None, :]
    cos = jnp.cos(ang)[None, :, None, :]
    sin = jnp.sin(ang)[None, :, None, :]
    x1, x2 = x[..., :half], x[..., half:]
    return jnp.concatenate([x1 * cos - x2 * sin, x1 * sin + x2 * cos], axis=-1)


def retention_chunkwise(q, k, v, s0, chunk):
    b, t, h, _ = q.shape
    dv = v.shape[-1]
    nc = t // chunk
    log_gamma = jnp.log1p(-jnp.exp2(-(5.0 + jnp.arange(h, dtype=jnp.float32))))
    idx = jnp.arange(chunk, dtype=jnp.float32)
    diff = idx[:, None] - idx[None, :]
    intra = jnp.where(diff >= 0, jnp.exp(log_gamma[:, None, None] * jnp.maximum(diff, 0.0)), 0.0)
    q_decay = jnp.exp(log_gamma[None, :] * (idx[:, None] + 1.0))
    k_decay = jnp.exp(log_gamma[None, :] * (chunk - 1.0 - idx[:, None]))
    chunk_decay = jnp.exp(log_gamma * chunk)

    def to_chunks(a):
        return jnp.moveaxis(a.reshape(b, nc, chunk, h, a.shape[-1]), 1, 0)

    def step(s, qkv):
        qc, kc, vc = qkv
        scores = jnp.einsum('bihd,bjhd->bhij', qc, kc) * intra[None]
        o = jnp.einsum('bhij,bjhe->bihe', scores, vc)
        o = o + jnp.einsum('bihd,bhde->bihe', qc, s) * q_decay[None, :, :, None]
        s = s * chunk_decay[None, :, None, None] + jnp.einsum(
            'bjhd,bjhe->bhde', kc * k_decay[None, :, :, None], vc)
        return s, o

    s, o = lax.scan(step, s0, (to_chunks(q), to_chunks(k), to_chunks(v)))
    o = jnp.moveaxis(o, 0, 1).reshape(b, t, h, dv)
    return o, s


def retention_mixer(h, pos, s0, chunk, w_q, w_k, w_v, w_g, w_o):
    b, t, _ = h.shape
    f32 = jnp.float32
    q = (h @ w_q).reshape(b, t, RET_HEADS, RET_DK).astype(f32)
    k = (h @ w_k).reshape(b, t, RET_HEADS, RET_DK).astype(f32)
    v = (h @ w_v).reshape(b, t, RET_HEADS, RET_DV).astype(f32)
    q = rotate(q, pos)
    k = rotate(k, pos) * (RET_DK ** -0.5)
    o, s = retention_chunkwise(q, k, v, s0.astype(f32), chunk)
    mu = jnp.mean(o, axis=-1, keepdims=True)
    var = jnp.mean(jnp.square(o - mu), axis=-1, keepdims=True)
    o = ((o - mu) * lax.rsqrt(var + GN_EPS)).reshape(b, t, RET_HEADS * RET_DV).astype(h.dtype)
    y = (jax.nn.silu(h @ w_g) * o) @ w_o
    return y, s


def swa_project(h, w_qkv, b_qkv):
    b, t, _ = h.shape
    qkv = h @ w_qkv + b_qkv
    nq = SWA_HEADS * SWA_HEAD_DIM
    nk = SWA_KV_HEADS * SWA_HEAD_DIM
    q = qkv[..., :nq].reshape(b, t, SWA_KV_HEADS, SWA_GROUP, SWA_HEAD_DIM)
    k = qkv[..., nq:nq + nk].reshape(b, t, SWA_KV_HEADS, SWA_HEAD_DIM)
    v = qkv[..., nq + nk:].reshape(b, t, SWA_KV_HEADS, SWA_HEAD_DIM)
    return q, k, v


def sink_attention(q, k, v, mask, sinks):
    s = jnp.einsum('bnqhgd,bnkhd->bnhgqk', q, k, preferred_element_type=jnp.float32) * (SWA_HEAD_DIM ** -0.5)
    s = jnp.where(mask[None, :, None, None, :, :], s, NEG_INF)
    sink = sinks.astype(jnp.float32).reshape(SWA_KV_HEADS, SWA_GROUP)[None, None, :, :, None, None]
    m = jnp.maximum(jnp.max(s, axis=-1, keepdims=True), sink)
    p = jnp.exp(s - m)
    p = p / (jnp.sum(p, axis=-1, keepdims=True) + jnp.exp(sink - m))
    return jnp.einsum('bnhgqk,bnkhd->bnqhgd', p.astype(v.dtype), v)


def swa_prompt(h, w_qkv, b_qkv, w_o, b_o, sinks):
    b, t, _ = h.shape
    nb = t // WINDOW
    q, k, v = swa_project(h, w_qkv, b_qkv)
    qb = q.reshape(b, nb, WINDOW, SWA_KV_HEADS, SWA_GROUP, SWA_HEAD_DIM)
    kb = k.reshape(b, nb, WINDOW, SWA_KV_HEADS, SWA_HEAD_DIM)
    vb = v.reshape(b, nb, WINDOW, SWA_KV_HEADS, SWA_HEAD_DIM)
    pad = ((0, 0), (1, 0), (0, 0), (0, 0), (0, 0))
    k_band = jnp.concatenate([jnp.pad(kb, pad)[:, :-1], kb], axis=2)
    v_band = jnp.concatenate([jnp.pad(vb, pad)[:, :-1], vb], axis=2)
    qpos = jnp.arange(t).reshape(nb, WINDOW)
    kpos = jnp.concatenate([qpos - WINDOW, qpos], axis=1)
    diff = qpos[:, :, None] - kpos[:, None, :]
    mask = (kpos[:, None, :] >= 0) & (diff >= 0) & (diff < WINDOW)
    o = sink_attention(qb, k_band, v_band, mask, sinks)
    y = o.reshape(b, t, SWA_HEADS * SWA_HEAD_DIM) @ w_o + b_o
    return y, k[:, t - WINDOW:], v[:, t - WINDOW:]


def swa_sample(h, ck, cv, w_qkv, b_qkv, w_o, b_o, sinks):
    b, t, _ = h.shape
    q, k, v = swa_project(h, w_qkv, b_qkv)
    k_all = jnp.concatenate([ck.astype(k.dtype), k], axis=1)
    v_all = jnp.concatenate([cv.astype(v.dtype), v], axis=1)
    qpos = PAST_LEN + jnp.arange(t)
    kpos = jnp.concatenate([PAST_LEN - WINDOW + jnp.arange(WINDOW), qpos])
    diff = qpos[:, None] - kpos[None, :]
    mask = ((diff >= 0) & (diff < WINDOW))[None]
    o = sink_attention(q[:, None], k_all[:, None], v_all[:, None], mask, sinks)
    y = o.reshape(b, t, SWA_HEADS * SWA_HEAD_DIM) @ w_o + b_o
    return y, k_all[:, -WINDOW:], v_all[:, -WINDOW:]


def setup_inputs(seed: int = 0) -> dict:
    key = jax.random.key(seed)
    ks = jax.random.split(key, 24)
    f32 = jnp.float32
    nr, ns = N_RET_LAYERS, N_SWA_LAYERS

    def w(k, shape, fan_in):
        return jax.random.normal(k, shape, f32) * (fan_in ** -0.5)

    return {
        "x_prompt": jax.random.normal(ks[0], (BATCH, SEQ, D_MODEL), f32),
        "x_sample": jax.random.normal(ks[1], (DEC_BATCH, DEC_SEQ, D_MODEL), f32),
        "state_ret": 0.5 * jax.random.normal(ks[2], (nr, DEC_BATCH, RET_HEADS, RET_DK, RET_DV), f32),
        "cache_swa_k": jax.random.normal(ks[3], (ns, DEC_BATCH, WINDOW, SWA_KV_HEADS, SWA_HEAD_DIM), f32),
        "cache_swa_v": jax.random.normal(ks[4], (ns, DEC_BATCH, WINDOW, SWA_KV_HEADS, SWA_HEAD_DIM), f32),
        "ret_w_q": w(ks[5], (nr, D_MODEL, RET_HEADS * RET_DK), D_MODEL),
        "ret_w_k": w(ks[6], (nr, D_MODEL, RET_HEADS * RET_DK), D_MODEL),
        "ret_w_v": w(ks[7], (nr, D_MODEL, RET_HEADS * RET_DV), D_MODEL),
        "ret_w_g": w(ks[8], (nr, D_MODEL, RET_HEADS * RET_DV), D_MODEL),
        "ret_w_o": w(ks[9], (nr, RET_HEADS * RET_DV, D_MODEL), RET_HEADS * RET_DV),
        "swa_w_qkv": w(ks[10], (ns, D_MODEL, SWA_QKV_WIDTH), D_MODEL),
        "swa_b_qkv": 0.02 * jax.random.normal(ks[11], (ns, SWA_QKV_WIDTH), f32),
        "swa_w_o": w(ks[12], (ns, SWA_HEADS * SWA_HEAD_DIM, D_MODEL), SWA_HEADS * SWA_HEAD_DIM),
        "swa_b_o": 0.02 * jax.random.normal(ks[13], (ns, D_MODEL), f32),
        "swa_sinks": 0.5 * jax.random.normal(ks[14], (ns, SWA_HEADS), f32),
        "norm_mix": 1.0 + 0.02 * jax.random.normal(ks[15], (DEPTH, D_MODEL), f32),
        "norm_ffn": 1.0 + 0.02 * jax.random.normal(ks[16], (DEPTH, D_MODEL), f32),
        "ffn_w1": w(ks[17], (DEPTH, D_MODEL, D_FF), D_MODEL),
        "ffn_w3": w(ks[18], (DEPTH, D_MODEL, D_FF), D_MODEL),
        "ffn_w2": w(ks[19], (DEPTH, D_FF, D_MODEL), D_FF),
        "norm_final": 1.0 + 0.02 * jax.random.normal(ks[20], (D_MODEL,), f32),
    }


def reference(x_prompt, x_sample, state_ret, cache_swa_k, cache_swa_v,
              ret_w_q, ret_w_k, ret_w_v, ret_w_g, ret_w_o,
              swa_w_qkv, swa_b_qkv, swa_w_o, swa_b_o, swa_sinks,
              norm_mix, norm_ffn, ffn_w1, ffn_w3, ffn_w2, norm_final):
    b_p, t_p, _ = x_prompt.shape
    pos_p = jnp.arange(t_p)
    pos_s = PAST_LEN + jnp.arange(x_sample.shape[1])
    xp, xs = x_prompt, x_sample
    ret_p, ret_s, kp_l, vp_l, ks_l, vs_l = [], [], [], [], [], []
    for layer in range(DEPTH):
        hp = rms_norm(xp, norm_mix[layer])
        hs = rms_norm(xs, norm_mix[layer])
        if layer % N_MIXERS == 0:
            r = layer // N_MIXERS
            rw = (ret_w_q[r], ret_w_k[r], ret_w_v[r], ret_w_g[r], ret_w_o[r])
            s0 = jnp.zeros((b_p, RET_HEADS, RET_DK, RET_DV), jnp.float32)
            yp, sp = retention_mixer(hp, pos_p, s0, RET_CHUNK, *rw)
            ys, ss = retention_mixer(hs, pos_s, state_ret[r], xs.shape[1], *rw)
            ret_p.append(sp.astype(x_prompt.dtype))
            ret_s.append(ss.astype(state_ret.dtype))
        else:
            a = layer // N_MIXERS
            aw = (swa_w_qkv[a], swa_b_qkv[a], swa_w_o[a], swa_b_o[a], swa_sinks[a])
            yp, kp, vp = swa_prompt(hp, *aw)
            ys, kn, vn = swa_sample(hs, cache_swa_k[a], cache_swa_v[a], *aw)
            kp_l.append(kp); vp_l.append(vp)
            ks_l.append(kn.astype(cache_swa_k.dtype)); vs_l.append(vn.astype(cache_swa_v.dtype))
        xp = xp + yp
        xs = xs + ys
        xp = xp + swiglu(rms_norm(xp, norm_ffn[layer]), ffn_w1[layer], ffn_w3[layer], ffn_w2[layer])
        xs = xs + swiglu(rms_norm(xs, norm_ffn[layer]), ffn_w1[layer], ffn_w3[layer], ffn_w2[layer])
    y_prompt = rms_norm(xp, norm_final)
    y_sample = rms_norm(xs, norm_final)
    state_ret_prompt = jnp.stack(ret_p)
    state_ret_sample = jnp.stack(ret_s)
    swa_k_prompt = jnp.stack(kp_l)
    swa_v_prompt = jnp.stack(vp_l)
    swa_k_sample = jnp.stack(ks_l)
    swa_v_sample = jnp.stack(vs_l)
    return (y_prompt, y_sample, state_ret_prompt, state_ret_sample,
            swa_k_prompt, swa_v_prompt, swa_k_sample, swa_v_sample)
```

```python
import functools
import math

import numpy as np
import jax
import jax.numpy as jnp
from jax import lax
from jax.experimental import pallas as pl
from jax.experimental.pallas import tpu as pltpu

F32 = jnp.float32
BF16 = jnp.bfloat16

D_MODEL = 1024
SEQ = 2048
DEC_SEQ = 8
PAST_LEN = 16384

RET_HEADS = 4
RET_DK = 256
RET_DV = 512
RET_CHUNK = 128
ROPE_BASE = 10000.0

SWA_HEADS = 16
SWA_KV_HEADS = 2
SWA_HEAD_DIM = 64
SWA_GROUP = SWA_HEADS // SWA_KV_HEADS
WINDOW = 128

D_FF = 2816
RMS_EPS = 1e-6
GN_EPS = 1e-6
NEG_INF = -1e30

LANES = 128
VMEM_LIMIT = 56 * 1024 * 1024


def _params(n_axes):
    return pltpu.CompilerParams(
        dimension_semantics=("arbitrary",) * n_axes, vmem_limit_bytes=VMEM_LIMIT)


def _rms(x, g):
    ms = jnp.mean(x * x, axis=-1, keepdims=True)
    return x * lax.rsqrt(ms + RMS_EPS) * g


def _silu(x):
    return x * jax.nn.sigmoid(x)


def _resident(shape):
    zeros = (0,) * len(shape)
    return pl.BlockSpec(shape, lambda *_: zeros, pipeline_mode=pl.Buffered(1))


def _ret_proj_kernel(x_ref, g_ref, w_ref, cos_ref, sin_ref, q_ref, k_ref, v_ref, gs_ref):
    h = _rms(x_ref[...], g_ref[...]).astype(BF16)
    cos = cos_ref[...]
    sin = sin_ref[...]
    half = RET_DK // 2
    nqk = RET_HEADS * RET_DK
    nv = RET_HEADS * RET_DV

    def rotated(col0, out_ref, scale):
        y = jnp.dot(h, w_ref[:, col0:col0 + nqk], preferred_element_type=F32)
        for hd in range(RET_HEADS):
            lo = hd * RET_DK
            a = y[:, lo:lo + half]
            b = y[:, lo + half:lo + RET_DK]
            out_ref[:, lo:lo + half] = ((a * cos - b * sin) * scale).astype(out_ref.dtype)
            out_ref[:, lo + half:lo + RET_DK] = ((a * sin + b * cos) * scale).astype(out_ref.dtype)

    rotated(0, q_ref, 1.0)
    rotated(nqk, k_ref, RET_DK ** -0.5)
    v_ref[...] = jnp.dot(h, w_ref[:, 2 * nqk:2 * nqk + nv],
                         preferred_element_type=F32).astype(v_ref.dtype)
    g = jnp.dot(h, w_ref[:, 2 * nqk + nv:2 * nqk + 2 * nv], preferred_element_type=F32)
    gs_ref[...] = _silu(g).astype(gs_ref.dtype)


def _ret_proj(x, gamma, w_cat, cos, sin, tm, out_dtype):
    n = x.shape[0]
    nqk = RET_HEADS * RET_DK
    nv = RET_HEADS * RET_DV
    n_pos_blocks = cos.shape[0] // tm
    row = lambda i: (i, 0)
    return pl.pallas_call(
        _ret_proj_kernel,
        grid=(n // tm,),
        in_specs=[
            pl.BlockSpec((tm, D_MODEL), row),
            _resident((1, D_MODEL)),
            _resident(w_cat.shape),
            pl.BlockSpec((tm, LANES), lambda i: (i % n_pos_blocks, 0)),
            pl.BlockSpec((tm, LANES), lambda i: (i % n_pos_blocks, 0)),
        ],
        out_specs=[
            pl.BlockSpec((tm, nqk), row),
            pl.BlockSpec((tm, nqk), row),
            pl.BlockSpec((tm, nv), row),
            pl.BlockSpec((tm, nv), row),
        ],
        out_shape=[
            jax.ShapeDtypeStruct((n, nqk), out_dtype),
            jax.ShapeDtypeStruct((n, nqk), out_dtype),
            jax.ShapeDtypeStruct((n, nv), out_dtype),
            jax.ShapeDtypeStruct((n, nv), out_dtype),
        ],
        compiler_params=_params(1),
        name="ret_proj",
    )(x, gamma, w_cat, cos, sin)


def _ret_chunk(qc, kc, vc, gsc, s_ref, chunk, store_gated):
    cq = qc.shape[0]
    ck = kc.shape[0]
    ii = lax.broadcasted_iota(jnp.int32, (cq, ck), 0)
    jj = lax.broadcasted_iota(jnp.int32, (cq, ck), 1)
    diff = (ii - jj).astype(F32)
    irow = lax.broadcasted_iota(jnp.int32, (cq, 1), 0).astype(F32)
    jrow = lax.broadcasted_iota(jnp.int32, (ck, 1), 0).astype(F32)
    for hd in range(RET_HEADS):
        log_gamma = math.log1p(-(2.0 ** -(5.0 + hd)))
        intra = jnp.where(diff >= 0, jnp.exp(log_gamma * jnp.maximum(diff, 0.0)), 0.0)
        q_decay = jnp.exp(log_gamma * (irow + 1.0))
        k_decay = jnp.exp(log_gamma * (chunk - 1.0 - jrow))
        chunk_decay = math.exp(log_gamma * chunk)
        qh = qc[:, hd * RET_DK:(hd + 1) * RET_DK]
        kh = kc[:, hd * RET_DK:(hd + 1) * RET_DK]
        vh = vc[:, hd * RET_DV:(hd + 1) * RET_DV]
        s_h = s_ref[0, hd]
        scores = lax.dot_general(qh, kh, (((1,), (1,)), ((), ())),
                                 preferred_element_type=F32) * intra
        o = jnp.dot(scores.astype(BF16), vh, preferred_element_type=F32)
        o = o + jnp.dot(qh, s_h.astype(BF16), preferred_element_type=F32) * q_decay
        kd = (kh.astype(F32) * k_decay).astype(BF16)
        s_ref[0, hd] = s_h * chunk_decay + lax.dot_general(
            kd, vh, (((0,), (0,)), ((), ())), preferred_element_type=F32)
        mu = jnp.mean(o, axis=-1, keepdims=True)
        oc = o - mu
        var = jnp.mean(oc * oc, axis=-1, keepdims=True)
        on = oc * lax.rsqrt(var + GN_EPS)
        store_gated(hd, on * gsc[:, hd * RET_DV:(hd + 1) * RET_DV].astype(F32))


def _ret_prompt_kernel(q_ref, k_ref, v_ref, gs_ref, x_ref, wo_ref, y_ref, s_ref, gated_ref,
                       *, n_sub):
    @pl.when(pl.program_id(1) == 0)
    def _():
        s_ref[...] = jnp.zeros_like(s_ref)

    for c in range(n_sub):
        rows = slice(c * RET_CHUNK, (c + 1) * RET_CHUNK)

        def store(hd, val, rows=rows):
            gated_ref[rows, hd * RET_DV:(hd + 1) * RET_DV] = val.astype(BF16)

        _ret_chunk(q_ref[rows, :], k_ref[rows, :], v_ref[rows, :], gs_ref[rows, :],
                   s_ref, RET_CHUNK, store)
    y_ref[...] = x_ref[...] + jnp.dot(gated_ref[...], wo_ref[...], preferred_element_type=F32)


def _ret_prompt(q, k, v, gs, x, wo, batch, n_sub):
    n = x.shape[0]
    tm = n_sub * RET_CHUNK
    steps = n // batch // tm
    nqk = RET_HEADS * RET_DK
    nv = RET_HEADS * RET_DV
    row = lambda b, c: (b * steps + c, 0)
    return pl.pallas_call(
        functools.partial(_ret_prompt_kernel, n_sub=n_sub),
        grid=(batch, steps),
        in_specs=[
            pl.BlockSpec((tm, nqk), row),
            pl.BlockSpec((tm, nqk), row),
            pl.BlockSpec((tm, nv), row),
            pl.BlockSpec((tm, nv), row),
            pl.BlockSpec((tm, D_MODEL), row),
            _resident(wo.shape),
        ],
        out_specs=[
            pl.BlockSpec((tm, D_MODEL), row),
            pl.BlockSpec((1, RET_HEADS, RET_DK, RET_DV), lambda b, c: (b, 0, 0, 0)),
        ],
        out_shape=[
            jax.ShapeDtypeStruct((n, D_MODEL), F32),
            jax.ShapeDtypeStruct((batch, RET_HEADS, RET_DK, RET_DV), F32),
        ],
        scratch_shapes=[pltpu.VMEM((tm, nv), BF16)],
        compiler_params=_params(2),
        name="ret_prompt",
    )(q, k, v, gs, x, wo)


def _ret_sample_kernel(q_ref, k_ref, v_ref, gs_ref, s0_ref, gated_ref, s_ref):
    s_ref[...] = s0_ref[...]
    pad = RET_CHUNK - DEC_SEQ
    kc = jnp.concatenate([k_ref[...], jnp.zeros((pad, k_ref.shape[1]), F32)], axis=0)
    vc = jnp.concatenate([v_ref[...], jnp.zeros((pad, v_ref.shape[1]), F32)], axis=0)

    def store(hd, val):
        gated_ref[:, hd * RET_DV:(hd + 1) * RET_DV] = val

    _ret_chunk(q_ref[...].astype(BF16), kc.astype(BF16), vc.astype(BF16), gs_ref[...],
               s_ref, DEC_SEQ, store)


def _ret_sample(q, k, v, gs, s0):
    batch = s0.shape[0]
    nqk = RET_HEADS * RET_DK
    nv = RET_HEADS * RET_DV
    row = lambda b: (b, 0)
    state = pl.BlockSpec((1, RET_HEADS, RET_DK, RET_DV), lambda b: (b, 0, 0, 0))
    return pl.pallas_call(
        _ret_sample_kernel,
        grid=(batch,),
        in_specs=[
            pl.BlockSpec((DEC_SEQ, nqk), row),
            pl.BlockSpec((DEC_SEQ, nqk), row),
            pl.BlockSpec((DEC_SEQ, nv), row),
            pl.BlockSpec((DEC_SEQ, nv), row),
            state,
        ],
        out_specs=[pl.BlockSpec((DEC_SEQ, nv), row), state],
        out_shape=[
            jax.ShapeDtypeStruct((batch * DEC_SEQ, nv), F32),
            jax.ShapeDtypeStruct(s0.shape, F32),
        ],
        compiler_params=_params(1),
        name="ret_sample",
    )(q, k, v, gs, s0)


def _linear_kernel(a_ref, w_ref, b_ref, res_ref, y_ref):
    y = jnp.dot(a_ref[...].astype(BF16), w_ref[...], preferred_element_type=F32)
    y_ref[...] = res_ref[...] + y + b_ref[...]


def _linear(a, w, bias, res, tm):
    n, kdim = a.shape
    row = lambda i: (i, 0)
    return pl.pallas_call(
        _linear_kernel,
        grid=(n // tm,),
        in_specs=[
            pl.BlockSpec((tm, kdim), row),
            _resident(w.shape),
            _resident(bias.shape),
            pl.BlockSpec((tm, w.shape[1]), row),
        ],
        out_specs=pl.BlockSpec((tm, w.shape[1]), row),
        out_shape=jax.ShapeDtypeStruct((n, w.shape[1]), F32),
        compiler_params=_params(1),
        name="linear_residual",
    )(a, w, bias, res)


def _ffn_kernel(x_ref, g_ref, w1_ref, w3_ref, w2_ref, gf_ref, y_ref, *, final_norm):
    x = x_ref[...]
    h = _rms(x, g_ref[...]).astype(BF16)
    a = jnp.dot(h, w1_ref[...], preferred_element_type=F32)
    b = jnp.dot(h, w3_ref[...], preferred_element_type=F32)
    act = (_silu(a) * b).astype(BF16)
    y = x + jnp.dot(act, w2_ref[...], preferred_element_type=F32)
    if final_norm:
        y = _rms(y, gf_ref[...])
    y_ref[...] = y


def _ffn(x, gamma, w1, w3, w2, gamma_final, tm, final_norm):
    n = x.shape[0]
    row = lambda i: (i, 0)
    return pl.pallas_call(
        functools.partial(_ffn_kernel, final_norm=final_norm),
        grid=(n // tm,),
        in_specs=[
            pl.BlockSpec((tm, D_MODEL), row),
            _resident((1, D_MODEL)),
            _resident(w1.shape),
            _resident(w3.shape),
            _resident(w2.shape),
            _resident((1, D_MODEL)),
        ],
        out_specs=pl.BlockSpec((tm, D_MODEL), row),
        out_shape=jax.ShapeDtypeStruct((n, D_MODEL), F32),
        compiler_params=_params(1),
        name="ffn",
    )(x, gamma, w1, w3, w2, gamma_final)


def _swa_proj_kernel(x_ref, g_ref, w_ref, b_ref, q_ref, kv_ref):
    h = _rms(x_ref[...], g_ref[...]).astype(BF16)
    nq = SWA_HEADS * SWA_HEAD_DIM
    y = jnp.dot(h, w_ref[...], preferred_element_type=F32) + b_ref[...]
    q_ref[...] = (y[:, :nq] * (SWA_HEAD_DIM ** -0.5)).astype(q_ref.dtype)
    kv_ref[...] = y[:, nq:]


def _swa_proj(x, gamma, w, bias, tm, q_dtype):
    n = x.shape[0]
    nq = SWA_HEADS * SWA_HEAD_DIM
    nkv = 2 * SWA_KV_HEADS * SWA_HEAD_DIM
    row = lambda i: (i, 0)
    return pl.pallas_call(
        _swa_proj_kernel,
        grid=(n // tm,),
        in_specs=[
            pl.BlockSpec((tm, D_MODEL), row),
            _resident((1, D_MODEL)),
            _resident(w.shape),
            _resident(bias.shape),
        ],
        out_specs=[pl.BlockSpec((tm, nq), row), pl.BlockSpec((tm, nkv), row)],
        out_shape=[
            jax.ShapeDtypeStruct((n, nq), q_dtype),
            jax.ShapeDtypeStruct((n, nkv), F32),
        ],
        compiler_params=_params(1),
        name="swa_proj",
    )(x, gamma, w, bias)


def _swa_attend(q, k_prev, v_prev, k_cur, v_cur, prev_valid, sinks_ref):
    r = q.shape[0]
    hd = SWA_HEAD_DIM
    lane_k = lax.broadcasted_iota(jnp.int32, (WINDOW, LANES), 1) < hd

    def both_halves(x):
        xr = pltpu.roll(x, hd, 1)
        return jnp.where(lane_k, x, xr), jnp.where(lane_k, xr, x)

    kk = [jnp.concatenate([p, c], axis=0).astype(BF16)
          for p, c in zip(both_halves(k_prev), both_halves(k_cur))]
    vv = [jnp.concatenate([p, c], axis=0).astype(BF16)
          for p, c in zip(both_halves(v_prev), both_halves(v_cur))]

    ri = lax.broadcasted_iota(jnp.int32, (r, 2 * WINDOW), 0)
    cj = lax.broadcasted_iota(jnp.int32, (r, 2 * WINDOW), 1)
    prev_lo = ri + jnp.where(prev_valid, 0, 2 * WINDOW)
    mask = jnp.logical_or(jnp.logical_and(cj < WINDOW, cj > prev_lo),
                          jnp.logical_and(cj >= WINDOW, cj - WINDOW <= ri))
    lane_q = lax.broadcasted_iota(jnp.int32, (r, LANES), 1) < hd

    outs = []
    for g in range(SWA_KV_HEADS):
        for pair in range(SWA_GROUP // 2):
            col = (g * (SWA_GROUP // 2) + pair) * LANES
            qp = q[:, col:col + LANES]
            o_pair = []
            for side in range(2):
                head = g * SWA_GROUP + pair * 2 + side
                keep = lane_q if side == 0 else jnp.logical_not(lane_q)
                qm = jnp.where(keep, qp, jnp.zeros_like(qp))
                s = lax.dot_general(qm, kk[g], (((1,), (1,)), ((), ())),
                                    preferred_element_type=F32)
                s = jnp.where(mask, s, NEG_INF)
                sink = sinks_ref[head]
                m = jnp.maximum(jnp.max(s, axis=-1, keepdims=True), sink)
                p = jnp.exp(s - m)
                den = jnp.sum(p, axis=-1, keepdims=True) + jnp.exp(sink - m)
                o = jnp.dot(p.astype(BF16), vv[g], preferred_element_type=F32)
                o_pair.append(o / den)
            outs.append(jnp.where(lane_q, o_pair[0], o_pair[1]))
    return jnp.concatenate(outs, axis=1)


def _swa_prompt_kernel(sinks_ref, q_ref, kvp_ref, kvc_ref, x_ref, wo_ref, bo_ref, y_ref):
    nk = SWA_KV_HEADS * SWA_HEAD_DIM
    kvp = kvp_ref[...]
    kvc = kvc_ref[...]
    o = _swa_attend(q_ref[...], kvp[:, :nk], kvp[:, nk:], kvc[:, :nk], kvc[:, nk:],
                    pl.program_id(1) > 0, sinks_ref)
    y = jnp.dot(o.astype(BF16), wo_ref[...], preferred_element_type=F32)
    y_ref[...] = x_ref[...] + y + bo_ref[...]


def _swa_prompt(q, kv, x, wo, bo, sinks, batch):
    n = x.shape[0]
    nb = n // batch // WINDOW
    nq = SWA_HEADS * SWA_HEAD_DIM
    nkv = kv.shape[1]
    row = lambda b, i: (b * nb + i, 0)
    prev = lambda b, i: (b * nb + jnp.maximum(i - 1, 0), 0)
    return pl.pallas_call(
        _swa_prompt_kernel,
        grid=(batch, nb),
        in_specs=[
            pl.BlockSpec(memory_space=pltpu.SMEM),
            pl.BlockSpec((WINDOW, nq), row),
            pl.BlockSpec((WINDOW, nkv), prev),
            pl.BlockSpec((WINDOW, nkv), row),
            pl.BlockSpec((WINDOW, D_MODEL), row),
            _resident(wo.shape),
            _resident(bo.shape),
        ],
        out_specs=pl.BlockSpec((WINDOW, D_MODEL), row),
        out_shape=jax.ShapeDtypeStruct((n, D_MODEL), F32),
        compiler_params=_params(2),
        name="swa_prompt",
    )(sinks, q, kv, kv, x, wo, bo)


def _swa_sample_kernel(sinks_ref, q_ref, kvn_ref, ck_ref, cv_ref, o_ref, nk_ref, nv_ref):
    nk = SWA_KV_HEADS * SWA_HEAD_DIM
    t = DEC_SEQ
    kvn = kvn_ref[...]
    k_new = kvn[:, :nk]
    v_new = kvn[:, nk:]
    ck = ck_ref[0]
    cv = cv_ref[0]
    zpad = jnp.zeros((WINDOW - t, nk), F32)
    o_ref[...] = _swa_attend(
        q_ref[...].astype(BF16), ck, cv,
        jnp.concatenate([k_new, zpad], axis=0), jnp.concatenate([v_new, zpad], axis=0),
        True, sinks_ref)
    nk_ref[0, :WINDOW - t, :] = ck[t:, :]
    nk_ref[0, WINDOW - t:, :] = k_new
    nv_ref[0, :WINDOW - t, :] = cv[t:, :]
    nv_ref[0, WINDOW - t:, :] = v_new


def _swa_sample(q, kv_new, cache_k, cache_v, sinks):
    batch = cache_k.shape[0]
    nq = SWA_HEADS * SWA_HEAD_DIM
    nk = SWA_KV_HEADS * SWA_HEAD_DIM
    row = lambda b: (b, 0)
    cache = pl.BlockSpec((1, WINDOW, nk), lambda b: (b, 0, 0))
    return pl.pallas_call(
        _swa_sample_kernel,
        grid=(batch,),
        in_specs=[
            pl.BlockSpec(memory_space=pltpu.SMEM),
            pl.BlockSpec((DEC_SEQ, nq), row),
            pl.BlockSpec((DEC_SEQ, 2 * nk), row),
            cache,
            cache,
        ],
        out_specs=[pl.BlockSpec((DEC_SEQ, nq), row), cache, cache],
        out_shape=[
            jax.ShapeDtypeStruct((batch * DEC_SEQ, nq), F32),
            jax.ShapeDtypeStruct(cache_k.shape, F32),
            jax.ShapeDtypeStruct(cache_v.shape, F32),
        ],
        compiler_params=_params(1),
        name="swa_sample",
    )(sinks, q, kv_new, cache_k, cache_v)


def _rotation_tables(positions, repeat):
    half = RET_DK // 2
    inv = 1.0 / (ROPE_BASE ** (np.arange(half, dtype=np.float64) / half))
    ang = np.asarray(positions, np.float64)[:, None] * inv[None, :]
    cos = np.tile(np.cos(ang), (repeat, 1)).astype(np.float32)
    sin = np.tile(np.sin(ang), (repeat, 1)).astype(np.float32)
    return jnp.asarray(cos), jnp.asarray(sin)


TM_PROMPT = 256
TM_SAMPLE = 256


def kernel(x_prompt, x_sample, state_ret, cache_swa_k, cache_swa_v, ret_w_q, ret_w_k, ret_w_v, ret_w_g, ret_w_o, swa_w_qkv, swa_b_qkv, swa_w_o, swa_b_o, swa_sinks, norm_mix, norm_ffn, ffn_w1, ffn_w3, ffn_w2, norm_final):
    bp, tp, d = x_prompt.shape
    bs, ts, _ = x_sample.shape
    assert (tp, d, ts) == (SEQ, D_MODEL, DEC_SEQ)
    xp = x_prompt.reshape(bp * tp, d)
    xs = x_sample.reshape(bs * ts, d)
    nk = SWA_KV_HEADS * SWA_HEAD_DIM

    cos_p, sin_p = _rotation_tables(np.arange(tp), 1)
    cos_s, sin_s = _rotation_tables(PAST_LEN + np.arange(ts), TM_SAMPLE // ts)
    gamma_final = norm_final.reshape(1, d)
    zero_bias = jnp.zeros((1, d), F32)

    w_cat = jnp.concatenate([ret_w_q[0], ret_w_k[0], ret_w_v[0], ret_w_g[0]], axis=1).astype(BF16)
    w_ro = ret_w_o[0].astype(BF16)
    g_mix = norm_mix[0].reshape(1, d)
    qp, kp, vp, gp = _ret_proj(xp, g_mix, w_cat, cos_p, sin_p, TM_PROMPT, BF16)
    xp, state_p = _ret_prompt(qp, kp, vp, gp, xp, w_ro, bp, 2)
    qs, ks, vs, gs = _ret_proj(xs, g_mix, w_cat, cos_s, sin_s, TM_SAMPLE, F32)
    gated_s, state_s = _ret_sample(qs, ks, vs, gs, state_ret[0])
    xs = _linear(gated_s, w_ro, zero_bias, xs, TM_SAMPLE)

    w1 = ffn_w1[0].astype(BF16)
    w3 = ffn_w3[0].astype(BF16)
    w2 = ffn_w2[0].astype(BF16)
    g_ffn = norm_ffn[0].reshape(1, d)
    xp = _ffn(xp, g_ffn, w1, w3, w2, gamma_final, TM_PROMPT, False)
    xs = _ffn(xs, g_ffn, w1, w3, w2, gamma_final, TM_SAMPLE, False)

    w_qkv = swa_w_qkv[0].astype(BF16)
    b_qkv = swa_b_qkv[0].reshape(1, -1)
    w_so = swa_w_o[0].astype(BF16)
    b_so = swa_b_o[0].reshape(1, d)
    sinks = swa_sinks[0]
    g_mix = norm_mix[1].reshape(1, d)
    qp, kvp = _swa_proj(xp, g_mix, w_qkv, b_qkv, TM_PROMPT, BF16)
    xp = _swa_prompt(qp, kvp, xp, w_so, b_so, sinks, bp)
    qs, kvs = _swa_proj(xs, g_mix, w_qkv, b_qkv, TM_SAMPLE, F32)
    att_s, new_k, new_v = _swa_sample(
        qs, kvs, cache_swa_k[0].reshape(bs, WINDOW, nk), cache_swa_v[0].reshape(bs, WINDOW, nk), sinks)
    xs = _linear(att_s, w_so, b_so, xs, TM_SAMPLE)

    w1 = ffn_w1[1].astype(BF16)
    w3 = ffn_w3[1].astype(BF16)
    w2 = ffn_w2[1].astype(BF16)
    g_ffn = norm_ffn[1].reshape(1, d)
    yp = _ffn(xp, g_ffn, w1, w3, w2, gamma_final, TM_PROMPT, True)
    ys = _ffn(xs, g_ffn, w1, w3, w2, gamma_final, TM_SAMPLE, True)

    kv_tail = kvp.reshape(bp, tp, 2 * nk)[:, tp - WINDOW:, :]
    kv_shape = (1, bp, WINDOW, SWA_KV_HEADS, SWA_HEAD_DIM)
    cache_shape = (1, bs, WINDOW, SWA_KV_HEADS, SWA_HEAD_DIM)
    return (
        yp.reshape(bp, tp, d),
        ys.reshape(bs, ts, d),
        state_p[None],
        state_s[None],
        kv_tail[..., :nk].reshape(kv_shape),
        kv_tail[..., nk:].reshape(kv_shape),
        new_k.reshape(cache_shape),
        new_v.reshape(cache_shape),
    )
```

```python
import functools
import math

import numpy as np
import jax
import jax.numpy as jnp
from jax import lax
from jax.experimental import pallas as pl
from jax.experimental.pallas import tpu as pltpu

F32 = jnp.float32
BF16 = jnp.bfloat16

D_MODEL = 1024
SEQ = 2048
DEC_SEQ = 8
PAST_LEN = 16384

RET_HEADS = 4
RET_DK = 256
RET_DV = 512
RET_CHUNK = 128
ROPE_BASE = 10000.0

SWA_HEADS = 16
SWA_KV_HEADS = 2
SWA_HEAD_DIM = 64
SWA_GROUP = SWA_HEADS // SWA_KV_HEADS
WINDOW = 128

D_FF = 2816
RMS_EPS = 1e-6
GN_EPS = 1e-6
NEG_INF = -1e30

LANES = 128
VMEM_LIMIT = 56 * 1024 * 1024


def _params(n_axes):
    return pltpu.CompilerParams(
        dimension_semantics=("arbitrary",) * n_axes, vmem_limit_bytes=VMEM_LIMIT)


def _rms(x, g):
    ms = jnp.mean(x * x, axis=-1, keepdims=True)
    return x * lax.rsqrt(ms + RMS_EPS) * g


def _silu(x):
    return x * jax.nn.sigmoid(x)


def _resident(shape):
    zeros = (0,) * len(shape)
    return pl.BlockSpec(shape, lambda *_: zeros, pipeline_mode=pl.Buffered(1))


def _ret_proj_kernel(x_ref, g_ref, w_ref, cos_ref, sin_ref, q_ref, k_ref, v_ref, gs_ref):
    h = _rms(x_ref[...], g_ref[...]).astype(BF16)
    cos = cos_ref[...]
    sin = sin_ref[...]
    half = RET_DK // 2
    nqk = RET_HEADS * RET_DK
    nv = RET_HEADS * RET_DV

    def rotated(col0, out_ref, scale):
        y = jnp.dot(h, w_ref[:, col0:col0 + nqk], preferred_element_type=F32)
        for hd in range(RET_HEADS):
            lo = hd * RET_DK
            a = y[:, lo:lo + half]
            b = y[:, lo + half:lo + RET_DK]
            out_ref[:, lo:lo + half] = ((a * cos - b * sin) * scale).astype(out_ref.dtype)
            out_ref[:, lo + half:lo + RET_DK] = ((a * sin + b * cos) * scale).astype(out_ref.dtype)

    rotated(0, q_ref, 1.0)
    rotated(nqk, k_ref, RET_DK ** -0.5)
    v_ref[...] = jnp.dot(h, w_ref[:, 2 * nqk:2 * nqk + nv],
                         preferred_element_type=F32).astype(v_ref.dtype)
    g = jnp.dot(h, w_ref[:, 2 * nqk + nv:2 * nqk + 2 * nv], preferred_element_type=F32)
    gs_ref[...] = _silu(g).astype(gs_ref.dtype)


def _ret_proj(x, gamma, w_cat, cos, sin, tm, out_dtype):
    n = x.shape[0]
    nqk = RET_HEADS * RET_DK
    nv = RET_HEADS * RET_DV
    n_pos_blocks = cos.shape[0] // tm
    row = lambda i: (i, 0)
    return pl.pallas_call(
        _ret_proj_kernel,
        grid=(n // tm,),
        in_specs=[
            pl.BlockSpec((tm, D_MODEL), row),
            _resident((1, D_MODEL)),
            _resident(w_cat.shape),
            pl.BlockSpec((tm, LANES), lambda i: (i % n_pos_blocks, 0)),
            pl.BlockSpec((tm, LANES), lambda i: (i % n_pos_blocks, 0)),
        ],
        out_specs=[
            pl.BlockSpec((tm, nqk), row),
            pl.BlockSpec((tm, nqk), row),
            pl.BlockSpec((tm, nv), row),
            pl.BlockSpec((tm, nv), row),
        ],
        out_shape=[
            jax.ShapeDtypeStruct((n, nqk), out_dtype),
            jax.ShapeDtypeStruct((n, nqk), out_dtype),
            jax.ShapeDtypeStruct((n, nv), out_dtype),
            jax.ShapeDtypeStruct((n, nv), out_dtype),
        ],
        compiler_params=_params(1),
        name="ret_proj",
    )(x, gamma, w_cat, cos, sin)


def _ret_chunk(qc, kc, vc, gsc, s_in, s_out, chunk, store_gated):
    cq = qc.shape[0]
    ck = kc.shape[0]
    ii = lax.broadcasted_iota(jnp.int32, (cq, ck), 0)
    jj = lax.broadcasted_iota(jnp.int32, (cq, ck), 1)
    diff = (ii - jj).astype(F32)
    irow = lax.broadcasted_iota(jnp.int32, (cq, 1), 0).astype(F32)
    jrow = lax.broadcasted_iota(jnp.int32, (ck, 1), 0).astype(F32)
    for hd in range(RET_HEADS):
        log_gamma = math.log1p(-(2.0 ** -(5.0 + hd)))
        intra = jnp.where(diff >= 0, jnp.exp(log_gamma * jnp.maximum(diff, 0.0)), 0.0)
        q_decay = jnp.exp(log_gamma * (irow + 1.0))
        k_decay = jnp.exp(log_gamma * (chunk - 1.0 - jrow))
        chunk_decay = math.exp(log_gamma * chunk)
        qh = qc[:, hd * RET_DK:(hd + 1) * RET_DK]
        kh = kc[:, hd * RET_DK:(hd + 1) * RET_DK]
        vh = vc[:, hd * RET_DV:(hd + 1) * RET_DV]
        s_h = s_in[hd]
        scores = lax.dot_general(qh, kh, (((1,), (1,)), ((), ())),
                                 preferred_element_type=F32) * intra
        o = jnp.dot(scores.astype(BF16), vh, preferred_element_type=F32)
        o = o + jnp.dot(qh, s_h.astype(BF16), preferred_element_type=F32) * q_decay
        kd = (kh.astype(F32) * k_decay).astype(BF16)
        s_out[hd] = s_h * chunk_decay + lax.dot_general(
            kd, vh, (((0,), (0,)), ((), ())), preferred_element_type=F32)
        mu = jnp.mean(o, axis=-1, keepdims=True)
        oc = o - mu
        var = jnp.mean(oc * oc, axis=-1, keepdims=True)
        on = oc * lax.rsqrt(var + GN_EPS)
        store_gated(hd, on * gsc[:, hd * RET_DV:(hd + 1) * RET_DV].astype(F32))


def _ret_prompt_kernel(q_ref, k_ref, v_ref, gs_ref, x_ref, wo_ref, y_ref, s_ref, gated_ref,
                       *, n_sub):
    @pl.when(pl.program_id(1) == 0)
    def _():
        s_ref[...] = jnp.zeros_like(s_ref)

    for c in range(n_sub):
        rows = slice(c * RET_CHUNK, (c + 1) * RET_CHUNK)

        def store(hd, val, rows=rows):
            gated_ref[rows, hd * RET_DV:(hd + 1) * RET_DV] = val.astype(BF16)

        _ret_chunk(q_ref[rows, :], k_ref[rows, :], v_ref[rows, :], gs_ref[rows, :],
                   s_ref.at[0], s_ref.at[0], RET_CHUNK, store)
    y_ref[...] = x_ref[...] + jnp.dot(gated_ref[...], wo_ref[...], preferred_element_type=F32)


def _ret_prompt(q, k, v, gs, x, wo, batch, n_sub):
    n = x.shape[0]
    tm = n_sub * RET_CHUNK
    steps = n // batch // tm
    nqk = RET_HEADS * RET_DK
    nv = RET_HEADS * RET_DV
    row = lambda b, c: (b * steps + c, 0)
    return pl.pallas_call(
        functools.partial(_ret_prompt_kernel, n_sub=n_sub),
        grid=(batch, steps),
        in_specs=[
            pl.BlockSpec((tm, nqk), row),
            pl.BlockSpec((tm, nqk), row),
            pl.BlockSpec((tm, nv), row),
            pl.BlockSpec((tm, nv), row),
            pl.BlockSpec((tm, D_MODEL), row),
            _resident(wo.shape),
        ],
        out_specs=[
            pl.BlockSpec((tm, D_MODEL), row),
            pl.BlockSpec((1, RET_HEADS, RET_DK, RET_DV), lambda b, c: (b, 0, 0, 0)),
        ],
        out_shape=[
            jax.ShapeDtypeStruct((n, D_MODEL), F32),
            jax.ShapeDtypeStruct((batch, RET_HEADS, RET_DK, RET_DV), F32),
        ],
        scratch_shapes=[pltpu.VMEM((tm, nv), BF16)],
        compiler_params=_params(2),
        name="ret_prompt",
    )(q, k, v, gs, x, wo)


def _ret_sample_kernel(q_ref, k_ref, v_ref, gs_ref, s0_ref, gated_ref, s_ref, *, nb):
    pad = RET_CHUNK - DEC_SEQ
    kpad = jnp.zeros((pad, k_ref.shape[1]), F32)
    vpad = jnp.zeros((pad, v_ref.shape[1]), F32)
    for b in range(nb):
        rows = slice(b * DEC_SEQ, (b + 1) * DEC_SEQ)
        kc = jnp.concatenate([k_ref[rows, :], kpad], axis=0).astype(BF16)
        vc = jnp.concatenate([v_ref[rows, :], vpad], axis=0).astype(BF16)

        def store(hd, val, rows=rows):
            gated_ref[rows, hd * RET_DV:(hd + 1) * RET_DV] = val

        _ret_chunk(q_ref[rows, :].astype(BF16), kc, vc, gs_ref[rows, :],
                   s0_ref.at[b], s_ref.at[b], DEC_SEQ, store)


def _ret_sample(q, k, v, gs, s0, nb):
    batch = s0.shape[0]
    nqk = RET_HEADS * RET_DK
    nv = RET_HEADS * RET_DV
    tm = nb * DEC_SEQ
    row = lambda b: (b, 0)
    state = pl.BlockSpec((nb, RET_HEADS, RET_DK, RET_DV), lambda b: (b, 0, 0, 0))
    return pl.pallas_call(
        functools.partial(_ret_sample_kernel, nb=nb),
        grid=(batch // nb,),
        in_specs=[
            pl.BlockSpec((tm, nqk), row),
            pl.BlockSpec((tm, nqk), row),
            pl.BlockSpec((tm, nv), row),
            pl.BlockSpec((tm, nv), row),
            state,
        ],
        out_specs=[pl.BlockSpec((tm, nv), row), state],
        out_shape=[
            jax.ShapeDtypeStruct((batch * DEC_SEQ, nv), F32),
            jax.ShapeDtypeStruct(s0.shape, F32),
        ],
        compiler_params=_params(1),
        name="ret_sample",
    )(q, k, v, gs, s0)


def _linear_kernel(a_ref, w_ref, b_ref, res_ref, y_ref):
    y = jnp.dot(a_ref[...].astype(BF16), w_ref[...], preferred_element_type=F32)
    y_ref[...] = res_ref[...] + y + b_ref[...]


def _linear(a, w, bias, res, tm):
    n, kdim = a.shape
    row = lambda i: (i, 0)
    return pl.pallas_call(
        _linear_kernel,
        grid=(n // tm,),
        in_specs=[
            pl.BlockSpec((tm, kdim), row),
            _resident(w.shape),
            _resident(bias.shape),
            pl.BlockSpec((tm, w.shape[1]), row),
        ],
        out_specs=pl.BlockSpec((tm, w.shape[1]), row),
        out_shape=jax.ShapeDtypeStruct((n, w.shape[1]), F32),
        compiler_params=_params(1),
        name="linear_residual",
    )(a, w, bias, res)


def _ffn_kernel(x_ref, g_ref, w1_ref, w3_ref, w2_ref, gf_ref, y_ref, *, final_norm):
    x = x_ref[...]
    h = _rms(x, g_ref[...]).astype(BF16)
    a = jnp.dot(h, w1_ref[...], preferred_element_type=F32)
    b = jnp.dot(h, w3_ref[...], preferred_element_type=F32)
    act = (_silu(a) * b).astype(BF16)
    y = x + jnp.dot(act, w2_ref[...], preferred_element_type=F32)
    if final_norm:
        y = _rms(y, gf_ref[...])
    y_ref[...] = y


def _ffn(x, gamma, w1, w3, w2, gamma_final, tm, final_norm):
    n = x.shape[0]
    row = lambda i: (i, 0)
    return pl.pallas_call(
        functools.partial(_ffn_kernel, final_norm=final_norm),
        grid=(n // tm,),
        in_specs=[
            pl.BlockSpec((tm, D_MODEL), row),
            _resident((1, D_MODEL)),
            _resident(w1.shape),
            _resident(w3.shape),
            _resident(w2.shape),
            _resident((1, D_MODEL)),
        ],
        out_specs=pl.BlockSpec((tm, D_MODEL), row),
        out_shape=jax.ShapeDtypeStruct((n, D_MODEL), F32),
        compiler_params=_params(1),
        name="ffn",
    )(x, gamma, w1, w3, w2, gamma_final)


def _swa_proj_kernel(x_ref, g_ref, w_ref, b_ref, q_ref, kv_ref):
    h = _rms(x_ref[...], g_ref[...]).astype(BF16)
    nq = SWA_HEADS * SWA_HEAD_DIM
    y = jnp.dot(h, w_ref[...], preferred_element_type=F32) + b_ref[...]
    q_ref[...] = (y[:, :nq] * (SWA_HEAD_DIM ** -0.5)).astype(q_ref.dtype)
    kv_ref[...] = y[:, nq:]


def _swa_proj(x, gamma, w, bias, tm, q_dtype):
    n = x.shape[0]
    nq = SWA_HEADS * SWA_HEAD_DIM
    nkv = 2 * SWA_KV_HEADS * SWA_HEAD_DIM
    row = lambda i: (i, 0)
    return pl.pallas_call(
        _swa_proj_kernel,
        grid=(n // tm,),
        in_specs=[
            pl.BlockSpec((tm, D_MODEL), row),
            _resident((1, D_MODEL)),
            _resident(w.shape),
            _resident(bias.shape),
        ],
        out_specs=[pl.BlockSpec((tm, nq), row), pl.BlockSpec((tm, nkv), row)],
        out_shape=[
            jax.ShapeDtypeStruct((n, nq), q_dtype),
            jax.ShapeDtypeStruct((n, nkv), F32),
        ],
        compiler_params=_params(1),
        name="swa_proj",
    )(x, gamma, w, bias)


def _swa_attend(q, k2, v2, prev_valid, sinks_ref):
    r = q.shape[0]
    hd = SWA_HEAD_DIM
    lane_lo = lax.broadcasted_iota(jnp.int32, (r, LANES), 1) < hd
    zero = jnp.zeros((r, LANES), q.dtype)
    pieces = []
    sink_pieces = []
    for j in range(SWA_GROUP):
        slab = q[:, j * LANES:(j + 1) * LANES]
        pieces.append(jnp.where(lane_lo, slab, zero))
        pieces.append(jnp.where(lane_lo, zero, slab))
        sink_pieces.append(jnp.full((r, 1), sinks_ref[j], F32))
        sink_pieces.append(jnp.full((r, 1), sinks_ref[SWA_GROUP + j], F32))
    qs = jnp.concatenate(pieces, axis=0).astype(BF16)
    sink = jnp.concatenate(sink_pieces, axis=0)
    rows = SWA_HEADS * r

    s2 = lax.dot_general(qs, k2, (((1,), (1,)), ((), ())), preferred_element_type=F32)
    qi = lax.broadcasted_iota(jnp.int32, (rows, WINDOW), 0) & (r - 1)
    kj = lax.broadcasted_iota(jnp.int32, (rows, WINDOW), 1)
    cur = kj <= qi
    prev_cap = jnp.where(prev_valid, jnp.float32(3.0e38), jnp.float32(NEG_INF))
    s = jnp.where(cur, s2[:, WINDOW:], jnp.minimum(s2[:, :WINDOW], prev_cap))
    m = jnp.maximum(jnp.max(s, axis=-1, keepdims=True), sink)
    p = jnp.exp(s - m)
    den = jnp.sum(p, axis=-1, keepdims=True) + jnp.exp(sink - m)
    p2 = jnp.concatenate([jnp.where(cur, 0.0, p), jnp.where(cur, p, 0.0)], axis=1).astype(BF16)
    o = jnp.dot(p2, v2, preferred_element_type=F32) / den
    outs = [jnp.where(lane_lo, o[(2 * j) * r:(2 * j + 1) * r], o[(2 * j + 1) * r:(2 * j + 2) * r])
            for j in range(SWA_GROUP)]
    return jnp.concatenate(outs, axis=1)


def _swa_prompt_kernel(sinks_ref, q_ref, kvp_ref, kvc_ref, x_ref, wo_ref, bo_ref, y_ref, o_ref,
                       *, n_sub):
    nk = SWA_KV_HEADS * SWA_HEAD_DIM
    r = WINDOW
    lane_lo = lax.broadcasted_iota(jnp.int32, (r, LANES), 1) < SWA_HEAD_DIM
    qi = lax.broadcasted_iota(jnp.int32, (r, WINDOW), 0)
    kj = lax.broadcasted_iota(jnp.int32, (r, WINDOW), 1)
    cur_mask = kj <= qi
    ones = jnp.ones((2 * WINDOW, LANES), BF16)
    nt = (((1,), (1,)), ((), ()))
    for c in range(n_sub):
        rows = slice(c * WINDOW, (c + 1) * WINDOW)
        prev = kvp_ref[...] if c == 0 else kvc_ref[(c - 1) * WINDOW:c * WINDOW, :]
        cur = kvc_ref[rows, :]
        k2 = jnp.concatenate([prev[:, :nk], cur[:, :nk]], axis=0).astype(BF16)
        v2 = jnp.concatenate([prev[:, nk:], cur[:, nk:]], axis=0).astype(BF16)
        v2_ones = jnp.concatenate([v2, ones], axis=1)
        prev_valid = (pl.program_id(1) > 0) if c == 0 else True
        prev_cap = jnp.where(prev_valid, jnp.float32(3.0e38), jnp.float32(NEG_INF))
        for j in range(SWA_GROUP):
            slab = q_ref[rows, j * LANES:(j + 1) * LANES]
            zero = jnp.zeros_like(slab)
            num = []
            den = []
            for side in range(2):
                qm = jnp.where(lane_lo, slab, zero) if side == 0 else jnp.where(lane_lo, zero, slab)
                s2 = lax.dot_general(qm, k2, nt, preferred_element_type=F32)
                s = jnp.where(cur_mask, s2[:, WINDOW:], jnp.minimum(s2[:, :WINDOW], prev_cap))
                sink = sinks_ref[j + side * SWA_GROUP]
                m = jnp.maximum(jnp.max(s, axis=-1, keepdims=True), sink)
                p = jnp.exp(s - m)
                p2 = jnp.concatenate([jnp.where(cur_mask, 0.0, p), jnp.where(cur_mask, p, 0.0)],
                                     axis=1).astype(BF16)
                o2 = jnp.dot(p2, v2_ones, preferred_element_type=F32)
                num.append(o2[:, :LANES])
                den.append(o2[:, LANES:] + jnp.exp(sink - m))
            o_ref[rows, j * LANES:(j + 1) * LANES] = (
                jnp.where(lane_lo, num[0], num[1]) / jnp.where(lane_lo, den[0], den[1])).astype(BF16)
    y = jnp.dot(o_ref[...], wo_ref[...], preferred_element_type=F32)
    y_ref[...] = x_ref[...] + y + bo_ref[...]


def _swa_prompt(q, kv, x, wo, bo, sinks, batch, n_sub):
    n = x.shape[0]
    tm = n_sub * WINDOW
    steps = n // batch // tm
    nq = SWA_HEADS * SWA_HEAD_DIM
    nkv = kv.shape[1]
    row = lambda b, i: (b * steps + i, 0)
    prev = lambda b, i: ((b * steps + i) * n_sub - jnp.minimum(i, 1), 0)
    return pl.pallas_call(
        functools.partial(_swa_prompt_kernel, n_sub=n_sub),
        grid=(batch, steps),
        in_specs=[
            pl.BlockSpec(memory_space=pltpu.SMEM),
            pl.BlockSpec((tm, nq), row),
            pl.BlockSpec((WINDOW, nkv), prev),
            pl.BlockSpec((tm, nkv), row),
            pl.BlockSpec((tm, D_MODEL), row),
            _resident(wo.shape),
            _resident(bo.shape),
        ],
        out_specs=pl.BlockSpec((tm, D_MODEL), row),
        out_shape=jax.ShapeDtypeStruct((n, D_MODEL), F32),
        scratch_shapes=[pltpu.VMEM((tm, nq), BF16)],
        compiler_params=_params(2),
        name="swa_prompt",
    )(sinks, q, kv, kv, x, wo, bo)


def _swa_sample_kernel(sinks_ref, q_ref, kvn_ref, ck_ref, cv_ref, o_ref, nk_ref, nv_ref, *, nb):
    nk = SWA_KV_HEADS * SWA_HEAD_DIM
    t = DEC_SEQ
    zpad = jnp.zeros((WINDOW - t, nk), F32)
    for b in range(nb):
        rows = slice(b * t, (b + 1) * t)
        k_new = kvn_ref[rows, :nk]
        v_new = kvn_ref[rows, nk:]
        ck = ck_ref[b]
        cv = cv_ref[b]
        k2 = jnp.concatenate([ck, k_new, zpad], axis=0).astype(BF16)
        v2 = jnp.concatenate([cv, v_new, zpad], axis=0).astype(BF16)
        o_ref[rows, :] = _swa_attend(q_ref[rows, :], k2, v2, True, sinks_ref)
        nk_ref[b, :WINDOW - t, :] = ck[t:, :]
        nk_ref[b, WINDOW - t:, :] = k_new
        nv_ref[b, :WINDOW - t, :] = cv[t:, :]
        nv_ref[b, WINDOW - t:, :] = v_new


def _swa_sample(q, kv_new, cache_k, cache_v, sinks, nb):
    batch = cache_k.shape[0]
    nq = SWA_HEADS * SWA_HEAD_DIM
    nk = SWA_KV_HEADS * SWA_HEAD_DIM
    row = lambda b: (b, 0)
    cache = pl.BlockSpec((nb, WINDOW, nk), lambda b: (b, 0, 0))
    return pl.pallas_call(
        functools.partial(_swa_sample_kernel, nb=nb),
        grid=(batch // nb,),
        in_specs=[
            pl.BlockSpec(memory_space=pltpu.SMEM),
            pl.BlockSpec((nb * DEC_SEQ, nq), row),
            pl.BlockSpec((nb * DEC_SEQ, 2 * nk), row),
            cache,
            cache,
        ],
        out_specs=[pl.BlockSpec((nb * DEC_SEQ, nq), row), cache, cache],
        out_shape=[
            jax.ShapeDtypeStruct((batch * DEC_SEQ, nq), F32),
            jax.ShapeDtypeStruct(cache_k.shape, F32),
            jax.ShapeDtypeStruct(cache_v.shape, F32),
        ],
        compiler_params=_params(1),
        name="swa_sample",
    )(sinks, q, kv_new, cache_k, cache_v)


def _rotation_tables(positions, repeat):
    half = RET_DK // 2
    inv = 1.0 / (ROPE_BASE ** (np.arange(half, dtype=np.float64) / half))
    ang = np.asarray(positions, np.float64)[:, None] * inv[None, :]
    cos = np.tile(np.cos(ang), (repeat, 1)).astype(np.float32)
    sin = np.tile(np.sin(ang), (repeat, 1)).astype(np.float32)
    return jnp.asarray(cos), jnp.asarray(sin)


TM_PROJ = 512
TM_SAMPLE = 256


def kernel(x_prompt, x_sample, state_ret, cache_swa_k, cache_swa_v, ret_w_q, ret_w_k, ret_w_v, ret_w_g, ret_w_o, swa_w_qkv, swa_b_qkv, swa_w_o, swa_b_o, swa_sinks, norm_mix, norm_ffn, ffn_w1, ffn_w3, ffn_w2, norm_final):
    bp, tp, d = x_prompt.shape
    bs, ts, _ = x_sample.shape
    assert (tp, d, ts) == (SEQ, D_MODEL, DEC_SEQ)
    xp = x_prompt.reshape(bp * tp, d)
    xs = x_sample.reshape(bs * ts, d)
    nk = SWA_KV_HEADS * SWA_HEAD_DIM

    cos_p, sin_p = _rotation_tables(np.arange(tp), 1)
    cos_s, sin_s = _rotation_tables(PAST_LEN + np.arange(ts), TM_SAMPLE // ts)
    gamma_final = norm_final.reshape(1, d)
    zero_bias = jnp.zeros((1, d), F32)

    w_cat = jnp.concatenate([ret_w_q[0], ret_w_k[0], ret_w_v[0], ret_w_g[0]], axis=1).astype(BF16)
    w_ro = ret_w_o[0].astype(BF16)
    g_mix = norm_mix[0].reshape(1, d)
    qp, kp, vp, gp = _ret_proj(xp, g_mix, w_cat, cos_p, sin_p, TM_PROJ, BF16)
    xp, state_p = _ret_prompt(qp, kp, vp, gp, xp, w_ro, bp, 2)
    qs, ks, vs, gs = _ret_proj(xs, g_mix, w_cat, cos_s, sin_s, TM_SAMPLE, F32)
    gated_s, state_s = _ret_sample(qs, ks, vs, gs, state_ret[0], 2)
    xs = _linear(gated_s, w_ro, zero_bias, xs, TM_SAMPLE)

    w1 = ffn_w1[0].astype(BF16)
    w3 = ffn_w3[0].astype(BF16)
    w2 = ffn_w2[0].astype(BF16)
    g_ffn = norm_ffn[0].reshape(1, d)
    xp = _ffn(xp, g_ffn, w1, w3, w2, gamma_final, TM_PROJ, False)
    xs = _ffn(xs, g_ffn, w1, w3, w2, gamma_final, TM_SAMPLE, False)

    nq = SWA_HEADS * SWA_HEAD_DIM
    slab_heads = np.arange(SWA_HEADS).reshape(SWA_KV_HEADS, SWA_GROUP).T.reshape(-1)
    q_cols = (slab_heads[:, None] * SWA_HEAD_DIM + np.arange(SWA_HEAD_DIM)[None, :]).reshape(-1)
    qkv_cols = np.concatenate([q_cols, np.arange(nq, nq + 2 * nk)])
    w_qkv = swa_w_qkv[0][:, qkv_cols].astype(BF16)
    b_qkv = swa_b_qkv[0][qkv_cols].reshape(1, -1)
    w_so = swa_w_o[0][q_cols, :].astype(BF16)
    b_so = swa_b_o[0].reshape(1, d)
    sinks = swa_sinks[0]
    g_mix = norm_mix[1].reshape(1, d)
    qp, kvp = _swa_proj(xp, g_mix, w_qkv, b_qkv, TM_PROJ, BF16)
    xp = _swa_prompt(qp, kvp, xp, w_so, b_so, sinks, bp, 2)
    qs, kvs = _swa_proj(xs, g_mix, w_qkv, b_qkv, TM_SAMPLE, F32)
    att_s, new_k, new_v = _swa_sample(
        qs, kvs, cache_swa_k[0].reshape(bs, WINDOW, nk), cache_swa_v[0].reshape(bs, WINDOW, nk), sinks, 8)
    xs = _linear(att_s, w_so, b_so, xs, TM_SAMPLE)

    w1 = ffn_w1[1].astype(BF16)
    w3 = ffn_w3[1].astype(BF16)
    w2 = ffn_w2[1].astype(BF16)
    g_ffn = norm_ffn[1].reshape(1, d)
    yp = _ffn(xp, g_ffn, w1, w3, w2, gamma_final, TM_PROJ, True)
    ys = _ffn(xs, g_ffn, w1, w3, w2, gamma_final, TM_SAMPLE, True)

    kv_tail = kvp.reshape(bp, tp, 2 * nk)[:, tp - WINDOW:, :]
    kv_shape = (1, bp, WINDOW, SWA_KV_HEADS, SWA_HEAD_DIM)
    cache_shape = (1, bs, WINDOW, SWA_KV_HEADS, SWA_HEAD_DIM)
    return (
        yp.reshape(bp, tp, d),
        ys.reshape(bs, ts, d),
        state_p[None],
        state_s[None],
        kv_tail[..., :nk].reshape(kv_shape),
        kv_tail[..., nk:].reshape(kv_shape),
        new_k.reshape(cache_shape),
        new_v.reshape(cache_shape),
    )
```

```python
import functools
import math

import numpy as np
import jax
import jax.numpy as jnp
from jax import lax
from jax.experimental import pallas as pl
from jax.experimental.pallas import tpu as pltpu

F32 = jnp.float32
BF16 = jnp.bfloat16

D_MODEL = 1024
SEQ = 2048
DEC_SEQ = 8
PAST_LEN = 16384

RET_HEADS = 4
RET_DK = 256
RET_DV = 512
RET_CHUNK = 128
RET_PROMPT_CHUNK = 256
ROPE_BASE = 10000.0

SWA_HEADS = 16
SWA_KV_HEADS = 2
SWA_HEAD_DIM = 64
SWA_GROUP = SWA_HEADS // SWA_KV_HEADS
WINDOW = 128

D_FF = 2816
RMS_EPS = 1e-6
GN_EPS = 1e-6
NEG_INF = -1e30

LANES = 128
VMEM_LIMIT = 56 * 1024 * 1024


def _params(n_axes):
    return pltpu.CompilerParams(
        dimension_semantics=("arbitrary",) * n_axes, vmem_limit_bytes=VMEM_LIMIT)


def _rms(x, g):
    ms = jnp.mean(x * x, axis=-1, keepdims=True)
    return x * lax.rsqrt(ms + RMS_EPS) * g


def _silu(x):
    return x * jax.nn.sigmoid(x)


def _resident(shape):
    zeros = (0,) * len(shape)
    return pl.BlockSpec(shape, lambda *_: zeros, pipeline_mode=pl.Buffered(1))


def _ret_proj_kernel(x_ref, g_ref, w_ref, cos_ref, sin_ref, q_ref, k_ref, v_ref, gs_ref):
    h = _rms(x_ref[...], g_ref[...]).astype(BF16)
    cos = cos_ref[...]
    sin = sin_ref[...]
    half = RET_DK // 2
    nqk = RET_HEADS * RET_DK
    nv = RET_HEADS * RET_DV

    def rotated(col0, out_ref, scale):
        y = jnp.dot(h, w_ref[:, col0:col0 + nqk], preferred_element_type=F32)
        for hd in range(RET_HEADS):
            lo = hd * RET_DK
            a = y[:, lo:lo + half]
            b = y[:, lo + half:lo + RET_DK]
            out_ref[:, lo:lo + half] = ((a * cos - b * sin) * scale).astype(out_ref.dtype)
            out_ref[:, lo + half:lo + RET_DK] = ((a * sin + b * cos) * scale).astype(out_ref.dtype)

    rotated(0, q_ref, 1.0)
    rotated(nqk, k_ref, RET_DK ** -0.5)
    v_ref[...] = jnp.dot(h, w_ref[:, 2 * nqk:2 * nqk + nv],
                         preferred_element_type=F32).astype(v_ref.dtype)
    g = jnp.dot(h, w_ref[:, 2 * nqk + nv:2 * nqk + 2 * nv], preferred_element_type=F32)
    gs_ref[...] = _silu(g).astype(gs_ref.dtype)


def _ret_proj(x, gamma, w_cat, cos, sin, tm, out_dtype):
    n = x.shape[0]
    nqk = RET_HEADS * RET_DK
    nv = RET_HEADS * RET_DV
    n_pos_blocks = cos.shape[0] // tm
    row = lambda i: (i, 0)
    return pl.pallas_call(
        _ret_proj_kernel,
        grid=(n // tm,),
        in_specs=[
            pl.BlockSpec((tm, D_MODEL), row),
            _resident((1, D_MODEL)),
            _resident(w_cat.shape),
            pl.BlockSpec((tm, LANES), lambda i: (i % n_pos_blocks, 0)),
            pl.BlockSpec((tm, LANES), lambda i: (i % n_pos_blocks, 0)),
        ],
        out_specs=[
            pl.BlockSpec((tm, nqk), row),
            pl.BlockSpec((tm, nqk), row),
            pl.BlockSpec((tm, nv), row),
            pl.BlockSpec((tm, nv), row),
        ],
        out_shape=[
            jax.ShapeDtypeStruct((n, nqk), out_dtype),
            jax.ShapeDtypeStruct((n, nqk), out_dtype),
            jax.ShapeDtypeStruct((n, nv), out_dtype),
            jax.ShapeDtypeStruct((n, nv), out_dtype),
        ],
        compiler_params=_params(1),
        name="ret_proj",
    )(x, gamma, w_cat, cos, sin)


def _ret_chunk(qc, kc, vc, gsc, s_in, s_out, chunk, store_gated):
    cq = qc.shape[0]
    ck = kc.shape[0]
    ii = lax.broadcasted_iota(jnp.int32, (cq, ck), 0)
    jj = lax.broadcasted_iota(jnp.int32, (cq, ck), 1)
    diff = (ii - jj).astype(F32)
    irow = lax.broadcasted_iota(jnp.int32, (cq, 1), 0).astype(F32)
    jrow = lax.broadcasted_iota(jnp.int32, (ck, 1), 0).astype(F32)
    for hd in range(RET_HEADS):
        log_gamma = math.log1p(-(2.0 ** -(5.0 + hd)))
        intra = jnp.where(diff >= 0, jnp.exp(log_gamma * jnp.maximum(diff, 0.0)), 0.0)
        q_decay = jnp.exp(log_gamma * (irow + 1.0))
        k_decay = jnp.exp(log_gamma * (chunk - 1.0 - jrow))
        chunk_decay = math.exp(log_gamma * chunk)
        qh = qc[:, hd * RET_DK:(hd + 1) * RET_DK]
        kh = kc[:, hd * RET_DK:(hd + 1) * RET_DK]
        vh = vc[:, hd * RET_DV:(hd + 1) * RET_DV]
        s_h = s_in[hd]
        scores = lax.dot_general(qh, kh, (((1,), (1,)), ((), ())),
                                 preferred_element_type=F32) * intra
        o = jnp.dot(scores.astype(BF16), vh, preferred_element_type=F32)
        o = o + jnp.dot(qh, s_h.astype(BF16), preferred_element_type=F32) * q_decay
        kd = (kh.astype(F32) * k_decay).astype(BF16)
        s_out[hd] = s_h * chunk_decay + lax.dot_general(
            kd, vh, (((0,), (0,)), ((), ())), preferred_element_type=F32)
        mu = jnp.mean(o, axis=-1, keepdims=True)
        oc = o - mu
        var = jnp.mean(oc * oc, axis=-1, keepdims=True)
        on = oc * lax.rsqrt(var + GN_EPS)
        store_gated(hd, on * gsc[:, hd * RET_DV:(hd + 1) * RET_DV].astype(F32))


def _ret_core_kernel(q_ref, k_ref, v_ref, gs_ref, x_ref, wo_ref,
                     qs_ref, ks_ref, vs_ref, gss_ref, s0_ref,
                     y_ref, sp_ref, gated_s_ref, ss_ref, gated_ref, *, chunk, nb):
    @pl.when(pl.program_id(1) == 0)
    def _():
        sp_ref[...] = jnp.zeros_like(sp_ref)

    def store_prompt(hd, val):
        gated_ref[:, hd * RET_DV:(hd + 1) * RET_DV] = val.astype(BF16)

    _ret_chunk(q_ref[...], k_ref[...], v_ref[...], gs_ref[...],
               sp_ref.at[0], sp_ref.at[0], chunk, store_prompt)
    y_ref[...] = x_ref[...] + jnp.dot(gated_ref[...], wo_ref[...], preferred_element_type=F32)

    pad = RET_CHUNK - DEC_SEQ
    kpad = jnp.zeros((pad, ks_ref.shape[1]), F32)
    vpad = jnp.zeros((pad, vs_ref.shape[1]), F32)
    for b in range(nb):
        rows = slice(b * DEC_SEQ, (b + 1) * DEC_SEQ)
        kc = jnp.concatenate([ks_ref[rows, :], kpad], axis=0).astype(BF16)
        vc = jnp.concatenate([vs_ref[rows, :], vpad], axis=0).astype(BF16)

        def store_sample(hd, val, rows=rows):
            gated_s_ref[rows, hd * RET_DV:(hd + 1) * RET_DV] = val

        _ret_chunk(qs_ref[rows, :].astype(BF16), kc, vc, gss_ref[rows, :],
                   s0_ref.at[b], ss_ref.at[b], DEC_SEQ, store_sample)


def _ret_core(q, k, v, gs, x, wo, qs, ks, vs, gss, s0, batch, chunk):
    n = x.shape[0]
    steps = n // batch // chunk
    nb = s0.shape[0] // (batch * steps)
    assert nb * batch * steps == s0.shape[0]
    nqk = RET_HEADS * RET_DK
    nv = RET_HEADS * RET_DV
    ts = nb * DEC_SEQ
    row = lambda b, c: (b * steps + c, 0)
    state_s = pl.BlockSpec((nb, RET_HEADS, RET_DK, RET_DV), lambda b, c: (b * steps + c, 0, 0, 0))
    return pl.pallas_call(
        functools.partial(_ret_core_kernel, chunk=chunk, nb=nb),
        grid=(batch, steps),
        in_specs=[
            pl.BlockSpec((chunk, nqk), row),
            pl.BlockSpec((chunk, nqk), row),
            pl.BlockSpec((chunk, nv), row),
            pl.BlockSpec((chunk, nv), row),
            pl.BlockSpec((chunk, D_MODEL), row),
            _resident(wo.shape),
            pl.BlockSpec((ts, nqk), row),
            pl.BlockSpec((ts, nqk), row),
            pl.BlockSpec((ts, nv), row),
            pl.BlockSpec((ts, nv), row),
            state_s,
        ],
        out_specs=[
            pl.BlockSpec((chunk, D_MODEL), row),
            pl.BlockSpec((1, RET_HEADS, RET_DK, RET_DV), lambda b, c: (b, 0, 0, 0)),
            pl.BlockSpec((ts, nv), row),
            state_s,
        ],
        out_shape=[
            jax.ShapeDtypeStruct((n, D_MODEL), F32),
            jax.ShapeDtypeStruct((batch, RET_HEADS, RET_DK, RET_DV), F32),
            jax.ShapeDtypeStruct((s0.shape[0] * DEC_SEQ, nv), F32),
            jax.ShapeDtypeStruct(s0.shape, F32),
        ],
        scratch_shapes=[pltpu.VMEM((chunk, nv), BF16)],
        compiler_params=_params(2),
        name="ret_core",
    )(q, k, v, gs, x, wo, qs, ks, vs, gss, s0)


def _linear_kernel(a_ref, w_ref, b_ref, res_ref, y_ref):
    y = jnp.dot(a_ref[...].astype(BF16), w_ref[...], preferred_element_type=F32)
    y_ref[...] = res_ref[...] + y + b_ref[...]


def _linear(a, w, bias, res, tm):
    n, kdim = a.shape
    row = lambda i: (i, 0)
    return pl.pallas_call(
        _linear_kernel,
        grid=(n // tm,),
        in_specs=[
            pl.BlockSpec((tm, kdim), row),
            _resident(w.shape),
            _resident(bias.shape),
            pl.BlockSpec((tm, w.shape[1]), row),
        ],
        out_specs=pl.BlockSpec((tm, w.shape[1]), row),
        out_shape=jax.ShapeDtypeStruct((n, w.shape[1]), F32),
        compiler_params=_params(1),
        name="linear_residual",
    )(a, w, bias, res)


def _ffn_kernel(x_ref, g_ref, w1_ref, w3_ref, w2_ref, gf_ref, y_ref, *, final_norm):
    x = x_ref[...]
    h = _rms(x, g_ref[...]).astype(BF16)
    a = jnp.dot(h, w1_ref[...], preferred_element_type=F32)
    b = jnp.dot(h, w3_ref[...], preferred_element_type=F32)
    act = (_silu(a) * b).astype(BF16)
    y = x + jnp.dot(act, w2_ref[...], preferred_element_type=F32)
    if final_norm:
        y = _rms(y, gf_ref[...])
    y_ref[...] = y


def _ffn(x, gamma, w1, w3, w2, gamma_final, tm, final_norm):
    n = x.shape[0]
    row = lambda i: (i, 0)
    return pl.pallas_call(
        functools.partial(_ffn_kernel, final_norm=final_norm),
        grid=(n // tm,),
        in_specs=[
            pl.BlockSpec((tm, D_MODEL), row),
            _resident((1, D_MODEL)),
            _resident(w1.shape),
            _resident(w3.shape),
            _resident(w2.shape),
            _resident((1, D_MODEL)),
        ],
        out_specs=pl.BlockSpec((tm, D_MODEL), row),
        out_shape=jax.ShapeDtypeStruct((n, D_MODEL), F32),
        compiler_params=_params(1),
        name="ffn",
    )(x, gamma, w1, w3, w2, gamma_final)


def _swa_proj_kernel(x_ref, g_ref, w_ref, b_ref, q_ref, kv_ref):
    h = _rms(x_ref[...], g_ref[...]).astype(BF16)
    nq = SWA_HEADS * SWA_HEAD_DIM
    y = jnp.dot(h, w_ref[...], preferred_element_type=F32) + b_ref[...]
    q_ref[...] = (y[:, :nq] * (SWA_HEAD_DIM ** -0.5)).astype(q_ref.dtype)
    kv_ref[...] = y[:, nq:]


def _swa_proj(x, gamma, w, bias, tm, q_dtype):
    n = x.shape[0]
    nq = SWA_HEADS * SWA_HEAD_DIM
    nkv = 2 * SWA_KV_HEADS * SWA_HEAD_DIM
    row = lambda i: (i, 0)
    return pl.pallas_call(
        _swa_proj_kernel,
        grid=(n // tm,),
        in_specs=[
            pl.BlockSpec((tm, D_MODEL), row),
            _resident((1, D_MODEL)),
            _resident(w.shape),
            _resident(bias.shape),
        ],
        out_specs=[pl.BlockSpec((tm, nq), row), pl.BlockSpec((tm, nkv), row)],
        out_shape=[
            jax.ShapeDtypeStruct((n, nq), q_dtype),
            jax.ShapeDtypeStruct((n, nkv), F32),
        ],
        compiler_params=_params(1),
        name="swa_proj",
    )(x, gamma, w, bias)


def _swa_attend(q, k2, v2, prev_valid, sinks_ref):
    r = q.shape[0]
    hd = SWA_HEAD_DIM
    lane_lo = lax.broadcasted_iota(jnp.int32, (r, LANES), 1) < hd
    zero = jnp.zeros((r, LANES), q.dtype)
    pieces = []
    sink_pieces = []
    for j in range(SWA_GROUP):
        slab = q[:, j * LANES:(j + 1) * LANES]
        pieces.append(jnp.where(lane_lo, slab, zero))
        pieces.append(jnp.where(lane_lo, zero, slab))
        sink_pieces.append(jnp.full((r, 1), sinks_ref[j], F32))
        sink_pieces.append(jnp.full((r, 1), sinks_ref[SWA_GROUP + j], F32))
    qs = jnp.concatenate(pieces, axis=0).astype(BF16)
    sink = jnp.concatenate(sink_pieces, axis=0)
    rows = SWA_HEADS * r

    s2 = lax.dot_general(qs, k2, (((1,), (1,)), ((), ())), preferred_element_type=F32)
    qi = lax.broadcasted_iota(jnp.int32, (rows, WINDOW), 0) & (r - 1)
    kj = lax.broadcasted_iota(jnp.int32, (rows, WINDOW), 1)
    cur = kj <= qi
    prev_cap = jnp.where(prev_valid, jnp.float32(3.0e38), jnp.float32(NEG_INF))
    s = jnp.where(cur, s2[:, WINDOW:], jnp.minimum(s2[:, :WINDOW], prev_cap))
    m = jnp.maximum(jnp.max(s, axis=-1, keepdims=True), sink)
    p = jnp.exp(s - m)
    den = jnp.sum(p, axis=-1, keepdims=True) + jnp.exp(sink - m)
    p2 = jnp.concatenate([jnp.where(cur, 0.0, p), jnp.where(cur, p, 0.0)], axis=1).astype(BF16)
    o = jnp.dot(p2, v2, preferred_element_type=F32) / den
    outs = [jnp.where(lane_lo, o[(2 * j) * r:(2 * j + 1) * r], o[(2 * j + 1) * r:(2 * j + 2) * r])
            for j in range(SWA_GROUP)]
    return jnp.concatenate(outs, axis=1)


def _swa_prompt_kernel(sinks_ref, q_ref, kvp_ref, kvc_ref, x_ref, wo_ref, bo_ref, y_ref, o_ref,
                       *, n_sub):
    nk = SWA_KV_HEADS * SWA_HEAD_DIM
    r = WINDOW
    lane_lo = lax.broadcasted_iota(jnp.int32, (r, LANES), 1) < SWA_HEAD_DIM
    qi = lax.broadcasted_iota(jnp.int32, (r, WINDOW), 0)
    kj = lax.broadcasted_iota(jnp.int32, (r, WINDOW), 1)
    cur_mask = kj <= qi
    ones = jnp.ones((2 * WINDOW, LANES), BF16)
    nt = (((1,), (1,)), ((), ()))
    for c in range(n_sub):
        rows = slice(c * WINDOW, (c + 1) * WINDOW)
        prev = kvp_ref[...] if c == 0 else kvc_ref[(c - 1) * WINDOW:c * WINDOW, :]
        cur = kvc_ref[rows, :]
        k2 = jnp.concatenate([prev[:, :nk], cur[:, :nk]], axis=0).astype(BF16)
        v2 = jnp.concatenate([prev[:, nk:], cur[:, nk:]], axis=0).astype(BF16)
        v2_ones = jnp.concatenate([v2, ones], axis=1)
        prev_valid = (pl.program_id(1) > 0) if c == 0 else True
        prev_cap = jnp.where(prev_valid, jnp.float32(3.0e38), jnp.float32(NEG_INF))
        for j in range(SWA_GROUP):
            slab = q_ref[rows, j * LANES:(j + 1) * LANES]
            zero = jnp.zeros_like(slab)
            num = []
            den = []
            for side in range(2):
                qm = jnp.where(lane_lo, slab, zero) if side == 0 else jnp.where(lane_lo, zero, slab)
                s2 = lax.dot_general(qm, k2, nt, preferred_element_type=F32)
                s = jnp.where(cur_mask, s2[:, WINDOW:], jnp.minimum(s2[:, :WINDOW], prev_cap))
                sink = sinks_ref[j + side * SWA_GROUP]
                m = jnp.maximum(jnp.max(s, axis=-1, keepdims=True), sink)
                p = jnp.exp(s - m)
                p2 = jnp.concatenate([jnp.where(cur_mask, 0.0, p), jnp.where(cur_mask, p, 0.0)],
                                     axis=1).astype(BF16)
                o2 = jnp.dot(p2, v2_ones, preferred_element_type=F32)
                num.append(o2[:, :LANES])
                den.append(o2[:, LANES:] + jnp.exp(sink - m))
            o_ref[rows, j * LANES:(j + 1) * LANES] = (
                jnp.where(lane_lo, num[0], num[1]) / jnp.where(lane_lo, den[0], den[1])).astype(BF16)
    y = jnp.dot(o_ref[...], wo_ref[...], preferred_element_type=F32)
    y_ref[...] = x_ref[...] + y + bo_ref[...]


def _swa_prompt(q, kv, x, wo, bo, sinks, batch, n_sub):
    n = x.shape[0]
    tm = n_sub * WINDOW
    steps = n // batch // tm
    nq = SWA_HEADS * SWA_HEAD_DIM
    nkv = kv.shape[1]
    row = lambda b, i: (b * steps + i, 0)
    prev = lambda b, i: ((b * steps + i) * n_sub - jnp.minimum(i, 1), 0)
    return pl.pallas_call(
        functools.partial(_swa_prompt_kernel, n_sub=n_sub),
        grid=(batch, steps),
        in_specs=[
            pl.BlockSpec(memory_space=pltpu.SMEM),
            pl.BlockSpec((tm, nq), row),
            pl.BlockSpec((WINDOW, nkv), prev),
            pl.BlockSpec((tm, nkv), row),
            pl.BlockSpec((tm, D_MODEL), row),
            _resident(wo.shape),
            _resident(bo.shape),
        ],
        out_specs=pl.BlockSpec((tm, D_MODEL), row),
        out_shape=jax.ShapeDtypeStruct((n, D_MODEL), F32),
        scratch_shapes=[pltpu.VMEM((tm, nq), BF16)],
        compiler_params=_params(2),
        name="swa_prompt",
    )(sinks, q, kv, kv, x, wo, bo)


def _swa_sample_kernel(sinks_ref, q_ref, kvn_ref, ck_ref, cv_ref, o_ref, nk_ref, nv_ref, *, nb):
    nk = SWA_KV_HEADS * SWA_HEAD_DIM
    t = DEC_SEQ
    zpad = jnp.zeros((WINDOW - t, nk), F32)
    for b in range(nb):
        rows = slice(b * t, (b + 1) * t)
        k_new = kvn_ref[rows, :nk]
        v_new = kvn_ref[rows, nk:]
        ck = ck_ref[b]
        cv = cv_ref[b]
        k2 = jnp.concatenate([ck, k_new, zpad], axis=0).astype(BF16)
        v2 = jnp.concatenate([cv, v_new, zpad], axis=0).astype(BF16)
        o_ref[rows, :] = _swa_attend(q_ref[rows, :], k2, v2, True, sinks_ref)
        nk_ref[b, :WINDOW - t, :] = ck[t:, :]
        nk_ref[b, WINDOW - t:, :] = k_new
        nv_ref[b, :WINDOW - t, :] = cv[t:, :]
        nv_ref[b, WINDOW - t:, :] = v_new


def _swa_sample(q, kv_new, cache_k, cache_v, sinks, nb):
    batch = cache_k.shape[0]
    nq = SWA_HEADS * SWA_HEAD_DIM
    nk = SWA_KV_HEADS * SWA_HEAD_DIM
    row = lambda b: (b, 0)
    cache = pl.BlockSpec((nb, WINDOW, nk), lambda b: (b, 0, 0))
    return pl.pallas_call(
        functools.partial(_swa_sample_kernel, nb=nb),
        grid=(batch // nb,),
        in_specs=[
            pl.BlockSpec(memory_space=pltpu.SMEM),
            pl.BlockSpec((nb * DEC_SEQ, nq), row),
            pl.BlockSpec((nb * DEC_SEQ, 2 * nk), row),
            cache,
            cache,
        ],
        out_specs=[pl.BlockSpec((nb * DEC_SEQ, nq), row), cache, cache],
        out_shape=[
            jax.ShapeDtypeStruct((batch * DEC_SEQ, nq), F32),
            jax.ShapeDtypeStruct(cache_k.shape, F32),
            jax.ShapeDtypeStruct(cache_v.shape, F32),
        ],
        compiler_params=_params(1),
        name="swa_sample",
    )(sinks, q, kv_new, cache_k, cache_v)


def _rotation_tables(positions, repeat):
    half = RET_DK // 2
    inv = 1.0 / (ROPE_BASE ** (np.arange(half, dtype=np.float64) / half))
    ang = np.asarray(positions, np.float64)[:, None] * inv[None, :]
    cos = np.tile(np.cos(ang), (repeat, 1)).astype(np.float32)
    sin = np.tile(np.sin(ang), (repeat, 1)).astype(np.float32)
    return jnp.asarray(cos), jnp.asarray(sin)


TM_PROJ = 512
TM_SAMPLE = 256


def kernel(x_prompt, x_sample, state_ret, cache_swa_k, cache_swa_v, ret_w_q, ret_w_k, ret_w_v, ret_w_g, ret_w_o, swa_w_qkv, swa_b_qkv, swa_w_o, swa_b_o, swa_sinks, norm_mix, norm_ffn, ffn_w1, ffn_w3, ffn_w2, norm_final):
    bp, tp, d = x_prompt.shape
    bs, ts, _ = x_sample.shape
    assert (tp, d, ts) == (SEQ, D_MODEL, DEC_SEQ)
    xp = x_prompt.reshape(bp * tp, d)
    xs = x_sample.reshape(bs * ts, d)
    nk = SWA_KV_HEADS * SWA_HEAD_DIM

    cos_p, sin_p = _rotation_tables(np.arange(tp), 1)
    cos_s, sin_s = _rotation_tables(PAST_LEN + np.arange(ts), TM_SAMPLE // ts)
    gamma_final = norm_final.reshape(1, d)
    zero_bias = jnp.zeros((1, d), F32)

    w_cat = jnp.concatenate([ret_w_q[0], ret_w_k[0], ret_w_v[0], ret_w_g[0]], axis=1).astype(BF16)
    w_ro = ret_w_o[0].astype(BF16)
    g_mix = norm_mix[0].reshape(1, d)
    qp, kp, vp, gp = _ret_proj(xp, g_mix, w_cat, cos_p, sin_p, TM_PROJ, BF16)
    qs, ks, vs, gs = _ret_proj(xs, g_mix, w_cat, cos_s, sin_s, TM_SAMPLE, F32)
    xp, state_p, gated_s, state_s = _ret_core(
        qp, kp, vp, gp, xp, w_ro, qs, ks, vs, gs, state_ret[0], bp, RET_PROMPT_CHUNK)
    xs = _linear(gated_s, w_ro, zero_bias, xs, TM_SAMPLE)

    w1 = ffn_w1[0].astype(BF16)
    w3 = ffn_w3[0].astype(BF16)
    w2 = ffn_w2[0].astype(BF16)
    g_ffn = norm_ffn[0].reshape(1, d)
    xp = _ffn(xp, g_ffn, w1, w3, w2, gamma_final, TM_PROJ, False)
    xs = _ffn(xs, g_ffn, w1, w3, w2, gamma_final, TM_SAMPLE, False)

    nq = SWA_HEADS * SWA_HEAD_DIM
    slab_heads = np.arange(SWA_HEADS).reshape(SWA_KV_HEADS, SWA_GROUP).T.reshape(-1)
    q_cols = (slab_heads[:, None] * SWA_HEAD_DIM + np.arange(SWA_HEAD_DIM)[None, :]).reshape(-1)
    qkv_cols = np.concatenate([q_cols, np.arange(nq, nq + 2 * nk)])
    w_qkv = swa_w_qkv[0][:, qkv_cols].astype(BF16)
    b_qkv = swa_b_qkv[0][qkv_cols].reshape(1, -1)
    w_so = swa_w_o[0][q_cols, :].astype(BF16)
    b_so = swa_b_o[0].reshape(1, d)
    sinks = swa_sinks[0]
    g_mix = norm_mix[1].reshape(1, d)
    qp, kvp = _swa_proj(xp, g_mix, w_qkv, b_qkv, TM_PROJ, BF16)
    xp = _swa_prompt(qp, kvp, xp, w_so, b_so, sinks, bp, 2)
    qs, kvs = _swa_proj(xs, g_mix, w_qkv, b_qkv, TM_SAMPLE, F32)
    att_s, new_k, new_v = _swa_sample(
        qs, kvs, cache_swa_k[0].reshape(bs, WINDOW, nk), cache_swa_v[0].reshape(bs, WINDOW, nk), sinks, 8)
    xs = _linear(att_s, w_so, b_so, xs, TM_SAMPLE)

    w1 = ffn_w1[1].astype(BF16)
    w3 = ffn_w3[1].astype(BF16)
    w2 = ffn_w2[1].astype(BF16)
    g_ffn = norm_ffn[1].reshape(1, d)
    yp = _ffn(xp, g_ffn, w1, w3, w2, gamma_final, TM_PROJ, True)
    ys = _ffn(xs, g_ffn, w1, w3, w2, gamma_final, TM_SAMPLE, True)

    kv_tail = kvp.reshape(bp, tp, 2 * nk)[:, tp - WINDOW:, :]
    kv_shape = (1, bp, WINDOW, SWA_KV_HEADS, SWA_HEAD_DIM)
    cache_shape = (1, bs, WINDOW, SWA_KV_HEADS, SWA_HEAD_DIM)
    return (
        yp.reshape(bp, tp, d),
        ys.reshape(bs, ts, d),
        state_p[None],
        state_s[None],
        kv_tail[..., :nk].reshape(kv_shape),
        kv_tail[..., nk:].reshape(kv_shape),
        new_k.reshape(cache_shape),
        new_v.reshape(cache_shape),
    )
```

```python
import functools
import math

import numpy as np
import jax
import jax.numpy as jnp
from jax import lax
from jax.experimental import pallas as pl
from jax.experimental.pallas import tpu as pltpu

F32 = jnp.float32
BF16 = jnp.bfloat16

D_MODEL = 1024
SEQ = 2048
DEC_SEQ = 8
PAST_LEN = 16384

RET_HEADS = 4
RET_DK = 256
RET_DV = 512
RET_CHUNK = 128
RET_PROMPT_CHUNK = 256
ROPE_BASE = 10000.0

SWA_HEADS = 16
SWA_KV_HEADS = 2
SWA_HEAD_DIM = 64
SWA_GROUP = SWA_HEADS // SWA_KV_HEADS
WINDOW = 128

D_FF = 2816
RMS_EPS = 1e-6
GN_EPS = 1e-6
NEG_INF = -1e30

LANES = 128
VMEM_LIMIT = 56 * 1024 * 1024


def _params(n_axes):
    return pltpu.CompilerParams(
        dimension_semantics=("arbitrary",) * n_axes, vmem_limit_bytes=VMEM_LIMIT)


def _rms(x, g):
    ms = jnp.mean(x * x, axis=-1, keepdims=True)
    return x * lax.rsqrt(ms + RMS_EPS) * g


def _silu(x):
    return x * jax.nn.sigmoid(x)


def _resident(shape):
    zeros = (0,) * len(shape)
    return pl.BlockSpec(shape, lambda *_: zeros, pipeline_mode=pl.Buffered(1))


def _layer_resident(stacked_shape, layer):
    return pl.BlockSpec((None,) + tuple(stacked_shape[1:]), lambda *_: (layer, 0, 0),
                        pipeline_mode=pl.Buffered(1))


def _ret_proj_kernel(x_ref, g_ref, wq_ref, wk_ref, wv_ref, wg_ref, cos_ref, sin_ref,
                     q_ref, k_ref, v_ref, gs_ref):
    h = _rms(x_ref[...], g_ref[...]).astype(BF16)
    cos = cos_ref[...]
    sin = sin_ref[...]
    half = RET_DK // 2

    def rotated(w_ref, out_ref, scale):
        y = jnp.dot(h, w_ref[...], preferred_element_type=F32)
        for hd in range(RET_HEADS):
            lo = hd * RET_DK
            a = y[:, lo:lo + half]
            b = y[:, lo + half:lo + RET_DK]
            out_ref[:, lo:lo + half] = ((a * cos - b * sin) * scale).astype(out_ref.dtype)
            out_ref[:, lo + half:lo + RET_DK] = ((a * sin + b * cos) * scale).astype(out_ref.dtype)

    rotated(wq_ref, q_ref, 1.0)
    rotated(wk_ref, k_ref, RET_DK ** -0.5)
    v_ref[...] = jnp.dot(h, wv_ref[...], preferred_element_type=F32).astype(v_ref.dtype)
    g = jnp.dot(h, wg_ref[...], preferred_element_type=F32)
    gs_ref[...] = _silu(g).astype(gs_ref.dtype)


def _ret_proj(x, gamma, wq, wk, wv, wg, layer, cos, sin, tm, out_dtype):
    n = x.shape[0]
    nqk = RET_HEADS * RET_DK
    nv = RET_HEADS * RET_DV
    n_pos_blocks = cos.shape[0] // tm
    row = lambda i: (i, 0)
    return pl.pallas_call(
        _ret_proj_kernel,
        grid=(n // tm,),
        in_specs=[
            pl.BlockSpec((tm, D_MODEL), row),
            _resident((1, D_MODEL)),
            _layer_resident(wq.shape, layer),
            _layer_resident(wk.shape, layer),
            _layer_resident(wv.shape, layer),
            _layer_resident(wg.shape, layer),
            pl.BlockSpec((tm, LANES), lambda i: (i % n_pos_blocks, 0)),
            pl.BlockSpec((tm, LANES), lambda i: (i % n_pos_blocks, 0)),
        ],
        out_specs=[
            pl.BlockSpec((tm, nqk), row),
            pl.BlockSpec((tm, nqk), row),
            pl.BlockSpec((tm, nv), row),
            pl.BlockSpec((tm, nv), row),
        ],
        out_shape=[
            jax.ShapeDtypeStruct((n, nqk), out_dtype),
            jax.ShapeDtypeStruct((n, nqk), out_dtype),
            jax.ShapeDtypeStruct((n, nv), out_dtype),
            jax.ShapeDtypeStruct((n, nv), out_dtype),
        ],
        compiler_params=_params(1),
        name="ret_proj",
    )(x, gamma, wq, wk, wv, wg, cos, sin)


def _ret_chunk(qc, kc, vc, gsc, s_in, s_out, chunk, store_gated):
    cq = qc.shape[0]
    ck = kc.shape[0]
    ii = lax.broadcasted_iota(jnp.int32, (cq, ck), 0)
    jj = lax.broadcasted_iota(jnp.int32, (cq, ck), 1)
    diff = (ii - jj).astype(F32)
    irow = lax.broadcasted_iota(jnp.int32, (cq, 1), 0).astype(F32)
    jrow = lax.broadcasted_iota(jnp.int32, (ck, 1), 0).astype(F32)
    for hd in range(RET_HEADS):
        log_gamma = math.log1p(-(2.0 ** -(5.0 + hd)))
        intra = jnp.where(diff >= 0, jnp.exp(log_gamma * jnp.maximum(diff, 0.0)), 0.0)
        q_decay = jnp.exp(log_gamma * (irow + 1.0))
        k_decay = jnp.exp(log_gamma * (chunk - 1.0 - jrow))
        chunk_decay = math.exp(log_gamma * chunk)
        qh = qc[:, hd * RET_DK:(hd + 1) * RET_DK]
        kh = kc[:, hd * RET_DK:(hd + 1) * RET_DK]
        vh = vc[:, hd * RET_DV:(hd + 1) * RET_DV]
        s_h = s_in[hd]
        scores = lax.dot_general(qh, kh, (((1,), (1,)), ((), ())),
                                 preferred_element_type=F32) * intra
        o = jnp.dot(scores.astype(BF16), vh, preferred_element_type=F32)
        o = o + jnp.dot(qh, s_h.astype(BF16), preferred_element_type=F32) * q_decay
        kd = (kh.astype(F32) * k_decay).astype(BF16)
        s_out[hd] = s_h * chunk_decay + lax.dot_general(
            kd, vh, (((0,), (0,)), ((), ())), preferred_element_type=F32)
        mu = jnp.mean(o, axis=-1, keepdims=True)
        oc = o - mu
        var = jnp.mean(oc * oc, axis=-1, keepdims=True)
        on = oc * lax.rsqrt(var + GN_EPS)
        store_gated(hd, on * gsc[:, hd * RET_DV:(hd + 1) * RET_DV].astype(F32))


def _ret_core_kernel(q_ref, k_ref, v_ref, gs_ref, x_ref, wo_ref,
                     qs_ref, ks_ref, vs_ref, gss_ref, s0_ref,
                     y_ref, sp_ref, gated_s_ref, ss_ref, gated_ref, *, chunk, nb):
    @pl.when(pl.program_id(1) == 0)
    def _():
        sp_ref[...] = jnp.zeros_like(sp_ref)

    def store_prompt(hd, val):
        gated_ref[:, hd * RET_DV:(hd + 1) * RET_DV] = val.astype(BF16)

    _ret_chunk(q_ref[...], k_ref[...], v_ref[...], gs_ref[...],
               sp_ref.at[0], sp_ref.at[0], chunk, store_prompt)
    y_ref[...] = x_ref[...] + jnp.dot(gated_ref[...], wo_ref[...], preferred_element_type=F32)

    pad = RET_CHUNK - DEC_SEQ
    kpad = jnp.zeros((pad, ks_ref.shape[1]), F32)
    vpad = jnp.zeros((pad, vs_ref.shape[1]), F32)
    for b in range(nb):
        rows = slice(b * DEC_SEQ, (b + 1) * DEC_SEQ)
        kc = jnp.concatenate([ks_ref[rows, :], kpad], axis=0).astype(BF16)
        vc = jnp.concatenate([vs_ref[rows, :], vpad], axis=0).astype(BF16)

        def store_sample(hd, val, rows=rows):
            gated_s_ref[rows, hd * RET_DV:(hd + 1) * RET_DV] = val

        _ret_chunk(qs_ref[rows, :].astype(BF16), kc, vc, gss_ref[rows, :],
                   s0_ref.at[b], ss_ref.at[b], DEC_SEQ, store_sample)


def _ret_core(q, k, v, gs, x, wo, layer, qs, ks, vs, gss, s0, batch, chunk):
    n = x.shape[0]
    steps = n // batch // chunk
    nb = s0.shape[0] // (batch * steps)
    assert nb * batch * steps == s0.shape[0]
    nqk = RET_HEADS * RET_DK
    nv = RET_HEADS * RET_DV
    ts = nb * DEC_SEQ
    row = lambda b, c: (b * steps + c, 0)
    state_s = pl.BlockSpec((nb, RET_HEADS, RET_DK, RET_DV), lambda b, c: (b * steps + c, 0, 0, 0))
    return pl.pallas_call(
        functools.partial(_ret_core_kernel, chunk=chunk, nb=nb),
        grid=(batch, steps),
        in_specs=[
            pl.BlockSpec((chunk, nqk), row),
            pl.BlockSpec((chunk, nqk), row),
            pl.BlockSpec((chunk, nv), row),
            pl.BlockSpec((chunk, nv), row),
            pl.BlockSpec((chunk, D_MODEL), row),
            _layer_resident(wo.shape, layer),
            pl.BlockSpec((ts, nqk), row),
            pl.BlockSpec((ts, nqk), row),
            pl.BlockSpec((ts, nv), row),
            pl.BlockSpec((ts, nv), row),
            state_s,
        ],
        out_specs=[
            pl.BlockSpec((chunk, D_MODEL), row),
            pl.BlockSpec((1, RET_HEADS, RET_DK, RET_DV), lambda b, c: (b, 0, 0, 0)),
            pl.BlockSpec((ts, nv), row),
            state_s,
        ],
        out_shape=[
            jax.ShapeDtypeStruct((n, D_MODEL), F32),
            jax.ShapeDtypeStruct((batch, RET_HEADS, RET_DK, RET_DV), F32),
            jax.ShapeDtypeStruct((s0.shape[0] * DEC_SEQ, nv), F32),
            jax.ShapeDtypeStruct(s0.shape, F32),
        ],
        scratch_shapes=[pltpu.VMEM((chunk, nv), BF16)],
        compiler_params=_params(2),
        name="ret_core",
    )(q, k, v, gs, x, wo, qs, ks, vs, gss, s0)


def _linear_kernel(a_ref, w_ref, b_ref, res_ref, y_ref):
    y = jnp.dot(a_ref[...].astype(BF16), w_ref[...], preferred_element_type=F32)
    y_ref[...] = res_ref[...] + y + b_ref[...]


def _linear(a, w, bias, res, tm):
    n, kdim = a.shape
    row = lambda i: (i, 0)
    return pl.pallas_call(
        _linear_kernel,
        grid=(n // tm,),
        in_specs=[
            pl.BlockSpec((tm, kdim), row),
            _resident(w.shape),
            _resident(bias.shape),
            pl.BlockSpec((tm, w.shape[1]), row),
        ],
        out_specs=pl.BlockSpec((tm, w.shape[1]), row),
        out_shape=jax.ShapeDtypeStruct((n, w.shape[1]), F32),
        compiler_params=_params(1),
        name="linear_residual",
    )(a, w, bias, res)


def _ffn_kernel(x_ref, g_ref, w1_ref, w3_ref, w2_ref, gf_ref, y_ref, *, final_norm):
    x = x_ref[...]
    h = _rms(x, g_ref[...]).astype(BF16)
    a = jnp.dot(h, w1_ref[...], preferred_element_type=F32)
    b = jnp.dot(h, w3_ref[...], preferred_element_type=F32)
    act = (_silu(a) * b).astype(BF16)
    y = x + jnp.dot(act, w2_ref[...], preferred_element_type=F32)
    if final_norm:
        y = _rms(y, gf_ref[...])
    y_ref[...] = y


def _ffn(x, gamma, w1, w3, w2, layer, gamma_final, tm, final_norm):
    n = x.shape[0]
    row = lambda i: (i, 0)
    return pl.pallas_call(
        functools.partial(_ffn_kernel, final_norm=final_norm),
        grid=(n // tm,),
        in_specs=[
            pl.BlockSpec((tm, D_MODEL), row),
            _resident((1, D_MODEL)),
            _layer_resident(w1.shape, layer),
            _layer_resident(w3.shape, layer),
            _layer_resident(w2.shape, layer),
            _resident((1, D_MODEL)),
        ],
        out_specs=pl.BlockSpec((tm, D_MODEL), row),
        out_shape=jax.ShapeDtypeStruct((n, D_MODEL), F32),
        compiler_params=_params(1),
        name="ffn",
    )(x, gamma, w1, w3, w2, gamma_final)


def _swa_proj_kernel(x_ref, g_ref, wq_ref, bq_ref, wkv_ref, bkv_ref, q_ref, kv_ref):
    h = _rms(x_ref[...], g_ref[...]).astype(BF16)
    q = jnp.dot(h, wq_ref[...], preferred_element_type=F32) + bq_ref[...]
    q_ref[...] = (q * (SWA_HEAD_DIM ** -0.5)).astype(q_ref.dtype)
    kv_ref[...] = jnp.dot(h, wkv_ref[...], preferred_element_type=F32) + bkv_ref[...]


def _swa_proj(x, gamma, wq, bq, wkv, bkv, tm, q_dtype):
    n = x.shape[0]
    nq = SWA_HEADS * SWA_HEAD_DIM
    nkv = 2 * SWA_KV_HEADS * SWA_HEAD_DIM
    row = lambda i: (i, 0)
    return pl.pallas_call(
        _swa_proj_kernel,
        grid=(n // tm,),
        in_specs=[
            pl.BlockSpec((tm, D_MODEL), row),
            _resident((1, D_MODEL)),
            _resident(wq.shape),
            _resident(bq.shape),
            _resident(wkv.shape),
            _resident(bkv.shape),
        ],
        out_specs=[pl.BlockSpec((tm, nq), row), pl.BlockSpec((tm, nkv), row)],
        out_shape=[
            jax.ShapeDtypeStruct((n, nq), q_dtype),
            jax.ShapeDtypeStruct((n, nkv), F32),
        ],
        compiler_params=_params(1),
        name="swa_proj",
    )(x, gamma, wq, bq, wkv, bkv)


def _swa_attend(q, k2, v2, prev_valid, sinks_ref):
    r = q.shape[0]
    hd = SWA_HEAD_DIM
    lane_lo = lax.broadcasted_iota(jnp.int32, (r, LANES), 1) < hd
    zero = jnp.zeros((r, LANES), q.dtype)
    pieces = []
    sink_pieces = []
    for j in range(SWA_GROUP):
        slab = q[:, j * LANES:(j + 1) * LANES]
        pieces.append(jnp.where(lane_lo, slab, zero))
        pieces.append(jnp.where(lane_lo, zero, slab))
        sink_pieces.append(jnp.full((r, 1), sinks_ref[j], F32))
        sink_pieces.append(jnp.full((r, 1), sinks_ref[SWA_GROUP + j], F32))
    qs = jnp.concatenate(pieces, axis=0).astype(BF16)
    sink = jnp.concatenate(sink_pieces, axis=0)
    rows = SWA_HEADS * r

    s2 = lax.dot_general(qs, k2, (((1,), (1,)), ((), ())), preferred_element_type=F32)
    qi = lax.broadcasted_iota(jnp.int32, (rows, WINDOW), 0) & (r - 1)
    kj = lax.broadcasted_iota(jnp.int32, (rows, WINDOW), 1)
    cur = kj <= qi
    prev_cap = jnp.where(prev_valid, jnp.float32(3.0e38), jnp.float32(NEG_INF))
    s = jnp.where(cur, s2[:, WINDOW:], jnp.minimum(s2[:, :WINDOW], prev_cap))
    m = jnp.maximum(jnp.max(s, axis=-1, keepdims=True), sink)
    p = jnp.exp(s - m)
    den = jnp.sum(p, axis=-1, keepdims=True) + jnp.exp(sink - m)
    p2 = jnp.concatenate([jnp.where(cur, 0.0, p), jnp.where(cur, p, 0.0)], axis=1).astype(BF16)
    o = jnp.dot(p2, v2, preferred_element_type=F32) / den
    outs = [jnp.where(lane_lo, o[(2 * j) * r:(2 * j + 1) * r], o[(2 * j + 1) * r:(2 * j + 2) * r])
            for j in range(SWA_GROUP)]
    return jnp.concatenate(outs, axis=1)


def _swa_prompt_kernel(sinks_ref, q_ref, kvp_ref, kvc_ref, x_ref, wo_ref, bo_ref, y_ref, o_ref,
                       *, n_sub):
    nk = SWA_KV_HEADS * SWA_HEAD_DIM
    r = WINDOW
    lane_lo = lax.broadcasted_iota(jnp.int32, (r, LANES), 1) < SWA_HEAD_DIM
    qi = lax.broadcasted_iota(jnp.int32, (r, WINDOW), 0)
    kj = lax.broadcasted_iota(jnp.int32, (r, WINDOW), 1)
    cur_mask = kj <= qi
    ones = jnp.ones((2 * WINDOW, LANES), BF16)
    nt = (((1,), (1,)), ((), ()))
    for c in range(n_sub):
        rows = slice(c * WINDOW, (c + 1) * WINDOW)
        prev = kvp_ref[...] if c == 0 else kvc_ref[(c - 1) * WINDOW:c * WINDOW, :]
        cur = kvc_ref[rows, :]
        k2 = jnp.concatenate([prev[:, :nk], cur[:, :nk]], axis=0).astype(BF16)
        v2 = jnp.concatenate([prev[:, nk:], cur[:, nk:]], axis=0).astype(BF16)
        v2_ones = jnp.concatenate([v2, ones], axis=1)
        prev_valid = (pl.program_id(1) > 0) if c == 0 else True
        prev_cap = jnp.where(prev_valid, jnp.float32(3.0e38), jnp.float32(NEG_INF))
        for j in range(SWA_GROUP):
            slab = q_ref[rows, j * LANES:(j + 1) * LANES]
            zero = jnp.zeros_like(slab)
            num = []
            den = []
            for side in range(2):
                qm = jnp.where(lane_lo, slab, zero) if side == 0 else jnp.where(lane_lo, zero, slab)
                s2 = lax.dot_general(qm, k2, nt, preferred_element_type=F32)
                s = jnp.where(cur_mask, s2[:, WINDOW:], jnp.minimum(s2[:, :WINDOW], prev_cap))
                sink = sinks_ref[j + side * SWA_GROUP]
                m = jnp.maximum(jnp.max(s, axis=-1, keepdims=True), sink)
                p = jnp.exp(s - m)
                p2 = jnp.concatenate([jnp.where(cur_mask, 0.0, p), jnp.where(cur_mask, p, 0.0)],
                                     axis=1).astype(BF16)
                o2 = jnp.dot(p2, v2_ones, preferred_element_type=F32)
                num.append(o2[:, :LANES])
                den.append(o2[:, LANES:] + jnp.exp(sink - m))
            o_ref[rows, j * LANES:(j + 1) * LANES] = (
                jnp.where(lane_lo, num[0], num[1]) / jnp.where(lane_lo, den[0], den[1])).astype(BF16)
    y = jnp.dot(o_ref[...], wo_ref[...], preferred_element_type=F32)
    y_ref[...] = x_ref[...] + y + bo_ref[...]


def _swa_prompt(q, kv, x, wo, bo, sinks, batch, n_sub):
    n = x.shape[0]
    tm = n_sub * WINDOW
    steps = n // batch // tm
    nq = SWA_HEADS * SWA_HEAD_DIM
    nkv = kv.shape[1]
    row = lambda b, i: (b * steps + i, 0)
    prev = lambda b, i: ((b * steps + i) * n_sub - jnp.minimum(i, 1), 0)
    return pl.pallas_call(
        functools.partial(_swa_prompt_kernel, n_sub=n_sub),
        grid=(batch, steps),
        in_specs=[
            pl.BlockSpec(memory_space=pltpu.SMEM),
            pl.BlockSpec((tm, nq), row),
            pl.BlockSpec((WINDOW, nkv), prev),
            pl.BlockSpec((tm, nkv), row),
            pl.BlockSpec((tm, D_MODEL), row),
            _resident(wo.shape),
            _resident(bo.shape),
        ],
        out_specs=pl.BlockSpec((tm, D_MODEL), row),
        out_shape=jax.ShapeDtypeStruct((n, D_MODEL), F32),
        scratch_shapes=[pltpu.VMEM((tm, nq), BF16)],
        compiler_params=_params(2),
        name="swa_prompt",
    )(sinks, q, kv, kv, x, wo, bo)


def _swa_sample_kernel(sinks_ref, q_ref, kvn_ref, ck_ref, cv_ref, o_ref, nk_ref, nv_ref, *, nb):
    nk = SWA_KV_HEADS * SWA_HEAD_DIM
    t = DEC_SEQ
    zpad = jnp.zeros((WINDOW - t, nk), F32)
    for b in range(nb):
        rows = slice(b * t, (b + 1) * t)
        k_new = kvn_ref[rows, :nk]
        v_new = kvn_ref[rows, nk:]
        ck = ck_ref[b]
        cv = cv_ref[b]
        k2 = jnp.concatenate([ck, k_new, zpad], axis=0).astype(BF16)
        v2 = jnp.concatenate([cv, v_new, zpad], axis=0).astype(BF16)
        o_ref[rows, :] = _swa_attend(q_ref[rows, :], k2, v2, True, sinks_ref)
        nk_ref[b, :WINDOW - t, :] = ck[t:, :]
        nk_ref[b, WINDOW - t:, :] = k_new
        nv_ref[b, :WINDOW - t, :] = cv[t:, :]
        nv_ref[b, WINDOW - t:, :] = v_new


def _swa_sample(q, kv_new, cache_k, cache_v, sinks, nb):
    batch = cache_k.shape[0]
    nq = SWA_HEADS * SWA_HEAD_DIM
    nk = SWA_KV_HEADS * SWA_HEAD_DIM
    row = lambda b: (b, 0)
    cache = pl.BlockSpec((nb, WINDOW, nk), lambda b: (b, 0, 0))
    return pl.pallas_call(
        functools.partial(_swa_sample_kernel, nb=nb),
        grid=(batch // nb,),
        in_specs=[
            pl.BlockSpec(memory_space=pltpu.SMEM),
            pl.BlockSpec((nb * DEC_SEQ, nq), row),
            pl.BlockSpec((nb * DEC_SEQ, 2 * nk), row),
            cache,
            cache,
        ],
        out_specs=[pl.BlockSpec((nb * DEC_SEQ, nq), row), cache, cache],
        out_shape=[
            jax.ShapeDtypeStruct((batch * DEC_SEQ, nq), F32),
            jax.ShapeDtypeStruct(cache_k.shape, F32),
            jax.ShapeDtypeStruct(cache_v.shape, F32),
        ],
        compiler_params=_params(1),
        name="swa_sample",
    )(sinks, q, kv_new, cache_k, cache_v)


def _rotation_tables(positions, repeat):
    half = RET_DK // 2
    inv = 1.0 / (ROPE_BASE ** (np.arange(half, dtype=np.float64) / half))
    ang = np.asarray(positions, np.float64)[:, None] * inv[None, :]
    cos = np.tile(np.cos(ang), (repeat, 1)).astype(np.float32)
    sin = np.tile(np.sin(ang), (repeat, 1)).astype(np.float32)
    return jnp.asarray(cos), jnp.asarray(sin)


TM_PROJ = 512
TM_SAMPLE = 256


def kernel(x_prompt, x_sample, state_ret, cache_swa_k, cache_swa_v, ret_w_q, ret_w_k, ret_w_v, ret_w_g, ret_w_o, swa_w_qkv, swa_b_qkv, swa_w_o, swa_b_o, swa_sinks, norm_mix, norm_ffn, ffn_w1, ffn_w3, ffn_w2, norm_final):
    bp, tp, d = x_prompt.shape
    bs, ts, _ = x_sample.shape
    assert (tp, d, ts) == (SEQ, D_MODEL, DEC_SEQ)
    xp = x_prompt.reshape(bp * tp, d)
    xs = x_sample.reshape(bs * ts, d)
    nk = SWA_KV_HEADS * SWA_HEAD_DIM

    cos_p, sin_p = _rotation_tables(np.arange(tp), 1)
    cos_s, sin_s = _rotation_tables(PAST_LEN + np.arange(ts), TM_SAMPLE // ts)
    gamma_final = norm_final.reshape(1, d)
    zero_bias = jnp.zeros((1, d), F32)

    ret_w = [w.astype(BF16) for w in (ret_w_q, ret_w_k, ret_w_v, ret_w_g)]
    w_ro = ret_w_o.astype(BF16)
    g_mix = norm_mix[0].reshape(1, d)
    qp, kp, vp, gp = _ret_proj(xp, g_mix, *ret_w, 0, cos_p, sin_p, TM_PROJ, BF16)
    qs, ks, vs, gs = _ret_proj(xs, g_mix, *ret_w, 0, cos_s, sin_s, TM_SAMPLE, F32)
    xp, state_p, gated_s, state_s = _ret_core(
        qp, kp, vp, gp, xp, w_ro, 0, qs, ks, vs, gs, state_ret[0], bp, RET_PROMPT_CHUNK)
    xs = _linear(gated_s, w_ro[0], zero_bias, xs, TM_SAMPLE)

    w1 = ffn_w1.astype(BF16)
    w3 = ffn_w3.astype(BF16)
    w2 = ffn_w2.astype(BF16)
    g_ffn = norm_ffn[0].reshape(1, d)
    xp = _ffn(xp, g_ffn, w1, w3, w2, 0, gamma_final, TM_PROJ, False)
    xs = _ffn(xs, g_ffn, w1, w3, w2, 0, gamma_final, TM_SAMPLE, False)

    nq = SWA_HEADS * SWA_HEAD_DIM
    heads = (SWA_KV_HEADS, SWA_GROUP, SWA_HEAD_DIM)
    w_q = swa_w_qkv[0][:, :nq].reshape(d, *heads).transpose(0, 2, 1, 3).reshape(d, nq).astype(BF16)
    b_q = swa_b_qkv[0][:nq].reshape(heads).transpose(1, 0, 2).reshape(1, nq)
    w_kv = swa_w_qkv[0][:, nq:].astype(BF16)
    b_kv = swa_b_qkv[0][nq:].reshape(1, 2 * nk)
    w_so = swa_w_o[0].reshape(*heads, d).transpose(1, 0, 2, 3).reshape(nq, d).astype(BF16)
    b_so = swa_b_o[0].reshape(1, d)
    sinks = swa_sinks[0]
    g_mix = norm_mix[1].reshape(1, d)
    qp, kvp = _swa_proj(xp, g_mix, w_q, b_q, w_kv, b_kv, TM_PROJ, BF16)
    xp = _swa_prompt(qp, kvp, xp, w_so, b_so, sinks, bp, 2)
    qs, kvs = _swa_proj(xs, g_mix, w_q, b_q, w_kv, b_kv, TM_SAMPLE, F32)
    att_s, new_k, new_v = _swa_sample(
        qs, kvs, cache_swa_k[0].reshape(bs, WINDOW, nk), cache_swa_v[0].reshape(bs, WINDOW, nk), sinks, 8)
    xs = _linear(att_s, w_so, b_so, xs, TM_SAMPLE)

    g_ffn = norm_ffn[1].reshape(1, d)
    yp = _ffn(xp, g_ffn, w1, w3, w2, 1, gamma_final, TM_PROJ, True)
    ys = _ffn(xs, g_ffn, w1, w3, w2, 1, gamma_final, TM_SAMPLE, True)

    kv_tail = kvp.reshape(bp, tp, 2 * nk)[:, tp - WINDOW:, :]
    kv_shape = (1, bp, WINDOW, SWA_KV_HEADS, SWA_HEAD_DIM)
    cache_shape = (1, bs, WINDOW, SWA_KV_HEADS, SWA_HEAD_DIM)
    return (
        yp.reshape(bp, tp, d),
        ys.reshape(bs, ts, d),
        state_p[None],
        state_s[None],
        kv_tail[..., :nk].reshape(kv_shape),
        kv_tail[..., nk:].reshape(kv_shape),
        new_k.reshape(cache_shape),
        new_v.reshape(cache_shape),
    )
```

```python
import functools
import math
from typing import NamedTuple

import numpy as np
import jax
import jax.numpy as jnp
from jax import lax
from jax.experimental import pallas as pl
from jax.experimental.pallas import tpu as pltpu

F32 = jnp.float32
BF16 = jnp.bfloat16

D_MODEL = 1024
SEQ = 2048
DEC_SEQ = 8
PAST_LEN = 16384

RET_HEADS = 4
RET_DK = 256
RET_DV = 512
RET_CHUNK = 128
RET_PROMPT_CHUNK = 256
NORM_ROWS = 64
ROPE_BASE = 10000.0

SWA_HEADS = 16
SWA_KV_HEADS = 2
SWA_HEAD_DIM = 64
SWA_GROUP = SWA_HEADS // SWA_KV_HEADS
WINDOW = 128

D_FF = 2816
RMS_EPS = 1e-6
GN_EPS = 1e-6
NEG_INF = -1e30

LANES = 128
VMEM_LIMIT = 56 * 1024 * 1024


def _params(n_axes):
    return pltpu.CompilerParams(
        dimension_semantics=("arbitrary",) * n_axes, vmem_limit_bytes=VMEM_LIMIT)


def _rms(x, g):
    ms = jnp.mean(x * x, axis=-1, keepdims=True)
    return x * lax.rsqrt(ms + RMS_EPS) * g


def _silu(x):
    return x * jax.nn.sigmoid(x)


def _resident(shape):
    zeros = (0,) * len(shape)
    return pl.BlockSpec(shape, lambda *_: zeros, pipeline_mode=pl.Buffered(1))


def _layer_resident(stacked_shape, layer):
    return pl.BlockSpec((None,) + tuple(stacked_shape[1:]), lambda *_: (layer, 0, 0),
                        pipeline_mode=pl.Buffered(1))


def _ret_proj_kernel(x_ref, g_ref, wq_ref, wk_ref, wv_ref, wg_ref, cos_ref, sin_ref,
                     q_ref, k_ref, v_ref, gs_ref):
    h = _rms(x_ref[...], g_ref[...]).astype(BF16)
    cos = cos_ref[...]
    sin = sin_ref[...]
    half = RET_DK // 2

    def rotated(w_ref, out_ref, scale):
        y = jnp.dot(h, w_ref[...], preferred_element_type=F32)
        for hd in range(RET_HEADS):
            lo = hd * RET_DK
            a = y[:, lo:lo + half]
            b = y[:, lo + half:lo + RET_DK]
            out_ref[:, lo:lo + half] = ((a * cos - b * sin) * scale).astype(out_ref.dtype)
            out_ref[:, lo + half:lo + RET_DK] = ((a * sin + b * cos) * scale).astype(out_ref.dtype)

    rotated(wq_ref, q_ref, 1.0)
    rotated(wk_ref, k_ref, RET_DK ** -0.5)
    v_ref[...] = jnp.dot(h, wv_ref[...], preferred_element_type=F32).astype(v_ref.dtype)
    g = jnp.dot(h, wg_ref[...], preferred_element_type=F32)
    gs_ref[...] = _silu(g).astype(gs_ref.dtype)


def _ret_proj(x, gamma, wq, wk, wv, wg, layer, cos, sin, tm, out_dtype):
    n = x.shape[0]
    nqk = RET_HEADS * RET_DK
    nv = RET_HEADS * RET_DV
    n_pos_blocks = cos.shape[0] // tm
    row = lambda i: (i, 0)
    return pl.pallas_call(
        _ret_proj_kernel,
        grid=(n // tm,),
        in_specs=[
            pl.BlockSpec((tm, D_MODEL), row),
            _resident((1, D_MODEL)),
            _layer_resident(wq.shape, layer),
            _layer_resident(wk.shape, layer),
            _layer_resident(wv.shape, layer),
            _layer_resident(wg.shape, layer),
            pl.BlockSpec((tm, LANES), lambda i: (i % n_pos_blocks, 0)),
            pl.BlockSpec((tm, LANES), lambda i: (i % n_pos_blocks, 0)),
        ],
        out_specs=[
            pl.BlockSpec((tm, nqk), row),
            pl.BlockSpec((tm, nqk), row),
            pl.BlockSpec((tm, nv), row),
            pl.BlockSpec((tm, nv), row),
        ],
        out_shape=[
            jax.ShapeDtypeStruct((n, nqk), out_dtype),
            jax.ShapeDtypeStruct((n, nqk), out_dtype),
            jax.ShapeDtypeStruct((n, nv), out_dtype),
            jax.ShapeDtypeStruct((n, nv), out_dtype),
        ],
        compiler_params=_params(1),
        name="ret_proj",
    )(x, gamma, wq, wk, wv, wg, cos, sin)


class _RetUnit(NamedTuple):
    qc: jax.Array
    kc: jax.Array
    vc: jax.Array
    gsc: jax.Array
    s_in: object
    s_out: object
    s_bf_in: object
    s_bf_out: object
    chunk: int


def _ret_head(u, hd):
    return (u.qc[:, hd * RET_DK:(hd + 1) * RET_DK], u.kc[:, hd * RET_DK:(hd + 1) * RET_DK],
            u.vc[:, hd * RET_DV:(hd + 1) * RET_DV])


def _ret_scores(u, hd):
    qh, kh, _ = _ret_head(u, hd)
    return lax.dot_general(qh, kh, (((1,), (1,)), ((), ())), preferred_element_type=F32)


def _ret_output(u, hd, scores):
    qh, _, vh = _ret_head(u, hd)
    cq, ck = scores.shape
    log_gamma = math.log1p(-(2.0 ** -(5.0 + hd)))
    ii = lax.broadcasted_iota(jnp.int32, (cq, ck), 0)
    jj = lax.broadcasted_iota(jnp.int32, (cq, ck), 1)
    diff = (ii - jj).astype(F32)
    intra = jnp.where(diff >= 0, jnp.exp(log_gamma * jnp.maximum(diff, 0.0)), 0.0)
    irow = lax.broadcasted_iota(jnp.int32, (cq, 1), 0).astype(F32)
    q_dec = (qh.astype(F32) * jnp.exp(log_gamma * (irow + 1.0))).astype(BF16)
    lhs = jnp.concatenate([(scores * intra).astype(BF16), q_dec], axis=1)
    rhs = jnp.concatenate([vh, u.s_bf_in[hd]], axis=0)
    return jnp.dot(lhs, rhs, preferred_element_type=F32)


def _ret_update_state(u, hd):
    _, kh, vh = _ret_head(u, hd)
    log_gamma = math.log1p(-(2.0 ** -(5.0 + hd)))
    jrow = lax.broadcasted_iota(jnp.int32, (kh.shape[0], 1), 0).astype(F32)
    kd = (kh.astype(F32) * jnp.exp(log_gamma * (u.chunk - 1.0 - jrow))).astype(BF16)
    s_new = u.s_in[hd] * math.exp(log_gamma * u.chunk) + lax.dot_general(
        kd, vh, (((0,), (0,)), ((), ())), preferred_element_type=F32)
    u.s_out[hd] = s_new
    if u.s_bf_out is not None:
        u.s_bf_out[hd] = s_new.astype(BF16)


def _ret_gate(o, gs):
    mu = jnp.mean(o, axis=-1, keepdims=True)
    oc = o - mu
    var = jnp.mean(oc * oc, axis=-1, keepdims=True)
    return oc * lax.rsqrt(var + GN_EPS) * gs.astype(F32)


def _ret_core_kernel(q_ref, k_ref, v_ref, gs_ref, x_ref, wo_ref,
                     qs_ref, ks_ref, vs_ref, gss_ref, s0_ref,
                     y_ref, sp_ref, gated_s_ref, ss_ref, gated_ref, o_scr, sp_bf, ss_bf,
                     *, chunk, nb):
    @pl.when(pl.program_id(1) == 0)
    def _():
        sp_ref[...] = jnp.zeros_like(sp_ref)
        sp_bf[...] = jnp.zeros_like(sp_bf)

    heads = range(RET_HEADS)
    for b in range(nb):
        for hd in heads:
            ss_bf[b, hd] = s0_ref[b, hd].astype(BF16)

    prompt = _RetUnit(q_ref[...], k_ref[...], v_ref[...], gs_ref[...],
                      sp_ref.at[0], sp_ref.at[0], sp_bf, sp_bf, chunk)
    pad = RET_CHUNK - DEC_SEQ
    kpad = jnp.zeros((pad, ks_ref.shape[1]), F32)
    vpad = jnp.zeros((pad, vs_ref.shape[1]), F32)
    samples = []
    for b in range(nb):
        rows = slice(b * DEC_SEQ, (b + 1) * DEC_SEQ)
        samples.append(_RetUnit(
            qs_ref[rows, :].astype(BF16),
            jnp.concatenate([ks_ref[rows, :], kpad], axis=0).astype(BF16),
            jnp.concatenate([vs_ref[rows, :], vpad], axis=0).astype(BF16),
            gss_ref[rows, :], s0_ref.at[b], ss_ref.at[b], ss_bf.at[b], None, DEC_SEQ))

    scores_p = [_ret_scores(prompt, hd) for hd in heads]
    scores_s = [[_ret_scores(u, hd) for hd in heads] for u in samples]
    for u in samples:
        for hd in heads:
            _ret_update_state(u, hd)
    for hd in heads:
        o_scr[hd] = _ret_output(prompt, hd, scores_p[hd])
    o_s = [[_ret_output(u, hd, sc[hd]) for hd in heads] for u, sc in zip(samples, scores_s)]
    for hd in heads:
        _ret_update_state(prompt, hd)

    for hd in heads:
        cols = slice(hd * RET_DV, (hd + 1) * RET_DV)
        for r0 in range(0, chunk, NORM_ROWS):
            rows = slice(r0, r0 + NORM_ROWS)
            gated_ref[rows, cols] = _ret_gate(o_scr[hd, rows, :], gs_ref[rows, cols]).astype(BF16)
        for b, u in enumerate(samples):
            rows = slice(b * DEC_SEQ, (b + 1) * DEC_SEQ)
            gated_s_ref[rows, cols] = _ret_gate(o_s[b][hd], u.gsc[:, cols])
    y_ref[...] = x_ref[...] + jnp.dot(gated_ref[...], wo_ref[...], preferred_element_type=F32)


def _ret_core(q, k, v, gs, x, wo, layer, qs, ks, vs, gss, s0, batch, chunk):
    n = x.shape[0]
    steps = n // batch // chunk
    nb = s0.shape[0] // (batch * steps)
    assert nb * batch * steps == s0.shape[0]
    nqk = RET_HEADS * RET_DK
    nv = RET_HEADS * RET_DV
    ts = nb * DEC_SEQ
    row = lambda b, c: (b * steps + c, 0)
    state_s = pl.BlockSpec((nb, RET_HEADS, RET_DK, RET_DV), lambda b, c: (b * steps + c, 0, 0, 0))
    return pl.pallas_call(
        functools.partial(_ret_core_kernel, chunk=chunk, nb=nb),
        grid=(batch, steps),
        in_specs=[
            pl.BlockSpec((chunk, nqk), row),
            pl.BlockSpec((chunk, nqk), row),
            pl.BlockSpec((chunk, nv), row),
            pl.BlockSpec((chunk, nv), row),
            pl.BlockSpec((chunk, D_MODEL), row),
            _layer_resident(wo.shape, layer),
            pl.BlockSpec((ts, nqk), row),
            pl.BlockSpec((ts, nqk), row),
            pl.BlockSpec((ts, nv), row),
            pl.BlockSpec((ts, nv), row),
            state_s,
        ],
        out_specs=[
            pl.BlockSpec((chunk, D_MODEL), row),
            pl.BlockSpec((1, RET_HEADS, RET_DK, RET_DV), lambda b, c: (b, 0, 0, 0)),
            pl.BlockSpec((ts, nv), row),
            state_s,
        ],
        out_shape=[
            jax.ShapeDtypeStruct((n, D_MODEL), F32),
            jax.ShapeDtypeStruct((batch, RET_HEADS, RET_DK, RET_DV), F32),
            jax.ShapeDtypeStruct((s0.shape[0] * DEC_SEQ, nv), F32),
            jax.ShapeDtypeStruct(s0.shape, F32),
        ],
        scratch_shapes=[
            pltpu.VMEM((chunk, nv), BF16),
            pltpu.VMEM((RET_HEADS, chunk, RET_DV), F32),
            pltpu.VMEM((RET_HEADS, RET_DK, RET_DV), BF16),
            pltpu.VMEM((nb, RET_HEADS, RET_DK, RET_DV), BF16),
        ],
        compiler_params=_params(2),
        name="ret_core",
    )(q, k, v, gs, x, wo, qs, ks, vs, gss, s0)


def _linear_kernel(a_ref, w_ref, b_ref, res_ref, y_ref):
    y = jnp.dot(a_ref[...].astype(BF16), w_ref[...], preferred_element_type=F32)
    y_ref[...] = res_ref[...] + y + b_ref[...]


def _linear(a, w, bias, res, tm):
    n, kdim = a.shape
    row = lambda i: (i, 0)
    return pl.pallas_call(
        _linear_kernel,
        grid=(n // tm,),
        in_specs=[
            pl.BlockSpec((tm, kdim), row),
            _resident(w.shape),
            _resident(bias.shape),
            pl.BlockSpec((tm, w.shape[1]), row),
        ],
        out_specs=pl.BlockSpec((tm, w.shape[1]), row),
        out_shape=jax.ShapeDtypeStruct((n, w.shape[1]), F32),
        compiler_params=_params(1),
        name="linear_residual",
    )(a, w, bias, res)


def _ffn_kernel(x_ref, g_ref, w1_ref, w3_ref, w2_ref, gf_ref, y_ref, *, final_norm):
    x = x_ref[...]
    h = _rms(x, g_ref[...]).astype(BF16)
    a = jnp.dot(h, w1_ref[...], preferred_element_type=F32)
    b = jnp.dot(h, w3_ref[...], preferred_element_type=F32)
    act = (_silu(a) * b).astype(BF16)
    y = x + jnp.dot(act, w2_ref[...], preferred_element_type=F32)
    if final_norm:
        y = _rms(y, gf_ref[...])
    y_ref[...] = y


def _ffn(x, gamma, w1, w3, w2, layer, gamma_final, tm, final_norm):
    n = x.shape[0]
    row = lambda i: (i, 0)
    return pl.pallas_call(
        functools.partial(_ffn_kernel, final_norm=final_norm),
        grid=(n // tm,),
        in_specs=[
            pl.BlockSpec((tm, D_MODEL), row),
            _resident((1, D_MODEL)),
            _layer_resident(w1.shape, layer),
            _layer_resident(w3.shape, layer),
            _layer_resident(w2.shape, layer),
            _resident((1, D_MODEL)),
        ],
        out_specs=pl.BlockSpec((tm, D_MODEL), row),
        out_shape=jax.ShapeDtypeStruct((n, D_MODEL), F32),
        compiler_params=_params(1),
        name="ffn",
    )(x, gamma, w1, w3, w2, gamma_final)


def _swa_proj_kernel(x_ref, g_ref, wq_ref, bq_ref, wkv_ref, bkv_ref, q_ref, kv_ref):
    h = _rms(x_ref[...], g_ref[...]).astype(BF16)
    q = jnp.dot(h, wq_ref[...], preferred_element_type=F32) + bq_ref[...]
    q_ref[...] = (q * (SWA_HEAD_DIM ** -0.5)).astype(q_ref.dtype)
    kv_ref[...] = jnp.dot(h, wkv_ref[...], preferred_element_type=F32) + bkv_ref[...]


def _swa_proj(x, gamma, wq, bq, wkv, bkv, tm, q_dtype):
    n = x.shape[0]
    nq = SWA_HEADS * SWA_HEAD_DIM
    nkv = 2 * SWA_KV_HEADS * SWA_HEAD_DIM
    row = lambda i: (i, 0)
    return pl.pallas_call(
        _swa_proj_kernel,
        grid=(n // tm,),
        in_specs=[
            pl.BlockSpec((tm, D_MODEL), row),
            _resident((1, D_MODEL)),
            _resident(wq.shape),
            _resident(bq.shape),
            _resident(wkv.shape),
            _resident(bkv.shape),
        ],
        out_specs=[pl.BlockSpec((tm, nq), row), pl.BlockSpec((tm, nkv), row)],
        out_shape=[
            jax.ShapeDtypeStruct((n, nq), q_dtype),
            jax.ShapeDtypeStruct((n, nkv), F32),
        ],
        compiler_params=_params(1),
        name="swa_proj",
    )(x, gamma, wq, bq, wkv, bkv)


def _swa_attend(q, k2, v2, prev_valid, sinks_ref):
    r = q.shape[0]
    hd = SWA_HEAD_DIM
    lane_lo = lax.broadcasted_iota(jnp.int32, (r, LANES), 1) < hd
    zero = jnp.zeros((r, LANES), q.dtype)
    pieces = []
    sink_pieces = []
    for j in range(SWA_GROUP):
        slab = q[:, j * LANES:(j + 1) * LANES]
        pieces.append(jnp.where(lane_lo, slab, zero))
        pieces.append(jnp.where(lane_lo, zero, slab))
        sink_pieces.append(jnp.full((r, 1), sinks_ref[j], F32))
        sink_pieces.append(jnp.full((r, 1), sinks_ref[SWA_GROUP + j], F32))
    qs = jnp.concatenate(pieces, axis=0).astype(BF16)
    sink = jnp.concatenate(sink_pieces, axis=0)
    rows = SWA_HEADS * r

    s2 = lax.dot_general(qs, k2, (((1,), (1,)), ((), ())), preferred_element_type=F32)
    qi = lax.broadcasted_iota(jnp.int32, (rows, WINDOW), 0) & (r - 1)
    kj = lax.broadcasted_iota(jnp.int32, (rows, WINDOW), 1)
    cur = kj <= qi
    prev_cap = jnp.where(prev_valid, jnp.float32(3.0e38), jnp.float32(NEG_INF))
    s = jnp.where(cur, s2[:, WINDOW:], jnp.minimum(s2[:, :WINDOW], prev_cap))
    m = jnp.maximum(jnp.max(s, axis=-1, keepdims=True), sink)
    p = jnp.exp(s - m)
    den = jnp.sum(p, axis=-1, keepdims=True) + jnp.exp(sink - m)
    p2 = jnp.concatenate([jnp.where(cur, 0.0, p), jnp.where(cur, p, 0.0)], axis=1).astype(BF16)
    o = jnp.dot(p2, v2, preferred_element_type=F32) / den
    outs = [jnp.where(lane_lo, o[(2 * j) * r:(2 * j + 1) * r], o[(2 * j + 1) * r:(2 * j + 2) * r])
            for j in range(SWA_GROUP)]
    return jnp.concatenate(outs, axis=1)


def _swa_prompt_kernel(sinks_ref, q_ref, kvp_ref, kvc_ref, x_ref, wo_ref, bo_ref, y_ref, o_ref,
                       *, n_sub):
    nk = SWA_KV_HEADS * SWA_HEAD_DIM
    r = WINDOW
    lane_lo = lax.broadcasted_iota(jnp.int32, (r, LANES), 1) < SWA_HEAD_DIM
    qi = lax.broadcasted_iota(jnp.int32, (r, WINDOW), 0)
    kj = lax.broadcasted_iota(jnp.int32, (r, WINDOW), 1)
    cur_mask = kj <= qi
    ones = jnp.ones((2 * WINDOW, LANES), BF16)
    nt = (((1,), (1,)), ((), ()))
    for c in range(n_sub):
        rows = slice(c * WINDOW, (c + 1) * WINDOW)
        prev = kvp_ref[...] if c == 0 else kvc_ref[(c - 1) * WINDOW:c * WINDOW, :]
        cur = kvc_ref[rows, :]
        k2 = jnp.concatenate([prev[:, :nk], cur[:, :nk]], axis=0).astype(BF16)
        v2 = jnp.concatenate([prev[:, nk:], cur[:, nk:]], axis=0).astype(BF16)
        v2_ones = jnp.concatenate([v2, ones], axis=1)
        prev_valid = (pl.program_id(1) > 0) if c == 0 else True
        prev_cap = jnp.where(prev_valid, jnp.float32(3.0e38), jnp.float32(NEG_INF))
        for j in range(SWA_GROUP):
            slab = q_ref[rows, j * LANES:(j + 1) * LANES]
            zero = jnp.zeros_like(slab)
            num = []
            den = []
            for side in range(2):
                qm = jnp.where(lane_lo, slab, zero) if side == 0 else jnp.where(lane_lo, zero, slab)
                s2 = lax.dot_general(qm, k2, nt, preferred_element_type=F32)
                s = jnp.where(cur_mask, s2[:, WINDOW:], jnp.minimum(s2[:, :WINDOW], prev_cap))
                sink = sinks_ref[j + side * SWA_GROUP]
                m = jnp.maximum(jnp.max(s, axis=-1, keepdims=True), sink)
                p = jnp.exp(s - m)
                p2 = jnp.concatenate([jnp.where(cur_mask, 0.0, p), jnp.where(cur_mask, p, 0.0)],
                                     axis=1).astype(BF16)
                o2 = jnp.dot(p2, v2_ones, preferred_element_type=F32)
                num.append(o2[:, :LANES])
                den.append(o2[:, LANES:] + jnp.exp(sink - m))
            o_ref[rows, j * LANES:(j + 1) * LANES] = (
                jnp.where(lane_lo, num[0], num[1]) / jnp.where(lane_lo, den[0], den[1])).astype(BF16)
    y = jnp.dot(o_ref[...], wo_ref[...], preferred_element_type=F32)
    y_ref[...] = x_ref[...] + y + bo_ref[...]


def _swa_prompt(q, kv, x, wo, bo, sinks, batch, n_sub):
    n = x.shape[0]
    tm = n_sub * WINDOW
    steps = n // batch // tm
    nq = SWA_HEADS * SWA_HEAD_DIM
    nkv = kv.shape[1]
    row = lambda b, i: (b * steps + i, 0)
    prev = lambda b, i: ((b * steps + i) * n_sub - jnp.minimum(i, 1), 0)
    return pl.pallas_call(
        functools.partial(_swa_prompt_kernel, n_sub=n_sub),
        grid=(batch, steps),
        in_specs=[
            pl.BlockSpec(memory_space=pltpu.SMEM),
            pl.BlockSpec((tm, nq), row),
            pl.BlockSpec((WINDOW, nkv), prev),
            pl.BlockSpec((tm, nkv), row),
            pl.BlockSpec((tm, D_MODEL), row),
            _resident(wo.shape),
            _resident(bo.shape),
        ],
        out_specs=pl.BlockSpec((tm, D_MODEL), row),
        out_shape=jax.ShapeDtypeStruct((n, D_MODEL), F32),
        scratch_shapes=[pltpu.VMEM((tm, nq), BF16)],
        compiler_params=_params(2),
        name="swa_prompt",
    )(sinks, q, kv, kv, x, wo, bo)


def _swa_sample_kernel(sinks_ref, q_ref, kvn_ref, ck_ref, cv_ref, o_ref, nk_ref, nv_ref, *, nb):
    nk = SWA_KV_HEADS * SWA_HEAD_DIM
    t = DEC_SEQ
    zpad = jnp.zeros((WINDOW - t, nk), F32)
    for b in range(nb):
        rows = slice(b * t, (b + 1) * t)
        k_new = kvn_ref[rows, :nk]
        v_new = kvn_ref[rows, nk:]
        ck = ck_ref[b]
        cv = cv_ref[b]
        k2 = jnp.concatenate([ck, k_new, zpad], axis=0).astype(BF16)
        v2 = jnp.concatenate([cv, v_new, zpad], axis=0).astype(BF16)
        o_ref[rows, :] = _swa_attend(q_ref[rows, :], k2, v2, True, sinks_ref)
        nk_ref[b, :WINDOW - t, :] = ck[t:, :]
        nk_ref[b, WINDOW - t:, :] = k_new
        nv_ref[b, :WINDOW - t, :] = cv[t:, :]
        nv_ref[b, WINDOW - t:, :] = v_new


def _swa_sample(q, kv_new, cache_k, cache_v, sinks, nb):
    batch = cache_k.shape[0]
    nq = SWA_HEADS * SWA_HEAD_DIM
    nk = SWA_KV_HEADS * SWA_HEAD_DIM
    row = lambda b: (b, 0)
    cache = pl.BlockSpec((nb, WINDOW, nk), lambda b: (b, 0, 0))
    return pl.pallas_call(
        functools.partial(_swa_sample_kernel, nb=nb),
        grid=(batch // nb,),
        in_specs=[
            pl.BlockSpec(memory_space=pltpu.SMEM),
            pl.BlockSpec((nb * DEC_SEQ, nq), row),
            pl.BlockSpec((nb * DEC_SEQ, 2 * nk), row),
            cache,
            cache,
        ],
        out_specs=[pl.BlockSpec((nb * DEC_SEQ, nq), row), cache, cache],
        out_shape=[
            jax.ShapeDtypeStruct((batch * DEC_SEQ, nq), F32),
            jax.ShapeDtypeStruct(cache_k.shape, F32),
            jax.ShapeDtypeStruct(cache_v.shape, F32),
        ],
        compiler_params=_params(1),
        name="swa_sample",
    )(sinks, q, kv_new, cache_k, cache_v)


def _rotation_tables(positions, repeat):
    half = RET_DK // 2
    inv = 1.0 / (ROPE_BASE ** (np.arange(half, dtype=np.float64) / half))
    ang = np.asarray(positions, np.float64)[:, None] * inv[None, :]
    cos = np.tile(np.cos(ang), (repeat, 1)).astype(np.float32)
    sin = np.tile(np.sin(ang), (repeat, 1)).astype(np.float32)
    return jnp.asarray(cos), jnp.asarray(sin)


TM_PROJ = 512
TM_SAMPLE = 256


def kernel(x_prompt, x_sample, state_ret, cache_swa_k, cache_swa_v, ret_w_q, ret_w_k, ret_w_v, ret_w_g, ret_w_o, swa_w_qkv, swa_b_qkv, swa_w_o, swa_b_o, swa_sinks, norm_mix, norm_ffn, ffn_w1, ffn_w3, ffn_w2, norm_final):
    bp, tp, d = x_prompt.shape
    bs, ts, _ = x_sample.shape
    assert (tp, d, ts) == (SEQ, D_MODEL, DEC_SEQ)
    xp = x_prompt.reshape(bp * tp, d)
    xs = x_sample.reshape(bs * ts, d)
    nk = SWA_KV_HEADS * SWA_HEAD_DIM

    cos_p, sin_p = _rotation_tables(np.arange(tp), 1)
    cos_s, sin_s = _rotation_tables(PAST_LEN + np.arange(ts), TM_SAMPLE // ts)
    gamma_final = norm_final.reshape(1, d)
    zero_bias = jnp.zeros((1, d), F32)

    ret_w = [w.astype(BF16) for w in (ret_w_q, ret_w_k, ret_w_v, ret_w_g)]
    w_ro = ret_w_o.astype(BF16)
    g_mix = norm_mix[0].reshape(1, d)
    qp, kp, vp, gp = _ret_proj(xp, g_mix, *ret_w, 0, cos_p, sin_p, TM_PROJ, BF16)
    qs, ks, vs, gs = _ret_proj(xs, g_mix, *ret_w, 0, cos_s, sin_s, TM_SAMPLE, F32)
    xp, state_p, gated_s, state_s = _ret_core(
        qp, kp, vp, gp, xp, w_ro, 0, qs, ks, vs, gs, state_ret[0], bp, RET_PROMPT_CHUNK)
    xs = _linear(gated_s, w_ro[0], zero_bias, xs, TM_SAMPLE)

    w1 = ffn_w1.astype(BF16)
    w3 = ffn_w3.astype(BF16)
    w2 = ffn_w2.astype(BF16)
    g_ffn = norm_ffn[0].reshape(1, d)
    xp = _ffn(xp, g_ffn, w1, w3, w2, 0, gamma_final, TM_PROJ, False)
    xs = _ffn(xs, g_ffn, w1, w3, w2, 0, gamma_final, TM_SAMPLE, False)

    nq = SWA_HEADS * SWA_HEAD_DIM
    heads = (SWA_KV_HEADS, SWA_GROUP, SWA_HEAD_DIM)
    w_q = swa_w_qkv[0][:, :nq].reshape(d, *heads).transpose(0, 2, 1, 3).reshape(d, nq).astype(BF16)
    b_q = swa_b_qkv[0][:nq].reshape(heads).transpose(1, 0, 2).reshape(1, nq)
    w_kv = swa_w_qkv[0][:, nq:].astype(BF16)
    b_kv = swa_b_qkv[0][nq:].reshape(1, 2 * nk)
    w_so = swa_w_o[0].reshape(*heads, d).transpose(1, 0, 2, 3).reshape(nq, d).astype(BF16)
    b_so = swa_b_o[0].reshape(1, d)
    sinks = swa_sinks[0]
    g_mix = norm_mix[1].reshape(1, d)
    qp, kvp = _swa_proj(xp, g_mix, w_q, b_q, w_kv, b_kv, TM_PROJ, BF16)
    xp = _swa_prompt(qp, kvp, xp, w_so, b_so, sinks, bp, 2)
    qs, kvs = _swa_proj(xs, g_mix, w_q, b_q, w_kv, b_kv, TM_SAMPLE, F32)
    att_s, new_k, new_v = _swa_sample(
        qs, kvs, cache_swa_k[0].reshape(bs, WINDOW, nk), cache_swa_v[0].reshape(bs, WINDOW, nk), sinks, 8)
    xs = _linear(att_s, w_so, b_so, xs, TM_SAMPLE)

    g_ffn = norm_ffn[1].reshape(1, d)
    yp = _ffn(xp, g_ffn, w1, w3, w2, 1, gamma_final, TM_PROJ, True)
    ys = _ffn(xs, g_ffn, w1, w3, w2, 1, gamma_final, TM_SAMPLE, True)

    kv_tail = kvp.reshape(bp, tp, 2 * nk)[:, tp - WINDOW:, :]
    kv_shape = (1, bp, WINDOW, SWA_KV_HEADS, SWA_HEAD_DIM)
    cache_shape = (1, bs, WINDOW, SWA_KV_HEADS, SWA_HEAD_DIM)
    return (
        yp.reshape(bp, tp, d),
        ys.reshape(bs, ts, d),
        state_p[None],
        state_s[None],
        kv_tail[..., :nk].reshape(kv_shape),
        kv_tail[..., nk:].reshape(kv_shape),
        new_k.reshape(cache_shape),
        new_v.reshape(cache_shape),
    )
```

```python
import functools
import math
from typing import NamedTuple

import numpy as np
import jax
import jax.numpy as jnp
from jax import lax
from jax.experimental import pallas as pl
from jax.experimental.pallas import tpu as pltpu

F32 = jnp.float32
BF16 = jnp.bfloat16

D_MODEL = 1024
SEQ = 2048
DEC_SEQ = 8
PAST_LEN = 16384

RET_HEADS = 4
RET_DK = 256
RET_DV = 512
RET_CHUNK = 128
RET_PROMPT_CHUNK = 256
NORM_ROWS = 64
ROPE_BASE = 10000.0

SWA_HEADS = 16
SWA_KV_HEADS = 2
SWA_HEAD_DIM = 64
SWA_GROUP = SWA_HEADS // SWA_KV_HEADS
WINDOW = 128

D_FF = 2816
RMS_EPS = 1e-6
GN_EPS = 1e-6
NEG_INF = -1e30

LANES = 128
VMEM_LIMIT = 56 * 1024 * 1024


def _params(n_axes):
    return pltpu.CompilerParams(
        dimension_semantics=("arbitrary",) * n_axes, vmem_limit_bytes=VMEM_LIMIT)


def _rms(x, g):
    ms = jnp.mean(x * x, axis=-1, keepdims=True)
    return x * lax.rsqrt(ms + RMS_EPS) * g


def _silu(x):
    return x * jax.nn.sigmoid(x)


def _resident(shape):
    zeros = (0,) * len(shape)
    return pl.BlockSpec(shape, lambda *_: zeros, pipeline_mode=pl.Buffered(1))


def _layer_resident(stacked_shape, layer):
    return pl.BlockSpec((None,) + tuple(stacked_shape[1:]), lambda *_: (layer, 0, 0),
                        pipeline_mode=pl.Buffered(1))


def _ret_proj_kernel(x_ref, g_ref, wq_ref, wk_ref, wv_ref, wg_ref, cos_ref, sin_ref,
                     q_ref, k_ref, v_ref, gs_ref):
    h = _rms(x_ref[...], g_ref[...]).astype(BF16)
    cos = cos_ref[...]
    sin = sin_ref[...]
    half = RET_DK // 2

    def rotated(w_ref, out_ref, scale):
        y = jnp.dot(h, w_ref[...], preferred_element_type=F32)
        for hd in range(RET_HEADS):
            lo = hd * RET_DK
            a = y[:, lo:lo + half]
            b = y[:, lo + half:lo + RET_DK]
            out_ref[:, lo:lo + half] = ((a * cos - b * sin) * scale).astype(out_ref.dtype)
            out_ref[:, lo + half:lo + RET_DK] = ((a * sin + b * cos) * scale).astype(out_ref.dtype)

    rotated(wq_ref, q_ref, 1.0)
    rotated(wk_ref, k_ref, RET_DK ** -0.5)
    v_ref[...] = jnp.dot(h, wv_ref[...], preferred_element_type=F32).astype(v_ref.dtype)
    g = jnp.dot(h, wg_ref[...], preferred_element_type=F32)
    gs_ref[...] = _silu(g).astype(gs_ref.dtype)


def _ret_proj(x, gamma, wq, wk, wv, wg, layer, cos, sin, tm, out_dtype):
    n = x.shape[0]
    nqk = RET_HEADS * RET_DK
    nv = RET_HEADS * RET_DV
    n_pos_blocks = cos.shape[0] // tm
    row = lambda i: (i, 0)
    return pl.pallas_call(
        _ret_proj_kernel,
        grid=(n // tm,),
        in_specs=[
            pl.BlockSpec((tm, D_MODEL), row),
            _resident((1, D_MODEL)),
            _layer_resident(wq.shape, layer),
            _layer_resident(wk.shape, layer),
            _layer_resident(wv.shape, layer),
            _layer_resident(wg.shape, layer),
            pl.BlockSpec((tm, LANES), lambda i: (i % n_pos_blocks, 0)),
            pl.BlockSpec((tm, LANES), lambda i: (i % n_pos_blocks, 0)),
        ],
        out_specs=[
            pl.BlockSpec((tm, nqk), row),
            pl.BlockSpec((tm, nqk), row),
            pl.BlockSpec((tm, nv), row),
            pl.BlockSpec((tm, nv), row),
        ],
        out_shape=[
            jax.ShapeDtypeStruct((n, nqk), out_dtype),
            jax.ShapeDtypeStruct((n, nqk), out_dtype),
            jax.ShapeDtypeStruct((n, nv), out_dtype),
            jax.ShapeDtypeStruct((n, nv), out_dtype),
        ],
        compiler_params=_params(1),
        name="ret_proj",
    )(x, gamma, wq, wk, wv, wg, cos, sin)


class _RetUnit(NamedTuple):
    qc: jax.Array
    kc: jax.Array
    vc: jax.Array
    gsc: jax.Array
    s_in: object
    s_out: object
    s_bf_in: object
    s_bf_out: object
    chunk: int


def _ret_head(u, hd):
    return (u.qc[:, hd * RET_DK:(hd + 1) * RET_DK], u.kc[:, hd * RET_DK:(hd + 1) * RET_DK],
            u.vc[:, hd * RET_DV:(hd + 1) * RET_DV])


def _ret_scores(u, hd):
    qh, kh, _ = _ret_head(u, hd)
    return lax.dot_general(qh, kh, (((1,), (1,)), ((), ())), preferred_element_type=F32)


def _ret_output(u, hd, scores):
    qh, _, vh = _ret_head(u, hd)
    cq, ck = scores.shape
    log_gamma = math.log1p(-(2.0 ** -(5.0 + hd)))
    ii = lax.broadcasted_iota(jnp.int32, (cq, ck), 0)
    jj = lax.broadcasted_iota(jnp.int32, (cq, ck), 1)
    diff = (ii - jj).astype(F32)
    intra = jnp.where(diff >= 0, jnp.exp(log_gamma * jnp.maximum(diff, 0.0)), 0.0)
    irow = lax.broadcasted_iota(jnp.int32, (cq, 1), 0).astype(F32)
    q_dec = (qh.astype(F32) * jnp.exp(log_gamma * (irow + 1.0))).astype(BF16)
    lhs = jnp.concatenate([(scores * intra).astype(BF16), q_dec], axis=1)
    rhs = jnp.concatenate([vh, u.s_bf_in[hd]], axis=0)
    return jnp.dot(lhs, rhs, preferred_element_type=F32)


def _ret_update_state(u, hd):
    _, kh, vh = _ret_head(u, hd)
    log_gamma = math.log1p(-(2.0 ** -(5.0 + hd)))
    jrow = lax.broadcasted_iota(jnp.int32, (kh.shape[0], 1), 0).astype(F32)
    kd = (kh.astype(F32) * jnp.exp(log_gamma * (u.chunk - 1.0 - jrow))).astype(BF16)
    s_new = u.s_in[hd] * math.exp(log_gamma * u.chunk) + lax.dot_general(
        kd, vh, (((0,), (0,)), ((), ())), preferred_element_type=F32)
    u.s_out[hd] = s_new
    if u.s_bf_out is not None:
        u.s_bf_out[hd] = s_new.astype(BF16)


def _ret_gate(o, gs):
    mu = jnp.mean(o, axis=-1, keepdims=True)
    oc = o - mu
    var = jnp.mean(oc * oc, axis=-1, keepdims=True)
    return oc * lax.rsqrt(var + GN_EPS) * gs.astype(F32)


def _ret_core_kernel(q_ref, k_ref, v_ref, gs_ref, x_ref, wo_ref,
                     qs_ref, ks_ref, vs_ref, gss_ref, s0_ref,
                     y_ref, sp_ref, gated_s_ref, ss_ref, gated_ref, o_scr, sp_bf, ss_bf,
                     *, chunk, nb):
    @pl.when(pl.program_id(1) == 0)
    def _():
        sp_ref[...] = jnp.zeros_like(sp_ref)
        sp_bf[...] = jnp.zeros_like(sp_bf)

    heads = range(RET_HEADS)
    for b in range(nb):
        for hd in heads:
            ss_bf[b, hd] = s0_ref[b, hd].astype(BF16)

    prompt = _RetUnit(q_ref[...], k_ref[...], v_ref[...], gs_ref[...],
                      sp_ref.at[0], sp_ref.at[0], sp_bf, sp_bf, chunk)
    pad = RET_CHUNK - DEC_SEQ
    kpad = jnp.zeros((pad, ks_ref.shape[1]), F32)
    vpad = jnp.zeros((pad, vs_ref.shape[1]), F32)
    samples = []
    for b in range(nb):
        rows = slice(b * DEC_SEQ, (b + 1) * DEC_SEQ)
        samples.append(_RetUnit(
            qs_ref[rows, :].astype(BF16),
            jnp.concatenate([ks_ref[rows, :], kpad], axis=0).astype(BF16),
            jnp.concatenate([vs_ref[rows, :], vpad], axis=0).astype(BF16),
            gss_ref[rows, :], s0_ref.at[b], ss_ref.at[b], ss_bf.at[b], None, DEC_SEQ))

    scores_p = [_ret_scores(prompt, hd) for hd in heads]
    scores_s = [[_ret_scores(u, hd) for hd in heads] for u in samples]
    for u in samples:
        for hd in heads:
            _ret_update_state(u, hd)
    for hd in heads:
        o_scr[hd] = _ret_output(prompt, hd, scores_p[hd])
    o_s = [[_ret_output(u, hd, sc[hd]) for hd in heads] for u, sc in zip(samples, scores_s)]
    for hd in heads:
        _ret_update_state(prompt, hd)

    for hd in heads:
        cols = slice(hd * RET_DV, (hd + 1) * RET_DV)
        for r0 in range(0, chunk, NORM_ROWS):
            rows = slice(r0, r0 + NORM_ROWS)
            gated_ref[rows, cols] = _ret_gate(o_scr[hd, rows, :], gs_ref[rows, cols]).astype(BF16)
        for b, u in enumerate(samples):
            rows = slice(b * DEC_SEQ, (b + 1) * DEC_SEQ)
            gated_s_ref[rows, cols] = _ret_gate(o_s[b][hd], u.gsc[:, cols])
    y_ref[...] = x_ref[...] + jnp.dot(gated_ref[...], wo_ref[...], preferred_element_type=F32)


def _ret_layer_kernel(x_ref, g_ref, wq_ref, wk_ref, wv_ref, wg_ref, cos_ref, sin_ref, wo_ref,
                      qs_ref, ks_ref, vs_ref, gss_ref, s0_ref,
                      y_ref, sp_ref, gated_s_ref, ss_ref,
                      q_scr, k_scr, v_scr, gs_scr, gated_ref, o_scr, sp_bf, ss_bf, *, chunk, nb):
    _ret_proj_kernel(x_ref, g_ref, wq_ref, wk_ref, wv_ref, wg_ref, cos_ref, sin_ref,
                     q_scr, k_scr, v_scr, gs_scr)
    _ret_core_kernel(q_scr, k_scr, v_scr, gs_scr, x_ref, wo_ref,
                     qs_ref, ks_ref, vs_ref, gss_ref, s0_ref,
                     y_ref, sp_ref, gated_s_ref, ss_ref, gated_ref, o_scr, sp_bf, ss_bf,
                     chunk=chunk, nb=nb)


def _ret_layer(x, gamma, wq, wk, wv, wg, wo, layer, cos, sin, qs, ks, vs, gss, s0, batch, chunk):
    n = x.shape[0]
    steps = n // batch // chunk
    assert cos.shape[0] == steps * chunk
    nb = s0.shape[0] // (batch * steps)
    assert nb * batch * steps == s0.shape[0]
    nqk = RET_HEADS * RET_DK
    nv = RET_HEADS * RET_DV
    ts = nb * DEC_SEQ
    row = lambda b, c: (b * steps + c, 0)
    pos = lambda b, c: (c, 0)
    state_s = pl.BlockSpec((nb, RET_HEADS, RET_DK, RET_DV), lambda b, c: (b * steps + c, 0, 0, 0))
    return pl.pallas_call(
        functools.partial(_ret_layer_kernel, chunk=chunk, nb=nb),
        grid=(batch, steps),
        in_specs=[
            pl.BlockSpec((chunk, D_MODEL), row),
            _resident((1, D_MODEL)),
            _layer_resident(wq.shape, layer),
            _layer_resident(wk.shape, layer),
            _layer_resident(wv.shape, layer),
            _layer_resident(wg.shape, layer),
            pl.BlockSpec((chunk, LANES), pos),
            pl.BlockSpec((chunk, LANES), pos),
            _layer_resident(wo.shape, layer),
            pl.BlockSpec((ts, nqk), row),
            pl.BlockSpec((ts, nqk), row),
            pl.BlockSpec((ts, nv), row),
            pl.BlockSpec((ts, nv), row),
            state_s,
        ],
        out_specs=[
            pl.BlockSpec((chunk, D_MODEL), row),
            pl.BlockSpec((1, RET_HEADS, RET_DK, RET_DV), lambda b, c: (b, 0, 0, 0)),
            pl.BlockSpec((ts, nv), row),
            state_s,
        ],
        out_shape=[
            jax.ShapeDtypeStruct((n, D_MODEL), F32),
            jax.ShapeDtypeStruct((batch, RET_HEADS, RET_DK, RET_DV), F32),
            jax.ShapeDtypeStruct((s0.shape[0] * DEC_SEQ, nv), F32),
            jax.ShapeDtypeStruct(s0.shape, F32),
        ],
        scratch_shapes=[
            pltpu.VMEM((chunk, nqk), BF16),
            pltpu.VMEM((chunk, nqk), BF16),
            pltpu.VMEM((chunk, nv), BF16),
            pltpu.VMEM((chunk, nv), BF16),
            pltpu.VMEM((chunk, nv), BF16),
            pltpu.VMEM((RET_HEADS, chunk, RET_DV), F32),
            pltpu.VMEM((RET_HEADS, RET_DK, RET_DV), BF16),
            pltpu.VMEM((nb, RET_HEADS, RET_DK, RET_DV), BF16),
        ],
        compiler_params=_params(2),
        name="ret_layer",
    )(x, gamma, wq, wk, wv, wg, cos, sin, wo, qs, ks, vs, gss, s0)


def _linear_kernel(a_ref, w_ref, b_ref, res_ref, y_ref):
    y = jnp.dot(a_ref[...].astype(BF16), w_ref[...], preferred_element_type=F32)
    y_ref[...] = res_ref[...] + y + b_ref[...]


def _linear(a, w, bias, res, tm):
    n, kdim = a.shape
    row = lambda i: (i, 0)
    return pl.pallas_call(
        _linear_kernel,
        grid=(n // tm,),
        in_specs=[
            pl.BlockSpec((tm, kdim), row),
            _resident(w.shape),
            _resident(bias.shape),
            pl.BlockSpec((tm, w.shape[1]), row),
        ],
        out_specs=pl.BlockSpec((tm, w.shape[1]), row),
        out_shape=jax.ShapeDtypeStruct((n, w.shape[1]), F32),
        compiler_params=_params(1),
        name="linear_residual",
    )(a, w, bias, res)


def _ffn_kernel(x_ref, g_ref, w1_ref, w3_ref, w2_ref, gf_ref, y_ref, *, final_norm):
    x = x_ref[...]
    h = _rms(x, g_ref[...]).astype(BF16)
    a = jnp.dot(h, w1_ref[...], preferred_element_type=F32)
    b = jnp.dot(h, w3_ref[...], preferred_element_type=F32)
    act = (_silu(a) * b).astype(BF16)
    y = x + jnp.dot(act, w2_ref[...], preferred_element_type=F32)
    if final_norm:
        y = _rms(y, gf_ref[...])
    y_ref[...] = y


def _ffn(x, gamma, w1, w3, w2, layer, gamma_final, tm, final_norm):
    n = x.shape[0]
    row = lambda i: (i, 0)
    return pl.pallas_call(
        functools.partial(_ffn_kernel, final_norm=final_norm),
        grid=(n // tm,),
        in_specs=[
            pl.BlockSpec((tm, D_MODEL), row),
            _resident((1, D_MODEL)),
            _layer_resident(w1.shape, layer),
            _layer_resident(w3.shape, layer),
            _layer_resident(w2.shape, layer),
            _resident((1, D_MODEL)),
        ],
        out_specs=pl.BlockSpec((tm, D_MODEL), row),
        out_shape=jax.ShapeDtypeStruct((n, D_MODEL), F32),
        compiler_params=_params(1),
        name="ffn",
    )(x, gamma, w1, w3, w2, gamma_final)


def _swa_proj_kernel(x_ref, g_ref, wq_ref, bq_ref, wkv_ref, bkv_ref, q_ref, kv_ref):
    h = _rms(x_ref[...], g_ref[...]).astype(BF16)
    q = jnp.dot(h, wq_ref[...], preferred_element_type=F32) + bq_ref[...]
    q_ref[...] = (q * (SWA_HEAD_DIM ** -0.5)).astype(q_ref.dtype)
    kv_ref[...] = jnp.dot(h, wkv_ref[...], preferred_element_type=F32) + bkv_ref[...]


def _swa_proj(x, gamma, wq, bq, wkv, bkv, tm, q_dtype):
    n = x.shape[0]
    nq = SWA_HEADS * SWA_HEAD_DIM
    nkv = 2 * SWA_KV_HEADS * SWA_HEAD_DIM
    row = lambda i: (i, 0)
    return pl.pallas_call(
        _swa_proj_kernel,
        grid=(n // tm,),
        in_specs=[
            pl.BlockSpec((tm, D_MODEL), row),
            _resident((1, D_MODEL)),
            _resident(wq.shape),
            _resident(bq.shape),
            _resident(wkv.shape),
            _resident(bkv.shape),
        ],
        out_specs=[pl.BlockSpec((tm, nq), row), pl.BlockSpec((tm, nkv), row)],
        out_shape=[
            jax.ShapeDtypeStruct((n, nq), q_dtype),
            jax.ShapeDtypeStruct((n, nkv), F32),
        ],
        compiler_params=_params(1),
        name="swa_proj",
    )(x, gamma, wq, bq, wkv, bkv)


def _swa_attend(q, k2, v2, prev_valid, sinks_ref):
    r = q.shape[0]
    hd = SWA_HEAD_DIM
    lane_lo = lax.broadcasted_iota(jnp.int32, (r, LANES), 1) < hd
    zero = jnp.zeros((r, LANES), q.dtype)
    pieces = []
    sink_pieces = []
    for j in range(SWA_GROUP):
        slab = q[:, j * LANES:(j + 1) * LANES]
        pieces.append(jnp.where(lane_lo, slab, zero))
        pieces.append(jnp.where(lane_lo, zero, slab))
        sink_pieces.append(jnp.full((r, 1), sinks_ref[j], F32))
        sink_pieces.append(jnp.full((r, 1), sinks_ref[SWA_GROUP + j], F32))
    qs = jnp.concatenate(pieces, axis=0).astype(BF16)
    sink = jnp.concatenate(sink_pieces, axis=0)
    rows = SWA_HEADS * r

    s2 = lax.dot_general(qs, k2, (((1,), (1,)), ((), ())), preferred_element_type=F32)
    qi = lax.broadcasted_iota(jnp.int32, (rows, WINDOW), 0) & (r - 1)
    kj = lax.broadcasted_iota(jnp.int32, (rows, WINDOW), 1)
    cur = kj <= qi
    prev_cap = jnp.where(prev_valid, jnp.float32(3.0e38), jnp.float32(NEG_INF))
    s = jnp.where(cur, s2[:, WINDOW:], jnp.minimum(s2[:, :WINDOW], prev_cap))
    m = jnp.maximum(jnp.max(s, axis=-1, keepdims=True), sink)
    p = jnp.exp(s - m)
    den = jnp.sum(p, axis=-1, keepdims=True) + jnp.exp(sink - m)
    p2 = jnp.concatenate([jnp.where(cur, 0.0, p), jnp.where(cur, p, 0.0)], axis=1).astype(BF16)
    o = jnp.dot(p2, v2, preferred_element_type=F32) / den
    outs = [jnp.where(lane_lo, o[(2 * j) * r:(2 * j + 1) * r], o[(2 * j + 1) * r:(2 * j + 2) * r])
            for j in range(SWA_GROUP)]
    return jnp.concatenate(outs, axis=1)


def _swa_prompt_kernel(sinks_ref, q_ref, kvp_ref, kvc_ref, x_ref, wo_ref, bo_ref, y_ref, o_ref,
                       *, n_sub):
    nk = SWA_KV_HEADS * SWA_HEAD_DIM
    r = WINDOW
    lane_lo = lax.broadcasted_iota(jnp.int32, (r, LANES), 1) < SWA_HEAD_DIM
    qi = lax.broadcasted_iota(jnp.int32, (r, WINDOW), 0)
    kj = lax.broadcasted_iota(jnp.int32, (r, WINDOW), 1)
    cur_mask = kj <= qi
    ones = jnp.ones((2 * WINDOW, LANES), BF16)
    nt = (((1,), (1,)), ((), ()))
    for c in range(n_sub):
        rows = slice(c * WINDOW, (c + 1) * WINDOW)
        prev = kvp_ref[...] if c == 0 else kvc_ref[(c - 1) * WINDOW:c * WINDOW, :]
        cur = kvc_ref[rows, :]
        k2 = jnp.concatenate([prev[:, :nk], cur[:, :nk]], axis=0).astype(BF16)
        v2 = jnp.concatenate([prev[:, nk:], cur[:, nk:]], axis=0).astype(BF16)
        v2_ones = jnp.concatenate([v2, ones], axis=1)
        prev_valid = (pl.program_id(1) > 0) if c == 0 else True
        prev_cap = jnp.where(prev_valid, jnp.float32(3.0e38), jnp.float32(NEG_INF))
        for j in range(SWA_GROUP):
            slab = q_ref[rows, j * LANES:(j + 1) * LANES]
            zero = jnp.zeros_like(slab)
            num = []
            den = []
            for side in range(2):
                qm = jnp.where(lane_lo, slab, zero) if side == 0 else jnp.where(lane_lo, zero, slab)
                s2 = lax.dot_general(qm, k2, nt, preferred_element_type=F32)
                s = jnp.where(cur_mask, s2[:, WINDOW:], jnp.minimum(s2[:, :WINDOW], prev_cap))
                sink = sinks_ref[j + side * SWA_GROUP]
                m = jnp.maximum(jnp.max(s, axis=-1, keepdims=True), sink)
                p = jnp.exp(s - m)
                p2 = jnp.concatenate([jnp.where(cur_mask, 0.0, p), jnp.where(cur_mask, p, 0.0)],
                                     axis=1).astype(BF16)
                o2 = jnp.dot(p2, v2_ones, preferred_element_type=F32)
                num.append(o2[:, :LANES])
                den.append(o2[:, LANES:] + jnp.exp(sink - m))
            o_ref[rows, j * LANES:(j + 1) * LANES] = (
                jnp.where(lane_lo, num[0], num[1]) / jnp.where(lane_lo, den[0], den[1])).astype(BF16)
    y = jnp.dot(o_ref[...], wo_ref[...], preferred_element_type=F32)
    y_ref[...] = x_ref[...] + y + bo_ref[...]


def _swa_prompt(q, kv, x, wo, bo, sinks, batch, n_sub):
    n = x.shape[0]
    tm = n_sub * WINDOW
    steps = n // batch // tm
    nq = SWA_HEADS * SWA_HEAD_DIM
    nkv = kv.shape[1]
    row = lambda b, i: (b * steps + i, 0)
    prev = lambda b, i: ((b * steps + i) * n_sub - jnp.minimum(i, 1), 0)
    return pl.pallas_call(
        functools.partial(_swa_prompt_kernel, n_sub=n_sub),
        grid=(batch, steps),
        in_specs=[
            pl.BlockSpec(memory_space=pltpu.SMEM),
            pl.BlockSpec((tm, nq), row),
            pl.BlockSpec((WINDOW, nkv), prev),
            pl.BlockSpec((tm, nkv), row),
            pl.BlockSpec((tm, D_MODEL), row),
            _resident(wo.shape),
            _resident(bo.shape),
        ],
        out_specs=pl.BlockSpec((tm, D_MODEL), row),
        out_shape=jax.ShapeDtypeStruct((n, D_MODEL), F32),
        scratch_shapes=[pltpu.VMEM((tm, nq), BF16)],
        compiler_params=_params(2),
        name="swa_prompt",
    )(sinks, q, kv, kv, x, wo, bo)


def _swa_sample_kernel(sinks_ref, q_ref, kvn_ref, ck_ref, cv_ref, o_ref, nk_ref, nv_ref, *, nb):
    nk = SWA_KV_HEADS * SWA_HEAD_DIM
    t = DEC_SEQ
    zpad = jnp.zeros((WINDOW - t, nk), F32)
    for b in range(nb):
        rows = slice(b * t, (b + 1) * t)
        k_new = kvn_ref[rows, :nk]
        v_new = kvn_ref[rows, nk:]
        ck = ck_ref[b]
        cv = cv_ref[b]
        k2 = jnp.concatenate([ck, k_new, zpad], axis=0).astype(BF16)
        v2 = jnp.concatenate([cv, v_new, zpad], axis=0).astype(BF16)
        o_ref[rows, :] = _swa_attend(q_ref[rows, :], k2, v2, True, sinks_ref)
        nk_ref[b, :WINDOW - t, :] = ck[t:, :]
        nk_ref[b, WINDOW - t:, :] = k_new
        nv_ref[b, :WINDOW - t, :] = cv[t:, :]
        nv_ref[b, WINDOW - t:, :] = v_new


def _swa_sample(q, kv_new, cache_k, cache_v, sinks, nb):
    batch = cache_k.shape[0]
    nq = SWA_HEADS * SWA_HEAD_DIM
    nk = SWA_KV_HEADS * SWA_HEAD_DIM
    row = lambda b: (b, 0)
    cache = pl.BlockSpec((nb, WINDOW, nk), lambda b: (b, 0, 0))
    return pl.pallas_call(
        functools.partial(_swa_sample_kernel, nb=nb),
        grid=(batch // nb,),
        in_specs=[
            pl.BlockSpec(memory_space=pltpu.SMEM),
            pl.BlockSpec((nb * DEC_SEQ, nq), row),
            pl.BlockSpec((nb * DEC_SEQ, 2 * nk), row),
            cache,
            cache,
        ],
        out_specs=[pl.BlockSpec((nb * DEC_SEQ, nq), row), cache, cache],
        out_shape=[
            jax.ShapeDtypeStruct((batch * DEC_SEQ, nq), F32),
            jax.ShapeDtypeStruct(cache_k.shape, F32),
            jax.ShapeDtypeStruct(cache_v.shape, F32),
        ],
        compiler_params=_params(1),
        name="swa_sample",
    )(sinks, q, kv_new, cache_k, cache_v)


def _rotation_tables(positions, repeat):
    half = RET_DK // 2
    inv = 1.0 / (ROPE_BASE ** (np.arange(half, dtype=np.float64) / half))
    ang = np.asarray(positions, np.float64)[:, None] * inv[None, :]
    cos = np.tile(np.cos(ang), (repeat, 1)).astype(np.float32)
    sin = np.tile(np.sin(ang), (repeat, 1)).astype(np.float32)
    return jnp.asarray(cos), jnp.asarray(sin)


TM_PROJ = 512
TM_SAMPLE = 256


def kernel(x_prompt, x_sample, state_ret, cache_swa_k, cache_swa_v, ret_w_q, ret_w_k, ret_w_v, ret_w_g, ret_w_o, swa_w_qkv, swa_b_qkv, swa_w_o, swa_b_o, swa_sinks, norm_mix, norm_ffn, ffn_w1, ffn_w3, ffn_w2, norm_final):
    bp, tp, d = x_prompt.shape
    bs, ts, _ = x_sample.shape
    assert (tp, d, ts) == (SEQ, D_MODEL, DEC_SEQ)
    xp = x_prompt.reshape(bp * tp, d)
    xs = x_sample.reshape(bs * ts, d)
    nk = SWA_KV_HEADS * SWA_HEAD_DIM

    cos_p, sin_p = _rotation_tables(np.arange(tp), 1)
    cos_s, sin_s = _rotation_tables(PAST_LEN + np.arange(ts), TM_SAMPLE // ts)
    gamma_final = norm_final.reshape(1, d)
    zero_bias = jnp.zeros((1, d), F32)

    ret_w = [w.astype(BF16) for w in (ret_w_q, ret_w_k, ret_w_v, ret_w_g)]
    w_ro = ret_w_o.astype(BF16)
    g_mix = norm_mix[0].reshape(1, d)
    qs, ks, vs, gs = _ret_proj(xs, g_mix, *ret_w, 0, cos_s, sin_s, TM_SAMPLE, F32)
    xp, state_p, gated_s, state_s = _ret_layer(
        xp, g_mix, *ret_w, w_ro, 0, cos_p, sin_p, qs, ks, vs, gs, state_ret[0], bp, RET_PROMPT_CHUNK)
    xs = _linear(gated_s, w_ro[0], zero_bias, xs, TM_SAMPLE)

    w1 = ffn_w1.astype(BF16)
    w3 = ffn_w3.astype(BF16)
    w2 = ffn_w2.astype(BF16)
    g_ffn = norm_ffn[0].reshape(1, d)
    xp = _ffn(xp, g_ffn, w1, w3, w2, 0, gamma_final, TM_PROJ, False)
    xs = _ffn(xs, g_ffn, w1, w3, w2, 0, gamma_final, TM_SAMPLE, False)

    nq = SWA_HEADS * SWA_HEAD_DIM
    heads = (SWA_KV_HEADS, SWA_GROUP, SWA_HEAD_DIM)
    w_q = swa_w_qkv[0][:, :nq].reshape(d, *heads).transpose(0, 2, 1, 3).reshape(d, nq).astype(BF16)
    b_q = swa_b_qkv[0][:nq].reshape(heads).transpose(1, 0, 2).reshape(1, nq)
    w_kv = swa_w_qkv[0][:, nq:].astype(BF16)
    b_kv = swa_b_qkv[0][nq:].reshape(1, 2 * nk)
    w_so = swa_w_o[0].reshape(*heads, d).transpose(1, 0, 2, 3).reshape(nq, d).astype(BF16)
    b_so = swa_b_o[0].reshape(1, d)
    sinks = swa_sinks[0]
    g_mix = norm_mix[1].reshape(1, d)
    qp, kvp = _swa_proj(xp, g_mix, w_q, b_q, w_kv, b_kv, TM_PROJ, BF16)
    xp = _swa_prompt(qp, kvp, xp, w_so, b_so, sinks, bp, 2)
    qs, kvs = _swa_proj(xs, g_mix, w_q, b_q, w_kv, b_kv, TM_SAMPLE, F32)
    att_s, new_k, new_v = _swa_sample(
        qs, kvs, cache_swa_k[0].reshape(bs, WINDOW, nk), cache_swa_v[0].reshape(bs, WINDOW, nk), sinks, 8)
    xs = _linear(att_s, w_so, b_so, xs, TM_SAMPLE)

    g_ffn = norm_ffn[1].reshape(1, d)
    yp = _ffn(xp, g_ffn, w1, w3, w2, 1, gamma_final, TM_PROJ, True)
    ys = _ffn(xs, g_ffn, w1, w3, w2, 1, gamma_final, TM_SAMPLE, True)

    kv_tail = kvp.reshape(bp, tp, 2 * nk)[:, tp - WINDOW:, :]
    kv_shape = (1, bp, WINDOW, SWA_KV_HEADS, SWA_HEAD_DIM)
    cache_shape = (1, bs, WINDOW, SWA_KV_HEADS, SWA_HEAD_DIM)
    return (
        yp.reshape(bp, tp, d),
        ys.reshape(bs, ts, d),
        state_p[None],
        state_s[None],
        kv_tail[..., :nk].reshape(kv_shape),
        kv_tail[..., nk:].reshape(kv_shape),
        new_k.reshape(cache_shape),
        new_v.reshape(cache_shape),
    )
```

```python
import functools
import math
from typing import NamedTuple

import numpy as np
import jax
import jax.numpy as jnp
from jax import lax
from jax.experimental import pallas as pl
from jax.experimental.pallas import tpu as pltpu

F32 = jnp.float32
BF16 = jnp.bfloat16

D_MODEL = 1024
SEQ = 2048
DEC_SEQ = 8
PAST_LEN = 16384

RET_HEADS = 4
RET_DK = 256
RET_DV = 512
RET_CHUNK = 128
RET_PROMPT_CHUNK = 256
NORM_ROWS = 64
SAMPLE_KEY_ROWS = 16
ROPE_BASE = 10000.0

SWA_HEADS = 16
SWA_KV_HEADS = 2
SWA_HEAD_DIM = 64
SWA_GROUP = SWA_HEADS // SWA_KV_HEADS
WINDOW = 128

D_FF = 2816
RMS_EPS = 1e-6
GN_EPS = 1e-6
NEG_INF = -1e30

LANES = 128
VMEM_LIMIT = 56 * 1024 * 1024


def _params(n_axes):
    return pltpu.CompilerParams(
        dimension_semantics=("arbitrary",) * n_axes, vmem_limit_bytes=VMEM_LIMIT)


def _rms(x, g):
    ms = jnp.mean(x * x, axis=-1, keepdims=True)
    return x * lax.rsqrt(ms + RMS_EPS) * g


def _silu(x):
    return x * jax.nn.sigmoid(x)


def _resident(shape):
    zeros = (0,) * len(shape)
    return pl.BlockSpec(shape, lambda *_: zeros, pipeline_mode=pl.Buffered(1))


def _layer_resident(stacked_shape, layer):
    return pl.BlockSpec((None,) + tuple(stacked_shape[1:]), lambda *_: (layer, 0, 0),
                        pipeline_mode=pl.Buffered(1))


def _ret_proj_kernel(x_ref, g_ref, wq_ref, wk_ref, wv_ref, wg_ref, cos_ref, sin_ref,
                     q_ref, k_ref, v_ref, gs_ref):
    h = _rms(x_ref[...], g_ref[...]).astype(BF16)
    cos = cos_ref[...]
    sin = sin_ref[...]
    half = RET_DK // 2

    def rotated(w_ref, out_ref, scale):
        y = jnp.dot(h, w_ref[...], preferred_element_type=F32)
        for hd in range(RET_HEADS):
            lo = hd * RET_DK
            a = y[:, lo:lo + half]
            b = y[:, lo + half:lo + RET_DK]
            out_ref[:, lo:lo + half] = ((a * cos - b * sin) * scale).astype(out_ref.dtype)
            out_ref[:, lo + half:lo + RET_DK] = ((a * sin + b * cos) * scale).astype(out_ref.dtype)

    g = jnp.dot(h, wg_ref[...], preferred_element_type=F32)
    gs_ref[...] = _silu(g).astype(gs_ref.dtype)
    v_ref[...] = jnp.dot(h, wv_ref[...], preferred_element_type=F32).astype(v_ref.dtype)
    rotated(wq_ref, q_ref, 1.0)
    rotated(wk_ref, k_ref, RET_DK ** -0.5)


def _ret_proj(x, gamma, wq, wk, wv, wg, layer, cos, sin, tm, out_dtype):
    n = x.shape[0]
    nqk = RET_HEADS * RET_DK
    nv = RET_HEADS * RET_DV
    n_pos_blocks = cos.shape[0] // tm
    row = lambda i: (i, 0)
    return pl.pallas_call(
        _ret_proj_kernel,
        grid=(n // tm,),
        in_specs=[
            pl.BlockSpec((tm, D_MODEL), row),
            _resident((1, D_MODEL)),
            _layer_resident(wq.shape, layer),
            _layer_resident(wk.shape, layer),
            _layer_resident(wv.shape, layer),
            _layer_resident(wg.shape, layer),
            pl.BlockSpec((tm, LANES), lambda i: (i % n_pos_blocks, 0)),
            pl.BlockSpec((tm, LANES), lambda i: (i % n_pos_blocks, 0)),
        ],
        out_specs=[
            pl.BlockSpec((tm, nqk), row),
            pl.BlockSpec((tm, nqk), row),
            pl.BlockSpec((tm, nv), row),
            pl.BlockSpec((tm, nv), row),
        ],
        out_shape=[
            jax.ShapeDtypeStruct((n, nqk), out_dtype),
            jax.ShapeDtypeStruct((n, nqk), out_dtype),
            jax.ShapeDtypeStruct((n, nv), out_dtype),
            jax.ShapeDtypeStruct((n, nv), out_dtype),
        ],
        compiler_params=_params(1),
        name="ret_proj",
    )(x, gamma, wq, wk, wv, wg, cos, sin)


class _RetUnit(NamedTuple):
    qc: jax.Array
    kc: jax.Array
    vc: jax.Array
    gsc: jax.Array
    s_in: object
    s_out: object
    s_bf_in: object
    s_bf_out: object
    chunk: int


def _ret_head(u, hd):
    return (u.qc[:, hd * RET_DK:(hd + 1) * RET_DK], u.kc[:, hd * RET_DK:(hd + 1) * RET_DK],
            u.vc[:, hd * RET_DV:(hd + 1) * RET_DV])


def _ret_scores(u, hd):
    qh, kh, _ = _ret_head(u, hd)
    return lax.dot_general(qh, kh, (((1,), (1,)), ((), ())), preferred_element_type=F32)


def _ret_output(u, hd, scores):
    qh, _, vh = _ret_head(u, hd)
    cq, ck = scores.shape
    log_gamma = math.log1p(-(2.0 ** -(5.0 + hd)))
    ii = lax.broadcasted_iota(jnp.int32, (cq, ck), 0)
    jj = lax.broadcasted_iota(jnp.int32, (cq, ck), 1)
    diff = (ii - jj).astype(F32)
    intra = jnp.where(diff >= 0, jnp.exp(log_gamma * jnp.maximum(diff, 0.0)), 0.0)
    irow = lax.broadcasted_iota(jnp.int32, (cq, 1), 0).astype(F32)
    q_dec = (qh.astype(F32) * jnp.exp(log_gamma * (irow + 1.0))).astype(BF16)
    lhs = jnp.concatenate([q_dec, (scores * intra).astype(BF16)], axis=1)
    rhs = jnp.concatenate([u.s_bf_in[hd], vh], axis=0)
    return jnp.dot(lhs, rhs, preferred_element_type=F32)


def _ret_update_state(u, hd):
    _, kh, vh = _ret_head(u, hd)
    log_gamma = math.log1p(-(2.0 ** -(5.0 + hd)))
    jrow = lax.broadcasted_iota(jnp.int32, (kh.shape[0], 1), 0).astype(F32)
    kd = (kh.astype(F32) * jnp.exp(log_gamma * (u.chunk - 1.0 - jrow))).astype(BF16)
    s_new = u.s_in[hd] * math.exp(log_gamma * u.chunk) + lax.dot_general(
        kd, vh, (((0,), (0,)), ((), ())), preferred_element_type=F32)
    u.s_out[hd] = s_new
    if u.s_bf_out is not None:
        u.s_bf_out[hd] = s_new.astype(BF16)


def _ret_gate(o, gs):
    mu = jnp.mean(o, axis=-1, keepdims=True)
    oc = o - mu
    var = jnp.mean(oc * oc, axis=-1, keepdims=True)
    return oc * lax.rsqrt(var + GN_EPS) * gs.astype(F32)


def _ret_layer_kernel(x_ref, g_ref, wq_ref, wk_ref, wv_ref, wg_ref, cos_ref, sin_ref, wo_ref,
                      qs_ref, ks_ref, vs_ref, gss_ref, s0_ref,
                      y_ref, sp_ref, gated_s_ref, ss_ref,
                      q_scr, k_scr, v_scr, gs_scr, gated_ref, o_scr, sp_bf, ss_bf, *, chunk, nb):
    @pl.when(pl.program_id(1) == 0)
    def _():
        sp_ref[...] = jnp.zeros_like(sp_ref)
        sp_bf[...] = jnp.zeros_like(sp_bf)

    heads = range(RET_HEADS)
    pad = SAMPLE_KEY_ROWS - DEC_SEQ
    kpad = jnp.zeros((pad, ks_ref.shape[1]), F32)
    vpad = jnp.zeros((pad, vs_ref.shape[1]), F32)
    samples = []
    for b in range(nb):
        rows = slice(b * DEC_SEQ, (b + 1) * DEC_SEQ)
        samples.append(_RetUnit(
            qs_ref[rows, :].astype(BF16),
            jnp.concatenate([ks_ref[rows, :], kpad], axis=0).astype(BF16),
            jnp.concatenate([vs_ref[rows, :], vpad], axis=0).astype(BF16),
            gss_ref[rows, :], s0_ref.at[b], ss_ref.at[b], ss_bf.at[b], None, DEC_SEQ))
    scores_s = [[_ret_scores(u, hd) for hd in heads] for u in samples]
    for u in samples:
        for hd in heads:
            _ret_update_state(u, hd)
    for b in range(nb):
        for hd in heads:
            ss_bf[b, hd] = s0_ref[b, hd].astype(BF16)

    _ret_proj_kernel(x_ref, g_ref, wq_ref, wk_ref, wv_ref, wg_ref, cos_ref, sin_ref,
                     q_scr, k_scr, v_scr, gs_scr)
    prompt = _RetUnit(q_scr[...], k_scr[...], v_scr[...], gs_scr[...],
                      sp_ref.at[0], sp_ref.at[0], sp_bf, sp_bf, chunk)
    scores_p = [_ret_scores(prompt, hd) for hd in heads]
    o_s = [[_ret_output(u, hd, sc[hd]) for hd in heads] for u, sc in zip(samples, scores_s)]
    for hd in heads:
        o_scr[hd] = _ret_output(prompt, hd, scores_p[hd])
    for hd in heads:
        _ret_update_state(prompt, hd)

    for hd in heads:
        cols = slice(hd * RET_DV, (hd + 1) * RET_DV)
        for r0 in range(0, chunk, NORM_ROWS):
            rows = slice(r0, r0 + NORM_ROWS)
            gated_ref[rows, cols] = _ret_gate(o_scr[hd, rows, :], gs_scr[rows, cols]).astype(BF16)
        for b, u in enumerate(samples):
            rows = slice(b * DEC_SEQ, (b + 1) * DEC_SEQ)
            gated_s_ref[rows, cols] = _ret_gate(o_s[b][hd], u.gsc[:, cols])
    y_ref[...] = x_ref[...] + jnp.dot(gated_ref[...], wo_ref[...], preferred_element_type=F32)


def _ret_layer(x, gamma, wq, wk, wv, wg, wo, layer, cos, sin, qs, ks, vs, gss, s0, batch, chunk):
    n = x.shape[0]
    steps = n // batch // chunk
    assert cos.shape[0] == steps * chunk
    nb = s0.shape[0] // (batch * steps)
    assert nb * batch * steps == s0.shape[0]
    nqk = RET_HEADS * RET_DK
    nv = RET_HEADS * RET_DV
    ts = nb * DEC_SEQ
    row = lambda b, c: (b * steps + c, 0)
    pos = lambda b, c: (c, 0)
    state_s = pl.BlockSpec((nb, RET_HEADS, RET_DK, RET_DV), lambda b, c: (b * steps + c, 0, 0, 0))
    return pl.pallas_call(
        functools.partial(_ret_layer_kernel, chunk=chunk, nb=nb),
        grid=(batch, steps),
        in_specs=[
            pl.BlockSpec((chunk, D_MODEL), row),
            _resident((1, D_MODEL)),
            _layer_resident(wq.shape, layer),
            _layer_resident(wk.shape, layer),
            _layer_resident(wv.shape, layer),
            _layer_resident(wg.shape, layer),
            pl.BlockSpec((chunk, LANES), pos),
            pl.BlockSpec((chunk, LANES), pos),
            _layer_resident(wo.shape, layer),
            pl.BlockSpec((ts, nqk), row),
            pl.BlockSpec((ts, nqk), row),
            pl.BlockSpec((ts, nv), row),
            pl.BlockSpec((ts, nv), row),
            state_s,
        ],
        out_specs=[
            pl.BlockSpec((chunk, D_MODEL), row),
            pl.BlockSpec((1, RET_HEADS, RET_DK, RET_DV), lambda b, c: (b, 0, 0, 0)),
            pl.BlockSpec((ts, nv), row),
            state_s,
        ],
        out_shape=[
            jax.ShapeDtypeStruct((n, D_MODEL), F32),
            jax.ShapeDtypeStruct((batch, RET_HEADS, RET_DK, RET_DV), F32),
            jax.ShapeDtypeStruct((s0.shape[0] * DEC_SEQ, nv), F32),
            jax.ShapeDtypeStruct(s0.shape, F32),
        ],
        scratch_shapes=[
            pltpu.VMEM((chunk, nqk), BF16),
            pltpu.VMEM((chunk, nqk), BF16),
            pltpu.VMEM((chunk, nv), BF16),
            pltpu.VMEM((chunk, nv), BF16),
            pltpu.VMEM((chunk, nv), BF16),
            pltpu.VMEM((RET_HEADS, chunk, RET_DV), F32),
            pltpu.VMEM((RET_HEADS, RET_DK, RET_DV), BF16),
            pltpu.VMEM((nb, RET_HEADS, RET_DK, RET_DV), BF16),
        ],
        compiler_params=_params(2),
        name="ret_layer",
    )(x, gamma, wq, wk, wv, wg, cos, sin, wo, qs, ks, vs, gss, s0)


def _linear_kernel(a_ref, w_ref, b_ref, res_ref, y_ref):
    y = jnp.dot(a_ref[...].astype(BF16), w_ref[...], preferred_element_type=F32)
    y_ref[...] = res_ref[...] + y + b_ref[...]


def _linear(a, w, bias, res, tm):
    n, kdim = a.shape
    row = lambda i: (i, 0)
    return pl.pallas_call(
        _linear_kernel,
        grid=(n // tm,),
        in_specs=[
            pl.BlockSpec((tm, kdim), row),
            _resident(w.shape),
            _resident(bias.shape),
            pl.BlockSpec((tm, w.shape[1]), row),
        ],
        out_specs=pl.BlockSpec((tm, w.shape[1]), row),
        out_shape=jax.ShapeDtypeStruct((n, w.shape[1]), F32),
        compiler_params=_params(1),
        name="linear_residual",
    )(a, w, bias, res)


def _ffn_kernel(x_ref, g_ref, w1_ref, w3_ref, w2_ref, gf_ref, y_ref, *, final_norm):
    x = x_ref[...]
    h = _rms(x, g_ref[...]).astype(BF16)
    a = jnp.dot(h, w1_ref[...], preferred_element_type=F32)
    b = jnp.dot(h, w3_ref[...], preferred_element_type=F32)
    act = (_silu(a) * b).astype(BF16)
    y = x + jnp.dot(act, w2_ref[...], preferred_element_type=F32)
    if final_norm:
        y = _rms(y, gf_ref[...])
    y_ref[...] = y


def _ffn(x, gamma, w1, w3, w2, layer, gamma_final, tm, final_norm):
    n = x.shape[0]
    row = lambda i: (i, 0)
    return pl.pallas_call(
        functools.partial(_ffn_kernel, final_norm=final_norm),
        grid=(n // tm,),
        in_specs=[
            pl.BlockSpec((tm, D_MODEL), row),
            _resident((1, D_MODEL)),
            _layer_resident(w1.shape, layer),
            _layer_resident(w3.shape, layer),
            _layer_resident(w2.shape, layer),
            _resident((1, D_MODEL)),
        ],
        out_specs=pl.BlockSpec((tm, D_MODEL), row),
        out_shape=jax.ShapeDtypeStruct((n, D_MODEL), F32),
        compiler_params=_params(1),
        name="ffn",
    )(x, gamma, w1, w3, w2, gamma_final)


def _swa_proj_kernel(x_ref, g_ref, wq_ref, bq_ref, wkv_ref, bkv_ref, q_ref, kv_ref):
    h = _rms(x_ref[...], g_ref[...]).astype(BF16)
    q = jnp.dot(h, wq_ref[...], preferred_element_type=F32) + bq_ref[...]
    q_ref[...] = (q * (SWA_HEAD_DIM ** -0.5)).astype(q_ref.dtype)
    kv_ref[...] = jnp.dot(h, wkv_ref[...], preferred_element_type=F32) + bkv_ref[...]


def _swa_proj(x, gamma, wq, bq, wkv, bkv, tm, q_dtype):
    n = x.shape[0]
    nq = SWA_HEADS * SWA_HEAD_DIM
    nkv = 2 * SWA_KV_HEADS * SWA_HEAD_DIM
    row = lambda i: (i, 0)
    return pl.pallas_call(
        _swa_proj_kernel,
        grid=(n // tm,),
        in_specs=[
            pl.BlockSpec((tm, D_MODEL), row),
            _resident((1, D_MODEL)),
            _resident(wq.shape),
            _resident(bq.shape),
            _resident(wkv.shape),
            _resident(bkv.shape),
        ],
        out_specs=[pl.BlockSpec((tm, nq), row), pl.BlockSpec((tm, nkv), row)],
        out_shape=[
            jax.ShapeDtypeStruct((n, nq), q_dtype),
            jax.ShapeDtypeStruct((n, nkv), F32),
        ],
        compiler_params=_params(1),
        name="swa_proj",
    )(x, gamma, wq, bq, wkv, bkv)


def _swa_attend(q, k2, v2, prev_valid, sinks_ref):
    r = q.shape[0]
    hd = SWA_HEAD_DIM
    lane_lo = lax.broadcasted_iota(jnp.int32, (r, LANES), 1) < hd
    zero = jnp.zeros((r, LANES), q.dtype)
    pieces = []
    sink_pieces = []
    for j in range(SWA_GROUP):
        slab = q[:, j * LANES:(j + 1) * LANES]
        pieces.append(jnp.where(lane_lo, slab, zero))
        pieces.append(jnp.where(lane_lo, zero, slab))
        sink_pieces.append(jnp.full((r, 1), sinks_ref[j], F32))
        sink_pieces.append(jnp.full((r, 1), sinks_ref[SWA_GROUP + j], F32))
    qs = jnp.concatenate(pieces, axis=0).astype(BF16)
    sink = jnp.concatenate(sink_pieces, axis=0)
    rows = SWA_HEADS * r

    s2 = lax.dot_general(qs, k2, (((1,), (1,)), ((), ())), preferred_element_type=F32)
    qi = lax.broadcasted_iota(jnp.int32, (rows, WINDOW), 0) & (r - 1)
    kj = lax.broadcasted_iota(jnp.int32, (rows, WINDOW), 1)
    cur = kj <= qi
    prev_cap = jnp.where(prev_valid, jnp.float32(3.0e38), jnp.float32(NEG_INF))
    s = jnp.where(cur, s2[:, WINDOW:], jnp.minimum(s2[:, :WINDOW], prev_cap))
    m = jnp.maximum(jnp.max(s, axis=-1, keepdims=True), sink)
    p = jnp.exp(s - m)
    den = jnp.sum(p, axis=-1, keepdims=True) + jnp.exp(sink - m)
    p2 = jnp.concatenate([jnp.where(cur, 0.0, p), jnp.where(cur, p, 0.0)], axis=1).astype(BF16)
    o = jnp.dot(p2, v2, preferred_element_type=F32) / den
    outs = [jnp.where(lane_lo, o[(2 * j) * r:(2 * j + 1) * r], o[(2 * j + 1) * r:(2 * j + 2) * r])
            for j in range(SWA_GROUP)]
    return jnp.concatenate(outs, axis=1)


def _swa_prompt_kernel(sinks_ref, q_ref, kvp_ref, kvc_ref, x_ref, wo_ref, bo_ref, y_ref, o_ref,
                       *, n_sub):
    nk = SWA_KV_HEADS * SWA_HEAD_DIM
    r = WINDOW
    lane_lo = lax.broadcasted_iota(jnp.int32, (r, LANES), 1) < SWA_HEAD_DIM
    qi = lax.broadcasted_iota(jnp.int32, (r, WINDOW), 0)
    kj = lax.broadcasted_iota(jnp.int32, (r, WINDOW), 1)
    cur_mask = kj <= qi
    ones = jnp.ones((2 * WINDOW, LANES), BF16)
    nt = (((1,), (1,)), ((), ()))
    for c in range(n_sub):
        rows = slice(c * WINDOW, (c + 1) * WINDOW)
        prev = kvp_ref[...] if c == 0 else kvc_ref[(c - 1) * WINDOW:c * WINDOW, :]
        cur = kvc_ref[rows, :]
        k2 = jnp.concatenate([prev[:, :nk], cur[:, :nk]], axis=0).astype(BF16)
        v2 = jnp.concatenate([prev[:, nk:], cur[:, nk:]], axis=0).astype(BF16)
        v2_ones = jnp.concatenate([v2, ones], axis=1)
        prev_valid = (pl.program_id(1) > 0) if c == 0 else True
        prev_cap = jnp.where(prev_valid, jnp.float32(3.0e38), jnp.float32(NEG_INF))
        for j in range(SWA_GROUP):
            slab = q_ref[rows, j * LANES:(j + 1) * LANES]
            zero = jnp.zeros_like(slab)
            num = []
            den = []
            for side in range(2):
                qm = jnp.where(lane_lo, slab, zero) if side == 0 else jnp.where(lane_lo, zero, slab)
                s2 = lax.dot_general(qm, k2, nt, preferred_element_type=F32)
                s = jnp.where(cur_mask, s2[:, WINDOW:], jnp.minimum(s2[:, :WINDOW], prev_cap))
                sink = sinks_ref[j + side * SWA_GROUP]
                m = jnp.maximum(jnp.max(s, axis=-1, keepdims=True), sink)
                p = jnp.exp(s - m)
                p2 = jnp.concatenate([jnp.where(cur_mask, 0.0, p), jnp.where(cur_mask, p, 0.0)],
                                     axis=1).astype(BF16)
                o2 = jnp.dot(p2, v2_ones, preferred_element_type=F32)
                num.append(o2[:, :LANES])
                den.append(o2[:, LANES:] + jnp.exp(sink - m))
            o_ref[rows, j * LANES:(j + 1) * LANES] = (
                jnp.where(lane_lo, num[0], num[1]) / jnp.where(lane_lo, den[0], den[1])).astype(BF16)
    y = jnp.dot(o_ref[...], wo_ref[...], preferred_element_type=F32)
    y_ref[...] = x_ref[...] + y + bo_ref[...]


def _swa_prompt(q, kv, x, wo, bo, sinks, batch, n_sub):
    n = x.shape[0]
    tm = n_sub * WINDOW
    steps = n // batch // tm
    nq = SWA_HEADS * SWA_HEAD_DIM
    nkv = kv.shape[1]
    row = lambda b, i: (b * steps + i, 0)
    prev = lambda b, i: ((b * steps + i) * n_sub - jnp.minimum(i, 1), 0)
    return pl.pallas_call(
        functools.partial(_swa_prompt_kernel, n_sub=n_sub),
        grid=(batch, steps),
        in_specs=[
            pl.BlockSpec(memory_space=pltpu.SMEM),
            pl.BlockSpec((tm, nq), row),
            pl.BlockSpec((WINDOW, nkv), prev),
            pl.BlockSpec((tm, nkv), row),
            pl.BlockSpec((tm, D_MODEL), row),
            _resident(wo.shape),
            _resident(bo.shape),
        ],
        out_specs=pl.BlockSpec((tm, D_MODEL), row),
        out_shape=jax.ShapeDtypeStruct((n, D_MODEL), F32),
        scratch_shapes=[pltpu.VMEM((tm, nq), BF16)],
        compiler_params=_params(2),
        name="swa_prompt",
    )(sinks, q, kv, kv, x, wo, bo)


def _swa_sample_kernel(sinks_ref, q_ref, kvn_ref, ck_ref, cv_ref, o_ref, nk_ref, nv_ref, *, nb):
    nk = SWA_KV_HEADS * SWA_HEAD_DIM
    t = DEC_SEQ
    zpad = jnp.zeros((WINDOW - t, nk), F32)
    for b in range(nb):
        rows = slice(b * t, (b + 1) * t)
        k_new = kvn_ref[rows, :nk]
        v_new = kvn_ref[rows, nk:]
        ck = ck_ref[b]
        cv = cv_ref[b]
        k2 = jnp.concatenate([ck, k_new, zpad], axis=0).astype(BF16)
        v2 = jnp.concatenate([cv, v_new, zpad], axis=0).astype(BF16)
        o_ref[rows, :] = _swa_attend(q_ref[rows, :], k2, v2, True, sinks_ref)
        nk_ref[b, :WINDOW - t, :] = ck[t:, :]
        nk_ref[b, WINDOW - t:, :] = k_new
        nv_ref[b, :WINDOW - t, :] = cv[t:, :]
        nv_ref[b, WINDOW - t:, :] = v_new


def _swa_sample(q, kv_new, cache_k, cache_v, sinks, nb):
    batch = cache_k.shape[0]
    nq = SWA_HEADS * SWA_HEAD_DIM
    nk = SWA_KV_HEADS * SWA_HEAD_DIM
    row = lambda b: (b, 0)
    cache = pl.BlockSpec((nb, WINDOW, nk), lambda b: (b, 0, 0))
    return pl.pallas_call(
        functools.partial(_swa_sample_kernel, nb=nb),
        grid=(batch // nb,),
        in_specs=[
            pl.BlockSpec(memory_space=pltpu.SMEM),
            pl.BlockSpec((nb * DEC_SEQ, nq), row),
            pl.BlockSpec((nb * DEC_SEQ, 2 * nk), row),
            cache,
            cache,
        ],
        out_specs=[pl.BlockSpec((nb * DEC_SEQ, nq), row), cache, cache],
        out_shape=[
            jax.ShapeDtypeStruct((batch * DEC_SEQ, nq), F32),
            jax.ShapeDtypeStruct(cache_k.shape, F32),
            jax.ShapeDtypeStruct(cache_v.shape, F32),
        ],
        compiler_params=_params(1),
        name="swa_sample",
    )(sinks, q, kv_new, cache_k, cache_v)


def _rotation_tables(positions, repeat):
    half = RET_DK // 2
    inv = 1.0 / (ROPE_BASE ** (np.arange(half, dtype=np.float64) / half))
    ang = np.asarray(positions, np.float64)[:, None] * inv[None, :]
    cos = np.tile(np.cos(ang), (repeat, 1)).astype(np.float32)
    sin = np.tile(np.sin(ang), (repeat, 1)).astype(np.float32)
    return jnp.asarray(cos), jnp.asarray(sin)


TM_PROJ = 512
TM_SAMPLE = 512


def kernel(x_prompt, x_sample, state_ret, cache_swa_k, cache_swa_v, ret_w_q, ret_w_k, ret_w_v, ret_w_g, ret_w_o, swa_w_qkv, swa_b_qkv, swa_w_o, swa_b_o, swa_sinks, norm_mix, norm_ffn, ffn_w1, ffn_w3, ffn_w2, norm_final):
    bp, tp, d = x_prompt.shape
    bs, ts, _ = x_sample.shape
    assert (tp, d, ts) == (SEQ, D_MODEL, DEC_SEQ)
    xp = x_prompt.reshape(bp * tp, d)
    xs = x_sample.reshape(bs * ts, d)
    nk = SWA_KV_HEADS * SWA_HEAD_DIM

    cos_p, sin_p = _rotation_tables(np.arange(tp), 1)
    cos_s, sin_s = _rotation_tables(PAST_LEN + np.arange(ts), TM_SAMPLE // ts)
    gamma_final = norm_final.reshape(1, d)
    zero_bias = jnp.zeros((1, d), F32)

    ret_w = [w.astype(BF16) for w in (ret_w_q, ret_w_k, ret_w_v, ret_w_g)]
    w_ro = ret_w_o.astype(BF16)
    g_mix = norm_mix[0].reshape(1, d)
    qs, ks, vs, gs = _ret_proj(xs, g_mix, *ret_w, 0, cos_s, sin_s, TM_SAMPLE, F32)
    xp, state_p, gated_s, state_s = _ret_layer(
        xp, g_mix, *ret_w, w_ro, 0, cos_p, sin_p, qs, ks, vs, gs, state_ret[0], bp, RET_PROMPT_CHUNK)
    xs = _linear(gated_s, w_ro[0], zero_bias, xs, TM_SAMPLE)

    w1 = ffn_w1.astype(BF16)
    w3 = ffn_w3.astype(BF16)
    w2 = ffn_w2.astype(BF16)
    g_ffn = norm_ffn[0].reshape(1, d)
    xp = _ffn(xp, g_ffn, w1, w3, w2, 0, gamma_final, TM_PROJ, False)
    xs = _ffn(xs, g_ffn, w1, w3, w2, 0, gamma_final, TM_SAMPLE, False)

    nq = SWA_HEADS * SWA_HEAD_DIM
    heads = (SWA_KV_HEADS, SWA_GROUP, SWA_HEAD_DIM)
    w_q = swa_w_qkv[0][:, :nq].reshape(d, *heads).transpose(0, 2, 1, 3).reshape(d, nq).astype(BF16)
    b_q = swa_b_qkv[0][:nq].reshape(heads).transpose(1, 0, 2).reshape(1, nq)
    w_kv = swa_w_qkv[0][:, nq:].astype(BF16)
    b_kv = swa_b_qkv[0][nq:].reshape(1, 2 * nk)
    w_so = swa_w_o[0].reshape(*heads, d).transpose(1, 0, 2, 3).reshape(nq, d).astype(BF16)
    b_so = swa_b_o[0].reshape(1, d)
    sinks = swa_sinks[0]
    g_mix = norm_mix[1].reshape(1, d)
    qp, kvp = _swa_proj(xp, g_mix, w_q, b_q, w_kv, b_kv, TM_PROJ, BF16)
    xp = _swa_prompt(qp, kvp, xp, w_so, b_so, sinks, bp, 2)
    qs, kvs = _swa_proj(xs, g_mix, w_q, b_q, w_kv, b_kv, TM_SAMPLE, F32)
    att_s, new_k, new_v = _swa_sample(
        qs, kvs, cache_swa_k[0].reshape(bs, WINDOW, nk), cache_swa_v[0].reshape(bs, WINDOW, nk), sinks, 8)
    xs = _linear(att_s, w_so, b_so, xs, TM_SAMPLE)

    g_ffn = norm_ffn[1].reshape(1, d)
    yp = _ffn(xp, g_ffn, w1, w3, w2, 1, gamma_final, TM_PROJ, True)
    ys = _ffn(xs, g_ffn, w1, w3, w2, 1, gamma_final, TM_SAMPLE, True)

    kv_tail = kvp.reshape(bp, tp, 2 * nk)[:, tp - WINDOW:, :]
    kv_shape = (1, bp, WINDOW, SWA_KV_HEADS, SWA_HEAD_DIM)
    cache_shape = (1, bs, WINDOW, SWA_KV_HEADS, SWA_HEAD_DIM)
    return (
        yp.reshape(bp, tp, d),
        ys.reshape(bs, ts, d),
        state_p[None],
        state_s[None],
        kv_tail[..., :nk].reshape(kv_shape),
        kv_tail[..., nk:].reshape(kv_shape),
        new_k.reshape(cache_shape),
        new_v.reshape(cache_shape),
    )
```

```python
import functools
import math
from typing import NamedTuple

import numpy as np
import jax
import jax.numpy as jnp
from jax import lax
from jax.experimental import pallas as pl
from jax.experimental.pallas import tpu as pltpu

F32 = jnp.float32
BF16 = jnp.bfloat16

D_MODEL = 1024
SEQ = 2048
DEC_SEQ = 8
PAST_LEN = 16384

RET_HEADS = 4
RET_DK = 256
RET_DV = 512
RET_CHUNK = 128
RET_PROMPT_CHUNK = 256
NORM_ROWS = 64
SAMPLE_KEY_ROWS = 16
ROPE_BASE = 10000.0

SWA_HEADS = 16
SWA_KV_HEADS = 2
SWA_HEAD_DIM = 64
SWA_GROUP = SWA_HEADS // SWA_KV_HEADS
WINDOW = 128

D_FF = 2816
RMS_EPS = 1e-6
GN_EPS = 1e-6
NEG_INF = -1e30

LANES = 128
VMEM_LIMIT = 56 * 1024 * 1024


def _params(n_axes):
    return pltpu.CompilerParams(
        dimension_semantics=("arbitrary",) * n_axes, vmem_limit_bytes=VMEM_LIMIT)


def _rms(x, g):
    ms = jnp.mean(x * x, axis=-1, keepdims=True)
    return x * lax.rsqrt(ms + RMS_EPS) * g


def _silu(x):
    return x * jax.nn.sigmoid(x)


def _resident(shape):
    zeros = (0,) * len(shape)
    return pl.BlockSpec(shape, lambda *_: zeros, pipeline_mode=pl.Buffered(1))


def _layer_resident(stacked_shape, layer):
    return pl.BlockSpec((None,) + tuple(stacked_shape[1:]), lambda *_: (layer, 0, 0),
                        pipeline_mode=pl.Buffered(1))


def _ret_proj_kernel(x_ref, g_ref, wq_ref, wk_ref, wv_ref, wg_ref, cos_ref, sin_ref,
                     q_ref, k_ref, v_ref, gs_ref):
    h = _rms(x_ref[...], g_ref[...]).astype(BF16)
    _ret_projections(h, wq_ref, wk_ref, wv_ref, wg_ref, cos_ref, sin_ref, q_ref, k_ref, v_ref, gs_ref)


def _ret_projections(h, wq_ref, wk_ref, wv_ref, wg_ref, cos_ref, sin_ref, q_ref, k_ref, v_ref, gs_ref):
    cos = cos_ref[...]
    sin = sin_ref[...]
    half = RET_DK // 2

    def rotated(w_ref, out_ref, scale):
        y = jnp.dot(h, w_ref[...], preferred_element_type=F32)
        for hd in range(RET_HEADS):
            lo = hd * RET_DK
            a = y[:, lo:lo + half]
            b = y[:, lo + half:lo + RET_DK]
            out_ref[:, lo:lo + half] = ((a * cos - b * sin) * scale).astype(out_ref.dtype)
            out_ref[:, lo + half:lo + RET_DK] = ((a * sin + b * cos) * scale).astype(out_ref.dtype)

    g = jnp.dot(h, wg_ref[...], preferred_element_type=F32)
    gs_ref[...] = _silu(g).astype(gs_ref.dtype)
    v_ref[...] = jnp.dot(h, wv_ref[...], preferred_element_type=F32).astype(v_ref.dtype)
    rotated(wq_ref, q_ref, 1.0)
    rotated(wk_ref, k_ref, RET_DK ** -0.5)


def _ret_proj(x, gamma, wq, wk, wv, wg, layer, cos, sin, tm, out_dtype):
    n = x.shape[0]
    nqk = RET_HEADS * RET_DK
    nv = RET_HEADS * RET_DV
    n_pos_blocks = cos.shape[0] // tm
    row = lambda i: (i, 0)
    return pl.pallas_call(
        _ret_proj_kernel,
        grid=(n // tm,),
        in_specs=[
            pl.BlockSpec((tm, D_MODEL), row),
            _resident((1, D_MODEL)),
            _layer_resident(wq.shape, layer),
            _layer_resident(wk.shape, layer),
            _layer_resident(wv.shape, layer),
            _layer_resident(wg.shape, layer),
            pl.BlockSpec((tm, LANES), lambda i: (i % n_pos_blocks, 0)),
            pl.BlockSpec((tm, LANES), lambda i: (i % n_pos_blocks, 0)),
        ],
        out_specs=[
            pl.BlockSpec((tm, nqk), row),
            pl.BlockSpec((tm, nqk), row),
            pl.BlockSpec((tm, nv), row),
            pl.BlockSpec((tm, nv), row),
        ],
        out_shape=[
            jax.ShapeDtypeStruct((n, nqk), out_dtype),
            jax.ShapeDtypeStruct((n, nqk), out_dtype),
            jax.ShapeDtypeStruct((n, nv), out_dtype),
            jax.ShapeDtypeStruct((n, nv), out_dtype),
        ],
        compiler_params=_params(1),
        name="ret_proj",
    )(x, gamma, wq, wk, wv, wg, cos, sin)


class _RetUnit(NamedTuple):
    qc: jax.Array
    kc: jax.Array
    vc: jax.Array
    gsc: jax.Array
    s_in: object
    s_out: object
    s_bf_in: object
    s_bf_out: object
    chunk: int


def _ret_head(u, hd):
    return (u.qc[:, hd * RET_DK:(hd + 1) * RET_DK], u.kc[:, hd * RET_DK:(hd + 1) * RET_DK],
            u.vc[:, hd * RET_DV:(hd + 1) * RET_DV])


def _ret_scores(u, hd):
    qh, kh, _ = _ret_head(u, hd)
    return lax.dot_general(qh, kh, (((1,), (1,)), ((), ())), preferred_element_type=F32)


def _ret_output(u, hd, scores):
    qh, _, vh = _ret_head(u, hd)
    cq, ck = scores.shape
    log_gamma = math.log1p(-(2.0 ** -(5.0 + hd)))
    ii = lax.broadcasted_iota(jnp.int32, (cq, ck), 0)
    jj = lax.broadcasted_iota(jnp.int32, (cq, ck), 1)
    diff = (ii - jj).astype(F32)
    intra = jnp.where(diff >= 0, jnp.exp(log_gamma * jnp.maximum(diff, 0.0)), 0.0)
    irow = lax.broadcasted_iota(jnp.int32, (cq, 1), 0).astype(F32)
    q_dec = (qh.astype(F32) * jnp.exp(log_gamma * (irow + 1.0))).astype(BF16)
    lhs = jnp.concatenate([q_dec, (scores * intra).astype(BF16)], axis=1)
    rhs = jnp.concatenate([u.s_bf_in[hd], vh], axis=0)
    return jnp.dot(lhs, rhs, preferred_element_type=F32)


def _ret_decayed_keys(u, hd):
    _, kh, _ = _ret_head(u, hd)
    log_gamma = math.log1p(-(2.0 ** -(5.0 + hd)))
    jrow = lax.broadcasted_iota(jnp.int32, (kh.shape[0], 1), 0).astype(F32)
    return kh.astype(F32) * jnp.exp(log_gamma * (u.chunk - 1.0 - jrow))


def _ret_update_state(u, hd, kd_t=None):
    _, _, vh = _ret_head(u, hd)
    log_gamma = math.log1p(-(2.0 ** -(5.0 + hd)))
    if kd_t is None:
        update = lax.dot_general(_ret_decayed_keys(u, hd).astype(BF16), vh, (((0,), (0,)), ((), ())),
                                 preferred_element_type=F32)
    else:
        update = jnp.dot(kd_t, vh, preferred_element_type=F32)
    s_new = u.s_in[hd] * math.exp(log_gamma * u.chunk) + update
    u.s_out[hd] = s_new
    if u.s_bf_out is not None:
        u.s_bf_out[hd] = s_new.astype(BF16)


def _ret_gate(o, gs):
    mu = jnp.mean(o, axis=-1, keepdims=True)
    oc = o - mu
    var = jnp.mean(oc * oc, axis=-1, keepdims=True)
    return oc * lax.rsqrt(var + GN_EPS) * gs.astype(F32)


def _ret_layer_kernel(x_ref, xn_ref, g_ref, wq_ref, wk_ref, wv_ref, wg_ref, cos_ref, sin_ref, wo_ref,
                      qs_ref, ks_ref, vs_ref, gss_ref, s0_ref,
                      y_ref, sp_ref, gated_s_ref, ss_ref,
                      h_scr, q_scr, k_scr, v_scr, gs_scr, kdt_scr, gated_ref, o_scr, sp_bf, ss_bf,
                      *, chunk, nb):
    @pl.when(pl.program_id(1) == 0)
    def _():
        sp_ref[...] = jnp.zeros_like(sp_ref)
        sp_bf[...] = jnp.zeros_like(sp_bf)

    @pl.when(jnp.logical_and(pl.program_id(0) == 0, pl.program_id(1) == 0))
    def _():
        h_scr[...] = _rms(x_ref[...], g_ref[...]).astype(BF16)

    heads = range(RET_HEADS)
    _ret_projections(h_scr[...], wq_ref, wk_ref, wv_ref, wg_ref, cos_ref, sin_ref,
                     q_scr, k_scr, v_scr, gs_scr)
    prompt = _RetUnit(q_scr[...], k_scr[...], v_scr[...], gs_scr[...],
                      sp_ref.at[0], sp_ref.at[0], sp_bf, sp_bf, chunk)
    for hd in heads:
        kdt_scr[hd] = _ret_decayed_keys(prompt, hd).T.astype(BF16)

    pad = SAMPLE_KEY_ROWS - DEC_SEQ
    kpad = jnp.zeros((pad, ks_ref.shape[1]), F32)
    vpad = jnp.zeros((pad, vs_ref.shape[1]), F32)
    samples = []
    for b in range(nb):
        rows = slice(b * DEC_SEQ, (b + 1) * DEC_SEQ)
        samples.append(_RetUnit(
            qs_ref[rows, :].astype(BF16),
            jnp.concatenate([ks_ref[rows, :], kpad], axis=0).astype(BF16),
            jnp.concatenate([vs_ref[rows, :], vpad], axis=0).astype(BF16),
            gss_ref[rows, :], s0_ref.at[b], ss_ref.at[b], ss_bf.at[b], None, DEC_SEQ))
    scores_s = [[_ret_scores(u, hd) for hd in heads] for u in samples]
    for u in samples:
        for hd in heads:
            _ret_update_state(u, hd)
    for b in range(nb):
        for hd in heads:
            ss_bf[b, hd] = s0_ref[b, hd].astype(BF16)

    scores_p = [_ret_scores(prompt, hd) for hd in heads]
    o_s = [[_ret_output(u, hd, sc[hd]) for hd in heads] for u, sc in zip(samples, scores_s)]
    for hd in heads:
        o_scr[hd] = _ret_output(prompt, hd, scores_p[hd])
    for hd in heads:
        _ret_update_state(prompt, hd, kdt_scr[hd])

    for hd in heads:
        cols = slice(hd * RET_DV, (hd + 1) * RET_DV)
        for r0 in range(0, chunk, NORM_ROWS):
            rows = slice(r0, r0 + NORM_ROWS)
            gated_ref[rows, cols] = _ret_gate(o_scr[hd, rows, :], gs_scr[rows, cols]).astype(BF16)
        for b, u in enumerate(samples):
            rows = slice(b * DEC_SEQ, (b + 1) * DEC_SEQ)
            gated_s_ref[rows, cols] = _ret_gate(o_s[b][hd], u.gsc[:, cols])
    y_ref[...] = x_ref[...] + jnp.dot(gated_ref[...], wo_ref[...], preferred_element_type=F32)
    h_scr[...] = _rms(xn_ref[...], g_ref[...]).astype(BF16)


def _ret_layer(x, gamma, wq, wk, wv, wg, wo, layer, cos, sin, qs, ks, vs, gss, s0, batch, chunk):
    n = x.shape[0]
    steps = n // batch // chunk
    assert cos.shape[0] == steps * chunk
    nb = s0.shape[0] // (batch * steps)
    assert nb * batch * steps == s0.shape[0]
    nqk = RET_HEADS * RET_DK
    nv = RET_HEADS * RET_DV
    ts = nb * DEC_SEQ
    row = lambda b, c: (b * steps + c, 0)
    next_row = lambda b, c: (jnp.minimum(b * steps + c + 1, batch * steps - 1), 0)
    pos = lambda b, c: (c, 0)
    state_s = pl.BlockSpec((nb, RET_HEADS, RET_DK, RET_DV), lambda b, c: (b * steps + c, 0, 0, 0))
    return pl.pallas_call(
        functools.partial(_ret_layer_kernel, chunk=chunk, nb=nb),
        grid=(batch, steps),
        in_specs=[
            pl.BlockSpec((chunk, D_MODEL), row),
            pl.BlockSpec((chunk, D_MODEL), next_row),
            _resident((1, D_MODEL)),
            _layer_resident(wq.shape, layer),
            _layer_resident(wk.shape, layer),
            _layer_resident(wv.shape, layer),
            _layer_resident(wg.shape, layer),
            pl.BlockSpec((chunk, LANES), pos),
            pl.BlockSpec((chunk, LANES), pos),
            _layer_resident(wo.shape, layer),
            pl.BlockSpec((ts, nqk), row),
            pl.BlockSpec((ts, nqk), row),
            pl.BlockSpec((ts, nv), row),
            pl.BlockSpec((ts, nv), row),
            state_s,
        ],
        out_specs=[
            pl.BlockSpec((chunk, D_MODEL), row),
            pl.BlockSpec((1, RET_HEADS, RET_DK, RET_DV), lambda b, c: (b, 0, 0, 0)),
            pl.BlockSpec((ts, nv), row),
            state_s,
        ],
        out_shape=[
            jax.ShapeDtypeStruct((n, D_MODEL), F32),
            jax.ShapeDtypeStruct((batch, RET_HEADS, RET_DK, RET_DV), F32),
            jax.ShapeDtypeStruct((s0.shape[0] * DEC_SEQ, nv), F32),
            jax.ShapeDtypeStruct(s0.shape, F32),
        ],
        scratch_shapes=[
            pltpu.VMEM((chunk, D_MODEL), BF16),
            pltpu.VMEM((chunk, nqk), BF16),
            pltpu.VMEM((chunk, nqk), BF16),
            pltpu.VMEM((chunk, nv), BF16),
            pltpu.VMEM((chunk, nv), BF16),
            pltpu.VMEM((RET_HEADS, RET_DK, chunk), BF16),
            pltpu.VMEM((chunk, nv), BF16),
            pltpu.VMEM((RET_HEADS, chunk, RET_DV), F32),
            pltpu.VMEM((RET_HEADS, RET_DK, RET_DV), BF16),
            pltpu.VMEM((nb, RET_HEADS, RET_DK, RET_DV), BF16),
        ],
        compiler_params=_params(2),
        name="ret_layer",
    )(x, x, gamma, wq, wk, wv, wg, cos, sin, wo, qs, ks, vs, gss, s0)


def _ffn_block(x, g_ref, w1_ref, w3_ref, w2_ref, gf_ref, final_norm):
    h = _rms(x, g_ref[...]).astype(BF16)
    a = jnp.dot(h, w1_ref[...], preferred_element_type=F32)
    b = jnp.dot(h, w3_ref[...], preferred_element_type=F32)
    act = (_silu(a) * b).astype(BF16)
    y = x + jnp.dot(act, w2_ref[...], preferred_element_type=F32)
    return _rms(y, gf_ref[...]) if final_norm else y


def _ffn_kernel(x_ref, g_ref, w1_ref, w3_ref, w2_ref, gf_ref, y_ref, *, final_norm):
    y_ref[...] = _ffn_block(x_ref[...], g_ref, w1_ref, w3_ref, w2_ref, gf_ref, final_norm)


def _mixer_out_ffn_kernel(a_ref, wl_ref, bl_ref, res_ref, g_ref, w1_ref, w3_ref, w2_ref, gf_ref, y_ref,
                          *, final_norm):
    x = res_ref[...] + jnp.dot(a_ref[...].astype(BF16), wl_ref[...],
                               preferred_element_type=F32) + bl_ref[...]
    y_ref[...] = _ffn_block(x, g_ref, w1_ref, w3_ref, w2_ref, gf_ref, final_norm)


def _mixer_out_ffn(a, wl, bl, res, gamma, w1, w3, w2, layer, gamma_final, tm, final_norm):
    n, kdim = a.shape
    row = lambda i: (i, 0)
    return pl.pallas_call(
        functools.partial(_mixer_out_ffn_kernel, final_norm=final_norm),
        grid=(n // tm,),
        in_specs=[
            pl.BlockSpec((tm, kdim), row),
            _resident(wl.shape),
            _resident(bl.shape),
            pl.BlockSpec((tm, D_MODEL), row),
            _resident((1, D_MODEL)),
            _layer_resident(w1.shape, layer),
            _layer_resident(w3.shape, layer),
            _layer_resident(w2.shape, layer),
            _resident((1, D_MODEL)),
        ],
        out_specs=pl.BlockSpec((tm, D_MODEL), row),
        out_shape=jax.ShapeDtypeStruct((n, D_MODEL), F32),
        compiler_params=_params(1),
        name="mixer_out_ffn",
    )(a, wl, bl, res, gamma, w1, w3, w2, gamma_final)


def _ffn(x, gamma, w1, w3, w2, layer, gamma_final, tm, final_norm):
    n = x.shape[0]
    row = lambda i: (i, 0)
    return pl.pallas_call(
        functools.partial(_ffn_kernel, final_norm=final_norm),
        grid=(n // tm,),
        in_specs=[
            pl.BlockSpec((tm, D_MODEL), row),
            _resident((1, D_MODEL)),
            _layer_resident(w1.shape, layer),
            _layer_resident(w3.shape, layer),
            _layer_resident(w2.shape, layer),
            _resident((1, D_MODEL)),
        ],
        out_specs=pl.BlockSpec((tm, D_MODEL), row),
        out_shape=jax.ShapeDtypeStruct((n, D_MODEL), F32),
        compiler_params=_params(1),
        name="ffn",
    )(x, gamma, w1, w3, w2, gamma_final)


def _swa_proj_kernel(x_ref, g_ref, wq_ref, bq_ref, wkv_ref, bkv_ref, q_ref, kv_ref):
    h = _rms(x_ref[...], g_ref[...]).astype(BF16)
    q = jnp.dot(h, wq_ref[...], preferred_element_type=F32) + bq_ref[...]
    q_ref[...] = (q * (SWA_HEAD_DIM ** -0.5)).astype(q_ref.dtype)
    kv_ref[...] = jnp.dot(h, wkv_ref[...], preferred_element_type=F32) + bkv_ref[...]


def _swa_proj(x, gamma, wq, bq, wkv, bkv, tm, q_dtype):
    n = x.shape[0]
    nq = SWA_HEADS * SWA_HEAD_DIM
    nkv = 2 * SWA_KV_HEADS * SWA_HEAD_DIM
    row = lambda i: (i, 0)
    return pl.pallas_call(
        _swa_proj_kernel,
        grid=(n // tm,),
        in_specs=[
            pl.BlockSpec((tm, D_MODEL), row),
            _resident((1, D_MODEL)),
            _resident(wq.shape),
            _resident(bq.shape),
            _resident(wkv.shape),
            _resident(bkv.shape),
        ],
        out_specs=[pl.BlockSpec((tm, nq), row), pl.BlockSpec((tm, nkv), row)],
        out_shape=[
            jax.ShapeDtypeStruct((n, nq), q_dtype),
            jax.ShapeDtypeStruct((n, nkv), F32),
        ],
        compiler_params=_params(1),
        name="swa_proj",
    )(x, gamma, wq, bq, wkv, bkv)


def _swa_attend(q, k2, v2, prev_valid, sinks_ref):
    r = q.shape[0]
    hd = SWA_HEAD_DIM
    lane_lo = lax.broadcasted_iota(jnp.int32, (r, LANES), 1) < hd
    zero = jnp.zeros((r, LANES), q.dtype)
    pieces = []
    sink_pieces = []
    for j in range(SWA_GROUP):
        slab = q[:, j * LANES:(j + 1) * LANES]
        pieces.append(jnp.where(lane_lo, slab, zero))
        pieces.append(jnp.where(lane_lo, zero, slab))
        sink_pieces.append(jnp.full((r, 1), sinks_ref[j], F32))
        sink_pieces.append(jnp.full((r, 1), sinks_ref[SWA_GROUP + j], F32))
    qs = jnp.concatenate(pieces, axis=0).astype(BF16)
    sink = jnp.concatenate(sink_pieces, axis=0)
    rows = SWA_HEADS * r

    s2 = lax.dot_general(qs, k2, (((1,), (1,)), ((), ())), preferred_element_type=F32)
    qi = lax.broadcasted_iota(jnp.int32, (rows, WINDOW), 0) & (r - 1)
    kj = lax.broadcasted_iota(jnp.int32, (rows, WINDOW), 1)
    cur = kj <= qi
    prev_cap = jnp.where(prev_valid, jnp.float32(3.0e38), jnp.float32(NEG_INF))
    s = jnp.where(cur, s2[:, WINDOW:], jnp.minimum(s2[:, :WINDOW], prev_cap))
    m = jnp.maximum(jnp.max(s, axis=-1, keepdims=True), sink)
    p = jnp.exp(s - m)
    den = jnp.sum(p, axis=-1, keepdims=True) + jnp.exp(sink - m)
    p2 = jnp.concatenate([jnp.where(cur, 0.0, p), jnp.where(cur, p, 0.0)], axis=1).astype(BF16)
    o = jnp.dot(p2, v2, preferred_element_type=F32) / den
    outs = [jnp.where(lane_lo, o[(2 * j) * r:(2 * j + 1) * r], o[(2 * j + 1) * r:(2 * j + 2) * r])
            for j in range(SWA_GROUP)]
    return jnp.concatenate(outs, axis=1)


def _swa_prompt_kernel(sinks_ref, q_ref, kvp_ref, kvc_ref, x_ref, wo_ref, bo_ref, y_ref, o_ref,
                       *, n_sub):
    nk = SWA_KV_HEADS * SWA_HEAD_DIM
    r = WINDOW
    lane_lo = lax.broadcasted_iota(jnp.int32, (r, LANES), 1) < SWA_HEAD_DIM
    qi = lax.broadcasted_iota(jnp.int32, (r, WINDOW), 0)
    kj = lax.broadcasted_iota(jnp.int32, (r, WINDOW), 1)
    cur_mask = kj <= qi
    ones = jnp.ones((2 * WINDOW, LANES), BF16)
    nt = (((1,), (1,)), ((), ()))
    for c in range(n_sub):
        rows = slice(c * WINDOW, (c + 1) * WINDOW)
        prev = kvp_ref[...] if c == 0 else kvc_ref[(c - 1) * WINDOW:c * WINDOW, :]
        cur = kvc_ref[rows, :]
        k2 = jnp.concatenate([prev[:, :nk], cur[:, :nk]], axis=0).astype(BF16)
        v2 = jnp.concatenate([prev[:, nk:], cur[:, nk:]], axis=0).astype(BF16)
        v2_ones = jnp.concatenate([v2, ones], axis=1)
        prev_valid = (pl.program_id(1) > 0) if c == 0 else True
        prev_cap = jnp.where(prev_valid, jnp.float32(3.0e38), jnp.float32(NEG_INF))
        for j in range(SWA_GROUP):
            slab = q_ref[rows, j * LANES:(j + 1) * LANES]
            zero = jnp.zeros_like(slab)
            num = []
            den = []
            for side in range(2):
                qm = jnp.where(lane_lo, slab, zero) if side == 0 else jnp.where(lane_lo, zero, slab)
                s2 = lax.dot_general(qm, k2, nt, preferred_element_type=F32)
                s = jnp.where(cur_mask, s2[:, WINDOW:], jnp.minimum(s2[:, :WINDOW], prev_cap))
                sink = sinks_ref[j + side * SWA_GROUP]
                m = jnp.maximum(jnp.max(s, axis=-1, keepdims=True), sink)
                p = jnp.exp(s - m)
                p2 = jnp.concatenate([jnp.where(cur_mask, 0.0, p), jnp.where(cur_mask, p, 0.0)],
                                     axis=1).astype(BF16)
                o2 = jnp.dot(p2, v2_ones, preferred_element_type=F32)
                num.append(o2[:, :LANES])
                den.append(o2[:, LANES:] + jnp.exp(sink - m))
            o_ref[rows, j * LANES:(j + 1) * LANES] = (
                jnp.where(lane_lo, num[0], num[1]) / jnp.where(lane_lo, den[0], den[1])).astype(BF16)
    y = jnp.dot(o_ref[...], wo_ref[...], preferred_element_type=F32)
    y_ref[...] = x_ref[...] + y + bo_ref[...]


def _swa_prompt(q, kv, x, wo, bo, sinks, batch, n_sub):
    n = x.shape[0]
    tm = n_sub * WINDOW
    steps = n // batch // tm
    nq = SWA_HEADS * SWA_HEAD_DIM
    nkv = kv.shape[1]
    row = lambda b, i: (b * steps + i, 0)
    prev = lambda b, i: ((b * steps + i) * n_sub - jnp.minimum(i, 1), 0)
    return pl.pallas_call(
        functools.partial(_swa_prompt_kernel, n_sub=n_sub),
        grid=(batch, steps),
        in_specs=[
            pl.BlockSpec(memory_space=pltpu.SMEM),
            pl.BlockSpec((tm, nq), row),
            pl.BlockSpec((WINDOW, nkv), prev),
            pl.BlockSpec((tm, nkv), row),
            pl.BlockSpec((tm, D_MODEL), row),
            _resident(wo.shape),
            _resident(bo.shape),
        ],
        out_specs=pl.BlockSpec((tm, D_MODEL), row),
        out_shape=jax.ShapeDtypeStruct((n, D_MODEL), F32),
        scratch_shapes=[pltpu.VMEM((tm, nq), BF16)],
        compiler_params=_params(2),
        name="swa_prompt",
    )(sinks, q, kv, kv, x, wo, bo)


def _swa_sample_kernel(sinks_ref, q_ref, kvn_ref, ck_ref, cv_ref, o_ref, nk_ref, nv_ref, *, nb):
    nk = SWA_KV_HEADS * SWA_HEAD_DIM
    t = DEC_SEQ
    zpad = jnp.zeros((WINDOW - t, nk), F32)
    for b in range(nb):
        rows = slice(b * t, (b + 1) * t)
        k_new = kvn_ref[rows, :nk]
        v_new = kvn_ref[rows, nk:]
        ck = ck_ref[b]
        cv = cv_ref[b]
        k2 = jnp.concatenate([ck, k_new, zpad], axis=0).astype(BF16)
        v2 = jnp.concatenate([cv, v_new, zpad], axis=0).astype(BF16)
        o_ref[rows, :] = _swa_attend(q_ref[rows, :], k2, v2, True, sinks_ref)
        nk_ref[b, :WINDOW - t, :] = ck[t:, :]
        nk_ref[b, WINDOW - t:, :] = k_new
        nv_ref[b, :WINDOW - t, :] = cv[t:, :]
        nv_ref[b, WINDOW - t:, :] = v_new


def _swa_sample(q, kv_new, cache_k, cache_v, sinks, nb):
    batch = cache_k.shape[0]
    nq = SWA_HEADS * SWA_HEAD_DIM
    nk = SWA_KV_HEADS * SWA_HEAD_DIM
    row = lambda b: (b, 0)
    cache = pl.BlockSpec((nb, WINDOW, nk), lambda b: (b, 0, 0))
    return pl.pallas_call(
        functools.partial(_swa_sample_kernel, nb=nb),
        grid=(batch // nb,),
        in_specs=[
            pl.BlockSpec(memory_space=pltpu.SMEM),
            pl.BlockSpec((nb * DEC_SEQ, nq), row),
            pl.BlockSpec((nb * DEC_SEQ, 2 * nk), row),
            cache,
            cache,
        ],
        out_specs=[pl.BlockSpec((nb * DEC_SEQ, nq), row), cache, cache],
        out_shape=[
            jax.ShapeDtypeStruct((batch * DEC_SEQ, nq), F32),
            jax.ShapeDtypeStruct(cache_k.shape, F32),
            jax.ShapeDtypeStruct(cache_v.shape, F32),
        ],
        compiler_params=_params(1),
        name="swa_sample",
    )(sinks, q, kv_new, cache_k, cache_v)


def _rotation_tables(positions, repeat):
    half = RET_DK // 2
    inv = 1.0 / (ROPE_BASE ** (np.arange(half, dtype=np.float64) / half))
    ang = np.asarray(positions, np.float64)[:, None] * inv[None, :]
    cos = np.tile(np.cos(ang), (repeat, 1)).astype(np.float32)
    sin = np.tile(np.sin(ang), (repeat, 1)).astype(np.float32)
    return jnp.asarray(cos), jnp.asarray(sin)


TM_PROJ = 512
TM_SAMPLE = 512
TM_SAMPLE_FFN = 256


def kernel(x_prompt, x_sample, state_ret, cache_swa_k, cache_swa_v, ret_w_q, ret_w_k, ret_w_v, ret_w_g, ret_w_o, swa_w_qkv, swa_b_qkv, swa_w_o, swa_b_o, swa_sinks, norm_mix, norm_ffn, ffn_w1, ffn_w3, ffn_w2, norm_final):
    bp, tp, d = x_prompt.shape
    bs, ts, _ = x_sample.shape
    assert (tp, d, ts) == (SEQ, D_MODEL, DEC_SEQ)
    xp = x_prompt.reshape(bp * tp, d)
    xs = x_sample.reshape(bs * ts, d)
    nk = SWA_KV_HEADS * SWA_HEAD_DIM

    cos_p, sin_p = _rotation_tables(np.arange(tp), 1)
    cos_s, sin_s = _rotation_tables(PAST_LEN + np.arange(ts), TM_SAMPLE // ts)
    gamma_final = norm_final.reshape(1, d)
    zero_bias = jnp.zeros((1, d), F32)

    ret_w = [w.astype(BF16) for w in (ret_w_q, ret_w_k, ret_w_v, ret_w_g)]
    w_ro = ret_w_o.astype(BF16)
    g_mix = norm_mix[0].reshape(1, d)
    qs, ks, vs, gs = _ret_proj(xs, g_mix, *ret_w, 0, cos_s, sin_s, TM_SAMPLE, F32)
    xp, state_p, gated_s, state_s = _ret_layer(
        xp, g_mix, *ret_w, w_ro, 0, cos_p, sin_p, qs, ks, vs, gs, state_ret[0], bp, RET_PROMPT_CHUNK)

    w1 = ffn_w1.astype(BF16)
    w3 = ffn_w3.astype(BF16)
    w2 = ffn_w2.astype(BF16)
    g_ffn = norm_ffn[0].reshape(1, d)
    xp = _ffn(xp, g_ffn, w1, w3, w2, 0, gamma_final, TM_PROJ, False)
    xs = _mixer_out_ffn(gated_s, w_ro[0], zero_bias, xs, g_ffn, w1, w3, w2, 0, gamma_final,
                        TM_SAMPLE_FFN, False)

    nq = SWA_HEADS * SWA_HEAD_DIM
    heads = (SWA_KV_HEADS, SWA_GROUP, SWA_HEAD_DIM)
    w_q = swa_w_qkv[0][:, :nq].reshape(d, *heads).transpose(0, 2, 1, 3).reshape(d, nq).astype(BF16)
    b_q = swa_b_qkv[0][:nq].reshape(heads).transpose(1, 0, 2).reshape(1, nq)
    w_kv = swa_w_qkv[0][:, nq:].astype(BF16)
    b_kv = swa_b_qkv[0][nq:].reshape(1, 2 * nk)
    w_so = swa_w_o[0].reshape(*heads, d).transpose(1, 0, 2, 3).reshape(nq, d).astype(BF16)
    b_so = swa_b_o[0].reshape(1, d)
    sinks = swa_sinks[0]
    g_mix = norm_mix[1].reshape(1, d)
    qp, kvp = _swa_proj(xp, g_mix, w_q, b_q, w_kv, b_kv, 2 * TM_PROJ, BF16)
    xp = _swa_prompt(qp, kvp, xp, w_so, b_so, sinks, bp, 8)
    qs, kvs = _swa_proj(xs, g_mix, w_q, b_q, w_kv, b_kv, TM_SAMPLE, F32)
    att_s, new_k, new_v = _swa_sample(
        qs, kvs, cache_swa_k[0].reshape(bs, WINDOW, nk), cache_swa_v[0].reshape(bs, WINDOW, nk), sinks, 8)

    g_ffn = norm_ffn[1].reshape(1, d)
    yp = _ffn(xp, g_ffn, w1, w3, w2, 1, gamma_final, TM_PROJ, True)
    ys = _mixer_out_ffn(att_s, w_so, b_so, xs, g_ffn, w1, w3, w2, 1, gamma_final, TM_SAMPLE_FFN, True)

    kv_tail = kvp.reshape(bp, tp, 2 * nk)[:, tp - WINDOW:, :]
    kv_shape = (1, bp, WINDOW, SWA_KV_HEADS, SWA_HEAD_DIM)
    cache_shape = (1, bs, WINDOW, SWA_KV_HEADS, SWA_HEAD_DIM)
    return (
        yp.reshape(bp, tp, d),
        ys.reshape(bs, ts, d),
        state_p[None],
        state_s[None],
        kv_tail[..., :nk].reshape(kv_shape),
        kv_tail[..., nk:].reshape(kv_shape),
        new_k.reshape(cache_shape),
        new_v.reshape(cache_shape),
    )
```

```python
import functools
import math
from typing import NamedTuple

import numpy as np
import jax
import jax.numpy as jnp
from jax import lax
from jax.experimental import pallas as pl
from jax.experimental.pallas import tpu as pltpu

F32 = jnp.float32
BF16 = jnp.bfloat16

D_MODEL = 1024
SEQ = 2048
DEC_SEQ = 8
PAST_LEN = 16384

RET_HEADS = 4
RET_DK = 256
RET_DV = 512
RET_CHUNK = 128
RET_PROMPT_CHUNK = 256
NORM_ROWS = 64
SAMPLE_KEY_ROWS = 16
ROPE_BASE = 10000.0

SWA_HEADS = 16
SWA_KV_HEADS = 2
SWA_HEAD_DIM = 64
SWA_GROUP = SWA_HEADS // SWA_KV_HEADS
WINDOW = 128

D_FF = 2816
RMS_EPS = 1e-6
GN_EPS = 1e-6
NEG_INF = -1e30

LANES = 128
VMEM_LIMIT = 56 * 1024 * 1024


def _params(n_axes):
    return pltpu.CompilerParams(
        dimension_semantics=("arbitrary",) * n_axes, vmem_limit_bytes=VMEM_LIMIT)


def _rms(x, g):
    ms = jnp.mean(x * x, axis=-1, keepdims=True)
    return x * lax.rsqrt(ms + RMS_EPS) * g


def _silu(x):
    return x * jax.nn.sigmoid(x)


def _resident(shape):
    zeros = (0,) * len(shape)
    return pl.BlockSpec(shape, lambda *_: zeros, pipeline_mode=pl.Buffered(1))


def _layer_resident(stacked_shape, layer):
    return pl.BlockSpec((None,) + tuple(stacked_shape[1:]), lambda *_: (layer, 0, 0),
                        pipeline_mode=pl.Buffered(1))


def _ret_proj_kernel(x_ref, g_ref, wq_ref, wk_ref, wv_ref, wg_ref, cos_ref, sin_ref,
                     q_ref, k_ref, v_ref, gs_ref):
    h = _rms(x_ref[...], g_ref[...]).astype(BF16)
    _ret_projections(h, wq_ref, wk_ref, wv_ref, wg_ref, cos_ref, sin_ref, q_ref, k_ref, v_ref, gs_ref)


def _ret_projections(h, wq_ref, wk_ref, wv_ref, wg_ref, cos_ref, sin_ref, q_ref, k_ref, v_ref, gs_ref):
    cos = cos_ref[...]
    sin = sin_ref[...]
    half = RET_DK // 2

    def rotated(w_ref, out_ref, scale):
        y = jnp.dot(h, w_ref[...], preferred_element_type=F32)
        for hd in range(RET_HEADS):
            lo = hd * RET_DK
            a = y[:, lo:lo + half]
            b = y[:, lo + half:lo + RET_DK]
            out_ref[:, lo:lo + half] = ((a * cos - b * sin) * scale).astype(out_ref.dtype)
            out_ref[:, lo + half:lo + RET_DK] = ((a * sin + b * cos) * scale).astype(out_ref.dtype)

    g = jnp.dot(h, wg_ref[...], preferred_element_type=F32)
    gs_ref[...] = _silu(g).astype(gs_ref.dtype)
    v_ref[...] = jnp.dot(h, wv_ref[...], preferred_element_type=F32).astype(v_ref.dtype)
    rotated(wq_ref, q_ref, 1.0)
    rotated(wk_ref, k_ref, RET_DK ** -0.5)


def _ret_proj(x, gamma, wq, wk, wv, wg, layer, cos, sin, tm, out_dtype):
    n = x.shape[0]
    nqk = RET_HEADS * RET_DK
    nv = RET_HEADS * RET_DV
    n_pos_blocks = cos.shape[0] // tm
    row = lambda i: (i, 0)
    return pl.pallas_call(
        _ret_proj_kernel,
        grid=(n // tm,),
        in_specs=[
            pl.BlockSpec((tm, D_MODEL), row),
            _resident((1, D_MODEL)),
            _layer_resident(wq.shape, layer),
            _layer_resident(wk.shape, layer),
            _layer_resident(wv.shape, layer),
            _layer_resident(wg.shape, layer),
            pl.BlockSpec((tm, LANES), lambda i: (i % n_pos_blocks, 0)),
            pl.BlockSpec((tm, LANES), lambda i: (i % n_pos_blocks, 0)),
        ],
        out_specs=[
            pl.BlockSpec((tm, nqk), row),
            pl.BlockSpec((tm, nqk), row),
            pl.BlockSpec((tm, nv), row),
            pl.BlockSpec((tm, nv), row),
        ],
        out_shape=[
            jax.ShapeDtypeStruct((n, nqk), out_dtype),
            jax.ShapeDtypeStruct((n, nqk), out_dtype),
            jax.ShapeDtypeStruct((n, nv), out_dtype),
            jax.ShapeDtypeStruct((n, nv), out_dtype),
        ],
        compiler_params=_params(1),
        name="ret_proj",
    )(x, gamma, wq, wk, wv, wg, cos, sin)


class _RetUnit(NamedTuple):
    qc: jax.Array
    kc: jax.Array
    vc: jax.Array
    gsc: jax.Array
    s_in: object
    s_out: object
    s_bf_in: object
    s_bf_out: object
    chunk: int


def _ret_head(u, hd):
    return (u.qc[:, hd * RET_DK:(hd + 1) * RET_DK], u.kc[:, hd * RET_DK:(hd + 1) * RET_DK],
            u.vc[:, hd * RET_DV:(hd + 1) * RET_DV])


def _ret_scores(u, hd):
    qh, kh, _ = _ret_head(u, hd)
    return lax.dot_general(qh, kh, (((1,), (1,)), ((), ())), preferred_element_type=F32)


def _ret_output(u, hd, scores):
    qh, _, vh = _ret_head(u, hd)
    cq, ck = scores.shape
    log_gamma = math.log1p(-(2.0 ** -(5.0 + hd)))
    ii = lax.broadcasted_iota(jnp.int32, (cq, ck), 0)
    jj = lax.broadcasted_iota(jnp.int32, (cq, ck), 1)
    diff = (ii - jj).astype(F32)
    intra = jnp.where(diff >= 0, jnp.exp(log_gamma * jnp.maximum(diff, 0.0)), 0.0)
    irow = lax.broadcasted_iota(jnp.int32, (cq, 1), 0).astype(F32)
    q_dec = (qh.astype(F32) * jnp.exp(log_gamma * (irow + 1.0))).astype(BF16)
    lhs = jnp.concatenate([q_dec, (scores * intra).astype(BF16)], axis=1)
    rhs = jnp.concatenate([u.s_bf_in[hd], vh], axis=0)
    return jnp.dot(lhs, rhs, preferred_element_type=F32)


def _ret_decayed_keys(u, hd):
    _, kh, _ = _ret_head(u, hd)
    log_gamma = math.log1p(-(2.0 ** -(5.0 + hd)))
    jrow = lax.broadcasted_iota(jnp.int32, (kh.shape[0], 1), 0).astype(F32)
    return kh.astype(F32) * jnp.exp(log_gamma * (u.chunk - 1.0 - jrow))


def _ret_update_state(u, hd, kd_t=None):
    _, _, vh = _ret_head(u, hd)
    log_gamma = math.log1p(-(2.0 ** -(5.0 + hd)))
    if kd_t is None:
        update = lax.dot_general(_ret_decayed_keys(u, hd).astype(BF16), vh, (((0,), (0,)), ((), ())),
                                 preferred_element_type=F32)
    else:
        update = jnp.dot(kd_t, vh, preferred_element_type=F32)
    s_new = u.s_in[hd] * math.exp(log_gamma * u.chunk) + update
    u.s_out[hd] = s_new
    if u.s_bf_out is not None:
        u.s_bf_out[hd] = s_new.astype(BF16)


def _ret_gate(o, gs):
    mu = jnp.mean(o, axis=-1, keepdims=True)
    oc = o - mu
    var = jnp.mean(oc * oc, axis=-1, keepdims=True)
    return oc * lax.rsqrt(var + GN_EPS) * gs.astype(F32)


def _ret_layer_kernel(x_ref, xn_ref, g_ref, wq_ref, wk_ref, wv_ref, wg_ref, cos_ref, sin_ref, wo_ref,
                      qs_ref, ks_ref, vs_ref, gss_ref, s0_ref,
                      y_ref, sp_ref, gated_s_ref, ss_ref,
                      h_scr, q_scr, k_scr, v_scr, gs_scr, kdt_scr, gated_ref, o_scr, sp_bf, ss_bf,
                      *, chunk, nb):
    @pl.when(pl.program_id(1) == 0)
    def _():
        sp_ref[...] = jnp.zeros_like(sp_ref)
        sp_bf[...] = jnp.zeros_like(sp_bf)

    @pl.when(jnp.logical_and(pl.program_id(0) == 0, pl.program_id(1) == 0))
    def _():
        h_scr[...] = _rms(x_ref[...], g_ref[...]).astype(BF16)

    heads = range(RET_HEADS)
    _ret_projections(h_scr[...], wq_ref, wk_ref, wv_ref, wg_ref, cos_ref, sin_ref,
                     q_scr, k_scr, v_scr, gs_scr)
    prompt = _RetUnit(q_scr[...], k_scr[...], v_scr[...], gs_scr[...],
                      sp_ref.at[0], sp_ref.at[0], sp_bf, sp_bf, chunk)
    for hd in heads:
        kdt_scr[hd] = _ret_decayed_keys(prompt, hd).T.astype(BF16)

    pad = SAMPLE_KEY_ROWS - DEC_SEQ
    kpad = jnp.zeros((pad, ks_ref.shape[1]), F32)
    vpad = jnp.zeros((pad, vs_ref.shape[1]), F32)
    samples = []
    for b in range(nb):
        rows = slice(b * DEC_SEQ, (b + 1) * DEC_SEQ)
        samples.append(_RetUnit(
            qs_ref[rows, :].astype(BF16),
            jnp.concatenate([ks_ref[rows, :], kpad], axis=0).astype(BF16),
            jnp.concatenate([vs_ref[rows, :], vpad], axis=0).astype(BF16),
            gss_ref[rows, :], s0_ref.at[b], ss_ref.at[b], ss_bf.at[b], None, DEC_SEQ))
    scores_s = [[_ret_scores(u, hd) for hd in heads] for u in samples]
    for u in samples:
        for hd in heads:
            _ret_update_state(u, hd)
    for b in range(nb):
        for hd in heads:
            ss_bf[b, hd] = s0_ref[b, hd].astype(BF16)

    scores_p = [_ret_scores(prompt, hd) for hd in heads]
    o_s = [[_ret_output(u, hd, sc[hd]) for hd in heads] for u, sc in zip(samples, scores_s)]
    for hd in heads:
        o_scr[hd] = _ret_output(prompt, hd, scores_p[hd])
    for hd in heads:
        _ret_update_state(prompt, hd, kdt_scr[hd])

    for hd in heads:
        cols = slice(hd * RET_DV, (hd + 1) * RET_DV)
        for r0 in range(0, chunk, NORM_ROWS):
            rows = slice(r0, r0 + NORM_ROWS)
            gated_ref[rows, cols] = _ret_gate(o_scr[hd, rows, :], gs_scr[rows, cols]).astype(BF16)
        for b, u in enumerate(samples):
            rows = slice(b * DEC_SEQ, (b + 1) * DEC_SEQ)
            gated_s_ref[rows, cols] = _ret_gate(o_s[b][hd], u.gsc[:, cols])
    y_ref[...] = x_ref[...] + jnp.dot(gated_ref[...], wo_ref[...], preferred_element_type=F32)
    h_scr[...] = _rms(xn_ref[...], g_ref[...]).astype(BF16)


def _ret_layer(x, gamma, wq, wk, wv, wg, wo, layer, cos, sin, qs, ks, vs, gss, s0, batch, chunk):
    n = x.shape[0]
    steps = n // batch // chunk
    assert cos.shape[0] == steps * chunk
    nb = s0.shape[0] // (batch * steps)
    assert nb * batch * steps == s0.shape[0]
    nqk = RET_HEADS * RET_DK
    nv = RET_HEADS * RET_DV
    ts = nb * DEC_SEQ
    row = lambda b, c: (b * steps + c, 0)
    next_row = lambda b, c: (jnp.minimum(b * steps + c + 1, batch * steps - 1), 0)
    pos = lambda b, c: (c, 0)
    state_s = pl.BlockSpec((nb, RET_HEADS, RET_DK, RET_DV), lambda b, c: (b * steps + c, 0, 0, 0))
    return pl.pallas_call(
        functools.partial(_ret_layer_kernel, chunk=chunk, nb=nb),
        grid=(batch, steps),
        in_specs=[
            pl.BlockSpec((chunk, D_MODEL), row),
            pl.BlockSpec((chunk, D_MODEL), next_row),
            _resident((1, D_MODEL)),
            _layer_resident(wq.shape, layer),
            _layer_resident(wk.shape, layer),
            _layer_resident(wv.shape, layer),
            _layer_resident(wg.shape, layer),
            pl.BlockSpec((chunk, LANES), pos),
            pl.BlockSpec((chunk, LANES), pos),
            _layer_resident(wo.shape, layer),
            pl.BlockSpec((ts, nqk), row),
            pl.BlockSpec((ts, nqk), row),
            pl.BlockSpec((ts, nv), row),
            pl.BlockSpec((ts, nv), row),
            state_s,
        ],
        out_specs=[
            pl.BlockSpec((chunk, D_MODEL), row),
            pl.BlockSpec((1, RET_HEADS, RET_DK, RET_DV), lambda b, c: (b, 0, 0, 0)),
            pl.BlockSpec((ts, nv), row),
            state_s,
        ],
        out_shape=[
            jax.ShapeDtypeStruct((n, D_MODEL), F32),
            jax.ShapeDtypeStruct((batch, RET_HEADS, RET_DK, RET_DV), F32),
            jax.ShapeDtypeStruct((s0.shape[0] * DEC_SEQ, nv), F32),
            jax.ShapeDtypeStruct(s0.shape, F32),
        ],
        scratch_shapes=[
            pltpu.VMEM((chunk, D_MODEL), BF16),
            pltpu.VMEM((chunk, nqk), BF16),
            pltpu.VMEM((chunk, nqk), BF16),
            pltpu.VMEM((chunk, nv), BF16),
            pltpu.VMEM((chunk, nv), BF16),
            pltpu.VMEM((RET_HEADS, RET_DK, chunk), BF16),
            pltpu.VMEM((chunk, nv), BF16),
            pltpu.VMEM((RET_HEADS, chunk, RET_DV), F32),
            pltpu.VMEM((RET_HEADS, RET_DK, RET_DV), BF16),
            pltpu.VMEM((nb, RET_HEADS, RET_DK, RET_DV), BF16),
        ],
        compiler_params=_params(2),
        name="ret_layer",
    )(x, x, gamma, wq, wk, wv, wg, cos, sin, wo, qs, ks, vs, gss, s0)


def _ffn_block(x, g_ref, w1_ref, w3_ref, w2_ref, gf_ref, final_norm):
    h = _rms(x, g_ref[...]).astype(BF16)
    y = x
    for lo, hi in zip(FFN_SPLITS[:-1], FFN_SPLITS[1:]):
        cols = slice(lo, hi)
        a = jnp.dot(h, w1_ref[:, cols], preferred_element_type=F32)
        b = jnp.dot(h, w3_ref[:, cols], preferred_element_type=F32)
        act = (_silu(a) * b).astype(BF16)
        y = y + jnp.dot(act, w2_ref[cols, :], preferred_element_type=F32)
    return _rms(y, gf_ref[...]) if final_norm else y


def _ffn_kernel(x_ref, g_ref, w1_ref, w3_ref, w2_ref, gf_ref, y_ref, *, final_norm):
    y_ref[...] = _ffn_block(x_ref[...], g_ref, w1_ref, w3_ref, w2_ref, gf_ref, final_norm)


def _mixer_out_ffn_kernel(a_ref, wl_ref, bl_ref, res_ref, g_ref, w1_ref, w3_ref, w2_ref, gf_ref, y_ref,
                          *, final_norm):
    x = res_ref[...] + jnp.dot(a_ref[...].astype(BF16), wl_ref[...],
                               preferred_element_type=F32) + bl_ref[...]
    y_ref[...] = _ffn_block(x, g_ref, w1_ref, w3_ref, w2_ref, gf_ref, final_norm)


def _mixer_out_ffn(a, wl, bl, res, gamma, w1, w3, w2, layer, gamma_final, tm, final_norm):
    n, kdim = a.shape
    row = lambda i: (i, 0)
    return pl.pallas_call(
        functools.partial(_mixer_out_ffn_kernel, final_norm=final_norm),
        grid=(n // tm,),
        in_specs=[
            pl.BlockSpec((tm, kdim), row),
            _resident(wl.shape),
            _resident(bl.shape),
            pl.BlockSpec((tm, D_MODEL), row),
            _resident((1, D_MODEL)),
            _layer_resident(w1.shape, layer),
            _layer_resident(w3.shape, layer),
            _layer_resident(w2.shape, layer),
            _resident((1, D_MODEL)),
        ],
        out_specs=pl.BlockSpec((tm, D_MODEL), row),
        out_shape=jax.ShapeDtypeStruct((n, D_MODEL), F32),
        compiler_params=_params(1),
        name="mixer_out_ffn",
    )(a, wl, bl, res, gamma, w1, w3, w2, gamma_final)


def _ffn(x, gamma, w1, w3, w2, layer, gamma_final, tm, final_norm):
    n = x.shape[0]
    row = lambda i: (i, 0)
    return pl.pallas_call(
        functools.partial(_ffn_kernel, final_norm=final_norm),
        grid=(n // tm,),
        in_specs=[
            pl.BlockSpec((tm, D_MODEL), row),
            _resident((1, D_MODEL)),
            _layer_resident(w1.shape, layer),
            _layer_resident(w3.shape, layer),
            _layer_resident(w2.shape, layer),
            _resident((1, D_MODEL)),
        ],
        out_specs=pl.BlockSpec((tm, D_MODEL), row),
        out_shape=jax.ShapeDtypeStruct((n, D_MODEL), F32),
        compiler_params=_params(1),
        name="ffn",
    )(x, gamma, w1, w3, w2, gamma_final)


def _swa_proj_kernel(x_ref, g_ref, wq_ref, bq_ref, wkv_ref, bkv_ref, q_ref, kv_ref):
    h = _rms(x_ref[...], g_ref[...]).astype(BF16)
    q = jnp.dot(h, wq_ref[...], preferred_element_type=F32) + bq_ref[...]
    q_ref[...] = (q * (SWA_HEAD_DIM ** -0.5)).astype(q_ref.dtype)
    kv_ref[...] = jnp.dot(h, wkv_ref[...], preferred_element_type=F32) + bkv_ref[...]


def _swa_proj(x, gamma, wq, bq, wkv, bkv, tm, q_dtype):
    n = x.shape[0]
    nq = SWA_HEADS * SWA_HEAD_DIM
    nkv = 2 * SWA_KV_HEADS * SWA_HEAD_DIM
    row = lambda i: (i, 0)
    return pl.pallas_call(
        _swa_proj_kernel,
        grid=(n // tm,),
        in_specs=[
            pl.BlockSpec((tm, D_MODEL), row),
            _resident((1, D_MODEL)),
            _resident(wq.shape),
            _resident(bq.shape),
            _resident(wkv.shape),
            _resident(bkv.shape),
        ],
        out_specs=[pl.BlockSpec((tm, nq), row), pl.BlockSpec((tm, nkv), row)],
        out_shape=[
            jax.ShapeDtypeStruct((n, nq), q_dtype),
            jax.ShapeDtypeStruct((n, nkv), F32),
        ],
        compiler_params=_params(1),
        name="swa_proj",
    )(x, gamma, wq, bq, wkv, bkv)


def _swa_attend(q, k2, v2, prev_valid, sinks_ref):
    r = q.shape[0]
    hd = SWA_HEAD_DIM
    lane_lo = lax.broadcasted_iota(jnp.int32, (r, LANES), 1) < hd
    zero = jnp.zeros((r, LANES), q.dtype)
    pieces = []
    sink_pieces = []
    for j in range(SWA_GROUP):
        slab = q[:, j * LANES:(j + 1) * LANES]
        pieces.append(jnp.where(lane_lo, slab, zero))
        pieces.append(jnp.where(lane_lo, zero, slab))
        sink_pieces.append(jnp.full((r, 1), sinks_ref[j], F32))
        sink_pieces.append(jnp.full((r, 1), sinks_ref[SWA_GROUP + j], F32))
    qs = jnp.concatenate(pieces, axis=0).astype(BF16)
    sink = jnp.concatenate(sink_pieces, axis=0)
    rows = SWA_HEADS * r

    s2 = lax.dot_general(qs, k2, (((1,), (1,)), ((), ())), preferred_element_type=F32)
    qi = lax.broadcasted_iota(jnp.int32, (rows, WINDOW), 0) & (r - 1)
    kj = lax.broadcasted_iota(jnp.int32, (rows, WINDOW), 1)
    cur = kj <= qi
    prev_cap = jnp.where(prev_valid, jnp.float32(3.0e38), jnp.float32(NEG_INF))
    s = jnp.where(cur, s2[:, WINDOW:], jnp.minimum(s2[:, :WINDOW], prev_cap))
    m = jnp.maximum(jnp.max(s, axis=-1, keepdims=True), sink)
    p = jnp.exp(s - m)
    den = jnp.sum(p, axis=-1, keepdims=True) + jnp.exp(sink - m)
    p2 = jnp.concatenate([jnp.where(cur, 0.0, p), jnp.where(cur, p, 0.0)], axis=1).astype(BF16)
    o = jnp.dot(p2, v2, preferred_element_type=F32) / den
    outs = [jnp.where(lane_lo, o[(2 * j) * r:(2 * j + 1) * r], o[(2 * j + 1) * r:(2 * j + 2) * r])
            for j in range(SWA_GROUP)]
    return jnp.concatenate(outs, axis=1)


def _swa_prompt_kernel(sinks_ref, q_ref, kvp_ref, kvc_ref, x_ref, wo_ref, bo_ref, y_ref, o_ref,
                       *, n_sub):
    nk = SWA_KV_HEADS * SWA_HEAD_DIM
    r = WINDOW
    lane_lo = lax.broadcasted_iota(jnp.int32, (r, LANES), 1) < SWA_HEAD_DIM
    qi = lax.broadcasted_iota(jnp.int32, (r, WINDOW), 0)
    kj = lax.broadcasted_iota(jnp.int32, (r, WINDOW), 1)
    cur_mask = kj <= qi
    ones = jnp.ones((2 * WINDOW, LANES), BF16)
    nt = (((1,), (1,)), ((), ()))
    for c in range(n_sub):
        rows = slice(c * WINDOW, (c + 1) * WINDOW)
        prev = kvp_ref[...] if c == 0 else kvc_ref[(c - 1) * WINDOW:c * WINDOW, :]
        cur = kvc_ref[rows, :]
        k2 = jnp.concatenate([prev[:, :nk], cur[:, :nk]], axis=0).astype(BF16)
        v2 = jnp.concatenate([prev[:, nk:], cur[:, nk:]], axis=0).astype(BF16)
        v2_ones = jnp.concatenate([v2, ones], axis=1)
        prev_valid = (pl.program_id(1) > 0) if c == 0 else True
        prev_cap = jnp.where(prev_valid, jnp.float32(3.0e38), jnp.float32(NEG_INF))
        for j in range(SWA_GROUP):
            slab = q_ref[rows, j * LANES:(j + 1) * LANES]
            zero = jnp.zeros_like(slab)
            num = []
            den = []
            for side in range(2):
                qm = jnp.where(lane_lo, slab, zero) if side == 0 else jnp.where(lane_lo, zero, slab)
                s2 = lax.dot_general(qm, k2, nt, preferred_element_type=F32)
                s = jnp.where(cur_mask, s2[:, WINDOW:], jnp.minimum(s2[:, :WINDOW], prev_cap))
                sink = sinks_ref[j + side * SWA_GROUP]
                m = jnp.maximum(jnp.max(s, axis=-1, keepdims=True), sink)
                p = jnp.exp(s - m)
                p2 = jnp.concatenate([jnp.where(cur_mask, 0.0, p), jnp.where(cur_mask, p, 0.0)],
                                     axis=1).astype(BF16)
                o2 = jnp.dot(p2, v2_ones, preferred_element_type=F32)
                num.append(o2[:, :LANES])
                den.append(o2[:, LANES:] + jnp.exp(sink - m))
            o_ref[rows, j * LANES:(j + 1) * LANES] = (
                jnp.where(lane_lo, num[0], num[1]) / jnp.where(lane_lo, den[0], den[1])).astype(BF16)
    y = jnp.dot(o_ref[...], wo_ref[...], preferred_element_type=F32)
    y_ref[...] = x_ref[...] + y + bo_ref[...]


def _swa_prompt(q, kv, x, wo, bo, sinks, batch, n_sub):
    n = x.shape[0]
    tm = n_sub * WINDOW
    steps = n // batch // tm
    nq = SWA_HEADS * SWA_HEAD_DIM
    nkv = kv.shape[1]
    row = lambda b, i: (b * steps + i, 0)
    prev = lambda b, i: ((b * steps + i) * n_sub - jnp.minimum(i, 1), 0)
    return pl.pallas_call(
        functools.partial(_swa_prompt_kernel, n_sub=n_sub),
        grid=(batch, steps),
        in_specs=[
            pl.BlockSpec(memory_space=pltpu.SMEM),
            pl.BlockSpec((tm, nq), row),
            pl.BlockSpec((WINDOW, nkv), prev),
            pl.BlockSpec((tm, nkv), row),
            pl.BlockSpec((tm, D_MODEL), row),
            _resident(wo.shape),
            _resident(bo.shape),
        ],
        out_specs=pl.BlockSpec((tm, D_MODEL), row),
        out_shape=jax.ShapeDtypeStruct((n, D_MODEL), F32),
        scratch_shapes=[pltpu.VMEM((tm, nq), BF16)],
        compiler_params=_params(2),
        name="swa_prompt",
    )(sinks, q, kv, kv, x, wo, bo)


def _swa_sample_kernel(sinks_ref, q_ref, kvn_ref, ck_ref, cv_ref, o_ref, nk_ref, nv_ref, *, nb):
    nk = SWA_KV_HEADS * SWA_HEAD_DIM
    t = DEC_SEQ
    zpad = jnp.zeros((WINDOW - t, nk), F32)
    for b in range(nb):
        rows = slice(b * t, (b + 1) * t)
        k_new = kvn_ref[rows, :nk]
        v_new = kvn_ref[rows, nk:]
        ck = ck_ref[b]
        cv = cv_ref[b]
        k2 = jnp.concatenate([ck, k_new, zpad], axis=0).astype(BF16)
        v2 = jnp.concatenate([cv, v_new, zpad], axis=0).astype(BF16)
        o_ref[rows, :] = _swa_attend(q_ref[rows, :], k2, v2, True, sinks_ref)
        nk_ref[b, :WINDOW - t, :] = ck[t:, :]
        nk_ref[b, WINDOW - t:, :] = k_new
        nv_ref[b, :WINDOW - t, :] = cv[t:, :]
        nv_ref[b, WINDOW - t:, :] = v_new


def _swa_sample(q, kv_new, cache_k, cache_v, sinks, nb):
    batch = cache_k.shape[0]
    nq = SWA_HEADS * SWA_HEAD_DIM
    nk = SWA_KV_HEADS * SWA_HEAD_DIM
    row = lambda b: (b, 0)
    cache = pl.BlockSpec((nb, WINDOW, nk), lambda b: (b, 0, 0))
    return pl.pallas_call(
        functools.partial(_swa_sample_kernel, nb=nb),
        grid=(batch // nb,),
        in_specs=[
            pl.BlockSpec(memory_space=pltpu.SMEM),
            pl.BlockSpec((nb * DEC_SEQ, nq), row),
            pl.BlockSpec((nb * DEC_SEQ, 2 * nk), row),
            cache,
            cache,
        ],
        out_specs=[pl.BlockSpec((nb * DEC_SEQ, nq), row), cache, cache],
        out_shape=[
            jax.ShapeDtypeStruct((batch * DEC_SEQ, nq), F32),
            jax.ShapeDtypeStruct(cache_k.shape, F32),
            jax.ShapeDtypeStruct(cache_v.shape, F32),
        ],
        compiler_params=_params(1),
        name="swa_sample",
    )(sinks, q, kv_new, cache_k, cache_v)


def _rotation_tables(positions, repeat):
    half = RET_DK // 2
    inv = 1.0 / (ROPE_BASE ** (np.arange(half, dtype=np.float64) / half))
    ang = np.asarray(positions, np.float64)[:, None] * inv[None, :]
    cos = np.tile(np.cos(ang), (repeat, 1)).astype(np.float32)
    sin = np.tile(np.sin(ang), (repeat, 1)).astype(np.float32)
    return jnp.asarray(cos), jnp.asarray(sin)


TM_PROJ = 512
TM_SAMPLE = 512
TM_FFN = 1024
FFN_SPLITS = (0, 1536, D_FF)
TM_SAMPLE_FFN = 512


def kernel(x_prompt, x_sample, state_ret, cache_swa_k, cache_swa_v, ret_w_q, ret_w_k, ret_w_v, ret_w_g, ret_w_o, swa_w_qkv, swa_b_qkv, swa_w_o, swa_b_o, swa_sinks, norm_mix, norm_ffn, ffn_w1, ffn_w3, ffn_w2, norm_final):
    bp, tp, d = x_prompt.shape
    bs, ts, _ = x_sample.shape
    assert (tp, d, ts) == (SEQ, D_MODEL, DEC_SEQ)
    xp = x_prompt.reshape(bp * tp, d)
    xs = x_sample.reshape(bs * ts, d)
    nk = SWA_KV_HEADS * SWA_HEAD_DIM

    cos_p, sin_p = _rotation_tables(np.arange(tp), 1)
    cos_s, sin_s = _rotation_tables(PAST_LEN + np.arange(ts), TM_SAMPLE // ts)
    gamma_final = norm_final.reshape(1, d)
    zero_bias = jnp.zeros((1, d), F32)

    ret_w = [w.astype(BF16) for w in (ret_w_q, ret_w_k, ret_w_v, ret_w_g)]
    w_ro = ret_w_o.astype(BF16)
    g_mix = norm_mix[0].reshape(1, d)
    qs, ks, vs, gs = _ret_proj(xs, g_mix, *ret_w, 0, cos_s, sin_s, TM_SAMPLE, F32)
    xp, state_p, gated_s, state_s = _ret_layer(
        xp, g_mix, *ret_w, w_ro, 0, cos_p, sin_p, qs, ks, vs, gs, state_ret[0], bp, RET_PROMPT_CHUNK)

    w1 = ffn_w1.astype(BF16)
    w3 = ffn_w3.astype(BF16)
    w2 = ffn_w2.astype(BF16)
    g_ffn = norm_ffn[0].reshape(1, d)
    xp = _ffn(xp, g_ffn, w1, w3, w2, 0, gamma_final, TM_FFN, False)
    xs = _mixer_out_ffn(gated_s, w_ro[0], zero_bias, xs, g_ffn, w1, w3, w2, 0, gamma_final,
                        TM_SAMPLE_FFN, False)

    nq = SWA_HEADS * SWA_HEAD_DIM
    heads = (SWA_KV_HEADS, SWA_GROUP, SWA_HEAD_DIM)
    w_q = swa_w_qkv[0][:, :nq].reshape(d, *heads).transpose(0, 2, 1, 3).reshape(d, nq).astype(BF16)
    b_q = swa_b_qkv[0][:nq].reshape(heads).transpose(1, 0, 2).reshape(1, nq)
    w_kv = swa_w_qkv[0][:, nq:].astype(BF16)
    b_kv = swa_b_qkv[0][nq:].reshape(1, 2 * nk)
    w_so = swa_w_o[0].reshape(*heads, d).transpose(1, 0, 2, 3).reshape(nq, d).astype(BF16)
    b_so = swa_b_o[0].reshape(1, d)
    sinks = swa_sinks[0]
    g_mix = norm_mix[1].reshape(1, d)
    qp, kvp = _swa_proj(xp, g_mix, w_q, b_q, w_kv, b_kv, 2 * TM_PROJ, BF16)
    xp = _swa_prompt(qp, kvp, xp, w_so, b_so, sinks, bp, 8)
    qs, kvs = _swa_proj(xs, g_mix, w_q, b_q, w_kv, b_kv, TM_SAMPLE, F32)
    att_s, new_k, new_v = _swa_sample(
        qs, kvs, cache_swa_k[0].reshape(bs, WINDOW, nk), cache_swa_v[0].reshape(bs, WINDOW, nk), sinks, 8)

    g_ffn = norm_ffn[1].reshape(1, d)
    yp = _ffn(xp, g_ffn, w1, w3, w2, 1, gamma_final, TM_FFN, True)
    ys = _mixer_out_ffn(att_s, w_so, b_so, xs, g_ffn, w1, w3, w2, 1, gamma_final, TM_SAMPLE_FFN, True)

    kv_tail = kvp.reshape(bp, tp, 2 * nk)[:, tp - WINDOW:, :]
    kv_shape = (1, bp, WINDOW, SWA_KV_HEADS, SWA_HEAD_DIM)
    cache_shape = (1, bs, WINDOW, SWA_KV_HEADS, SWA_HEAD_DIM)
    return (
        yp.reshape(bp, tp, d),
        ys.reshape(bs, ts, d),
        state_p[None],
        state_s[None],
        kv_tail[..., :nk].reshape(kv_shape),
        kv_tail[..., nk:].reshape(kv_shape),
        new_k.reshape(cache_shape),
        new_v.reshape(cache_shape),
    )
```

```python
import functools
import math
from typing import NamedTuple

import numpy as np
import jax
import jax.numpy as jnp
from jax import lax
from jax.experimental import pallas as pl
from jax.experimental.pallas import tpu as pltpu

F32 = jnp.float32
BF16 = jnp.bfloat16

D_MODEL = 1024
SEQ = 2048
DEC_SEQ = 8
PAST_LEN = 16384

RET_HEADS = 4
RET_DK = 256
RET_DV = 512
RET_CHUNK = 128
RET_PROMPT_CHUNK = 256
NORM_ROWS = 64
SAMPLE_KEY_ROWS = 16
ROPE_BASE = 10000.0

SWA_HEADS = 16
SWA_KV_HEADS = 2
SWA_HEAD_DIM = 64
SWA_GROUP = SWA_HEADS // SWA_KV_HEADS
WINDOW = 128

D_FF = 2816
RMS_EPS = 1e-6
GN_EPS = 1e-6
NEG_INF = -1e30

LANES = 128
BF16_SUBLANES = 16
VMEM_LIMIT = 56 * 1024 * 1024


def _params(n_axes):
    return pltpu.CompilerParams(
        dimension_semantics=("arbitrary",) * n_axes, vmem_limit_bytes=VMEM_LIMIT)


def _rms(x, g):
    ms = jnp.mean(x * x, axis=-1, keepdims=True)
    return x * lax.rsqrt(ms + RMS_EPS) * g


def _silu(x):
    return x * jax.nn.sigmoid(x)


def _resident(shape):
    zeros = (0,) * len(shape)
    return pl.BlockSpec(shape, lambda *_: zeros, pipeline_mode=pl.Buffered(1))


def _layer_resident(stacked_shape, layer):
    return pl.BlockSpec((None,) + tuple(stacked_shape[1:]), lambda *_: (layer, 0, 0),
                        pipeline_mode=pl.Buffered(1))


def _ret_proj_kernel(x_ref, g_ref, wq_ref, wk_ref, wv_ref, wg_ref, cos_ref, sin_ref,
                     q_ref, k_ref, v_ref, gs_ref):
    h = _rms(x_ref[...], g_ref[...]).astype(BF16)
    _ret_projections(h, wq_ref, wk_ref, wv_ref, wg_ref, cos_ref, sin_ref, q_ref, k_ref, v_ref, gs_ref)


def _ret_projections(h, wq_ref, wk_ref, wv_ref, wg_ref, cos_ref, sin_ref, q_ref, k_ref, v_ref, gs_ref):
    cos = cos_ref[...]
    sin = sin_ref[...]
    half = RET_DK // 2

    def rotated(w_ref, out_ref, scale):
        y = jnp.dot(h, w_ref[...], preferred_element_type=F32)
        for hd in range(RET_HEADS):
            lo = hd * RET_DK
            a = y[:, lo:lo + half]
            b = y[:, lo + half:lo + RET_DK]
            out_ref[:, lo:lo + half] = ((a * cos - b * sin) * scale).astype(out_ref.dtype)
            out_ref[:, lo + half:lo + RET_DK] = ((a * sin + b * cos) * scale).astype(out_ref.dtype)

    g = jnp.dot(h, wg_ref[...], preferred_element_type=F32)
    gs_ref[...] = _silu(g).astype(gs_ref.dtype)
    v_ref[...] = jnp.dot(h, wv_ref[...], preferred_element_type=F32).astype(v_ref.dtype)
    rotated(wq_ref, q_ref, 1.0)
    rotated(wk_ref, k_ref, RET_DK ** -0.5)


def _ret_proj(x, gamma, wq, wk, wv, wg, layer, cos, sin, tm, out_dtype):
    n = x.shape[0]
    nqk = RET_HEADS * RET_DK
    nv = RET_HEADS * RET_DV
    n_pos_blocks = cos.shape[0] // tm
    row = lambda i: (i, 0)
    return pl.pallas_call(
        _ret_proj_kernel,
        grid=(n // tm,),
        in_specs=[
            pl.BlockSpec((tm, D_MODEL), row),
            _resident((1, D_MODEL)),
            _layer_resident(wq.shape, layer),
            _layer_resident(wk.shape, layer),
            _layer_resident(wv.shape, layer),
            _layer_resident(wg.shape, layer),
            pl.BlockSpec((tm, LANES), lambda i: (i % n_pos_blocks, 0)),
            pl.BlockSpec((tm, LANES), lambda i: (i % n_pos_blocks, 0)),
        ],
        out_specs=[
            pl.BlockSpec((tm, nqk), row),
            pl.BlockSpec((tm, nqk), row),
            pl.BlockSpec((tm, nv), row),
            pl.BlockSpec((tm, nv), row),
        ],
        out_shape=[
            jax.ShapeDtypeStruct((n, nqk), out_dtype),
            jax.ShapeDtypeStruct((n, nqk), out_dtype),
            jax.ShapeDtypeStruct((n, nv), out_dtype),
            jax.ShapeDtypeStruct((n, nv), out_dtype),
        ],
        compiler_params=_params(1),
        name="ret_proj",
    )(x, gamma, wq, wk, wv, wg, cos, sin)


class _RetUnit(NamedTuple):
    qc: jax.Array
    kc: jax.Array
    vc: jax.Array
    gsc: jax.Array
    s_in: object
    s_out: object
    s_bf_in: object
    s_bf_out: object
    chunk: int


def _ret_head(u, hd):
    return (u.qc[:, hd * RET_DK:(hd + 1) * RET_DK], u.kc[:, hd * RET_DK:(hd + 1) * RET_DK],
            u.vc[:, hd * RET_DV:(hd + 1) * RET_DV])


def _ret_scores(u, hd):
    qh, kh, _ = _ret_head(u, hd)
    return lax.dot_general(qh, kh, (((1,), (1,)), ((), ())), preferred_element_type=F32)


def _ret_output(u, hd, scores):
    qh, _, vh = _ret_head(u, hd)
    cq, ck = scores.shape
    log_gamma = math.log1p(-(2.0 ** -(5.0 + hd)))
    ii = lax.broadcasted_iota(jnp.int32, (cq, ck), 0)
    jj = lax.broadcasted_iota(jnp.int32, (cq, ck), 1)
    diff = (ii - jj).astype(F32)
    intra = jnp.where(diff >= 0, jnp.exp(log_gamma * jnp.maximum(diff, 0.0)), 0.0)
    irow = lax.broadcasted_iota(jnp.int32, (cq, 1), 0).astype(F32)
    q_dec = (qh.astype(F32) * jnp.exp(log_gamma * (irow + 1.0))).astype(BF16)
    lhs = jnp.concatenate([q_dec, (scores * intra).astype(BF16)], axis=1)
    rhs = jnp.concatenate([u.s_bf_in[hd], vh], axis=0)
    return jnp.dot(lhs, rhs, preferred_element_type=F32)


def _ret_decayed_keys(u, hd):
    _, kh, _ = _ret_head(u, hd)
    log_gamma = math.log1p(-(2.0 ** -(5.0 + hd)))
    jrow = lax.broadcasted_iota(jnp.int32, (kh.shape[0], 1), 0).astype(F32)
    return kh.astype(F32) * jnp.exp(log_gamma * (u.chunk - 1.0 - jrow))


def _ret_update_state(u, hd, kd_t=None):
    _, _, vh = _ret_head(u, hd)
    log_gamma = math.log1p(-(2.0 ** -(5.0 + hd)))
    if kd_t is None:
        update = lax.dot_general(_ret_decayed_keys(u, hd).astype(BF16), vh, (((0,), (0,)), ((), ())),
                                 preferred_element_type=F32)
    else:
        update = jnp.dot(kd_t, vh, preferred_element_type=F32)
    s_new = u.s_in[hd] * math.exp(log_gamma * u.chunk) + update
    u.s_out[hd] = s_new
    if u.s_bf_out is not None:
        u.s_bf_out[hd] = s_new.astype(BF16)


def _ret_gate(o, gs):
    mu = jnp.mean(o, axis=-1, keepdims=True)
    oc = o - mu
    var = jnp.mean(oc * oc, axis=-1, keepdims=True)
    return oc * lax.rsqrt(var + GN_EPS) * gs.astype(F32)


def _ret_layer_kernel(*refs, chunk, nb, n_cast):
    n_in, n_out = 15, 4
    (x_ref, xn_ref, g_ref, wq_ref, wk_ref, wv_ref, wg_ref, cos_ref, sin_ref, wo_ref,
     qs_ref, ks_ref, vs_ref, gss_ref, s0_ref) = refs[:n_in]
    cast_in = refs[n_in:n_in + n_cast]
    y_ref, sp_ref, gated_s_ref, ss_ref = refs[n_in + n_cast:n_in + n_cast + n_out]
    cast_out = refs[n_in + n_cast + n_out:n_in + 2 * n_cast + n_out]
    (h_scr, q_scr, k_scr, v_scr, gs_scr, kdt_scr, gated_ref, o_scr, sp_bf,
     ss_bf) = refs[n_in + 2 * n_cast + n_out:]
    _ret_layer_body(x_ref, xn_ref, g_ref, wq_ref, wk_ref, wv_ref, wg_ref, cos_ref, sin_ref, wo_ref,
                    qs_ref, ks_ref, vs_ref, gss_ref, s0_ref, y_ref, sp_ref, gated_s_ref, ss_ref,
                    h_scr, q_scr, k_scr, v_scr, gs_scr, kdt_scr, gated_ref, o_scr, sp_bf, ss_bf,
                    chunk=chunk, nb=nb)
    for src_ref, dst_ref in zip(cast_in, cast_out):
        dst_ref[...] = src_ref[...].astype(dst_ref.dtype)


def _ret_layer_body(x_ref, xn_ref, g_ref, wq_ref, wk_ref, wv_ref, wg_ref, cos_ref, sin_ref, wo_ref,
                    qs_ref, ks_ref, vs_ref, gss_ref, s0_ref,
                    y_ref, sp_ref, gated_s_ref, ss_ref,
                    h_scr, q_scr, k_scr, v_scr, gs_scr, kdt_scr, gated_ref, o_scr, sp_bf, ss_bf,
                    *, chunk, nb):
    @pl.when(pl.program_id(1) == 0)
    def _():
        sp_ref[...] = jnp.zeros_like(sp_ref)
        sp_bf[...] = jnp.zeros_like(sp_bf)

    @pl.when(jnp.logical_and(pl.program_id(0) == 0, pl.program_id(1) == 0))
    def _():
        h_scr[...] = _rms(x_ref[...], g_ref[...]).astype(BF16)

    heads = range(RET_HEADS)
    _ret_projections(h_scr[...], wq_ref, wk_ref, wv_ref, wg_ref, cos_ref, sin_ref,
                     q_scr, k_scr, v_scr, gs_scr)
    prompt = _RetUnit(q_scr[...], k_scr[...], v_scr[...], gs_scr[...],
                      sp_ref.at[0], sp_ref.at[0], sp_bf, sp_bf, chunk)
    for hd in heads:
        kdt_scr[hd] = _ret_decayed_keys(prompt, hd).T.astype(BF16)

    pad = SAMPLE_KEY_ROWS - DEC_SEQ
    kpad = jnp.zeros((pad, ks_ref.shape[1]), F32)
    vpad = jnp.zeros((pad, vs_ref.shape[1]), F32)
    samples = []
    for b in range(nb):
        rows = slice(b * DEC_SEQ, (b + 1) * DEC_SEQ)
        samples.append(_RetUnit(
            qs_ref[rows, :].astype(BF16),
            jnp.concatenate([ks_ref[rows, :], kpad], axis=0).astype(BF16),
            jnp.concatenate([vs_ref[rows, :], vpad], axis=0).astype(BF16),
            gss_ref[rows, :], s0_ref.at[b], ss_ref.at[b], ss_bf.at[b], None, DEC_SEQ))
    scores_s = [[_ret_scores(u, hd) for hd in heads] for u in samples]
    for u in samples:
        for hd in heads:
            _ret_update_state(u, hd)
    for b in range(nb):
        for hd in heads:
            ss_bf[b, hd] = s0_ref[b, hd].astype(BF16)

    scores_p = [_ret_scores(prompt, hd) for hd in heads]
    o_s = [[_ret_output(u, hd, sc[hd]) for hd in heads] for u, sc in zip(samples, scores_s)]
    for hd in heads:
        o_scr[hd] = _ret_output(prompt, hd, scores_p[hd])
    for hd in heads:
        _ret_update_state(prompt, hd, kdt_scr[hd])

    for hd in heads:
        cols = slice(hd * RET_DV, (hd + 1) * RET_DV)
        for r0 in range(0, chunk, NORM_ROWS):
            rows = slice(r0, r0 + NORM_ROWS)
            gated_ref[rows, cols] = _ret_gate(o_scr[hd, rows, :], gs_scr[rows, cols]).astype(BF16)
        for b, u in enumerate(samples):
            rows = slice(b * DEC_SEQ, (b + 1) * DEC_SEQ)
            gated_s_ref[rows, cols] = _ret_gate(o_s[b][hd], u.gsc[:, cols])
    y_ref[...] = x_ref[...] + jnp.dot(gated_ref[...], wo_ref[...], preferred_element_type=F32)
    h_scr[...] = _rms(xn_ref[...], g_ref[...]).astype(BF16)


def _cast_slab_spec(rows, cols, n_steps):
    for span in (1, 2, 4):
        slab, rem = divmod(rows * span, n_steps)
        if rem == 0 and slab % BF16_SUBLANES == 0:
            return slab, span
    raise ValueError(f"no slab tiling for {rows} rows over {n_steps} steps")


def _ret_layer(x, gamma, wq, wk, wv, wg, wo, layer, cos, sin, qs, ks, vs, gss, s0, batch, chunk, to_cast):
    n = x.shape[0]
    steps = n // batch // chunk
    cast_specs = []
    for w in to_cast:
        slab, span = _cast_slab_spec(w.shape[0], w.shape[1], batch * steps)
        cast_specs.append(pl.BlockSpec(
            (slab, w.shape[1]), lambda b, c, span=span: ((b * steps + c) // span, 0)))
    assert cos.shape[0] == steps * chunk
    nb = s0.shape[0] // (batch * steps)
    assert nb * batch * steps == s0.shape[0]
    nqk = RET_HEADS * RET_DK
    nv = RET_HEADS * RET_DV
    ts = nb * DEC_SEQ
    row = lambda b, c: (b * steps + c, 0)
    next_row = lambda b, c: (jnp.minimum(b * steps + c + 1, batch * steps - 1), 0)
    pos = lambda b, c: (c, 0)
    state_s = pl.BlockSpec((nb, RET_HEADS, RET_DK, RET_DV), lambda b, c: (b * steps + c, 0, 0, 0))
    return pl.pallas_call(
        functools.partial(_ret_layer_kernel, chunk=chunk, nb=nb, n_cast=len(to_cast)),
        grid=(batch, steps),
        in_specs=[
            pl.BlockSpec((chunk, D_MODEL), row),
            pl.BlockSpec((chunk, D_MODEL), next_row),
            _resident((1, D_MODEL)),
            _layer_resident(wq.shape, layer),
            _layer_resident(wk.shape, layer),
            _layer_resident(wv.shape, layer),
            _layer_resident(wg.shape, layer),
            pl.BlockSpec((chunk, LANES), pos),
            pl.BlockSpec((chunk, LANES), pos),
            _layer_resident(wo.shape, layer),
            pl.BlockSpec((ts, nqk), row),
            pl.BlockSpec((ts, nqk), row),
            pl.BlockSpec((ts, nv), row),
            pl.BlockSpec((ts, nv), row),
            state_s,
        ] + cast_specs,
        out_specs=[
            pl.BlockSpec((chunk, D_MODEL), row),
            pl.BlockSpec((1, RET_HEADS, RET_DK, RET_DV), lambda b, c: (b, 0, 0, 0)),
            pl.BlockSpec((ts, nv), row),
            state_s,
        ] + cast_specs,
        out_shape=[
            jax.ShapeDtypeStruct((n, D_MODEL), F32),
            jax.ShapeDtypeStruct((batch, RET_HEADS, RET_DK, RET_DV), F32),
            jax.ShapeDtypeStruct((s0.shape[0] * DEC_SEQ, nv), F32),
            jax.ShapeDtypeStruct(s0.shape, F32),
        ] + [jax.ShapeDtypeStruct(w.shape, BF16) for w in to_cast],
        scratch_shapes=[
            pltpu.VMEM((chunk, D_MODEL), BF16),
            pltpu.VMEM((chunk, nqk), BF16),
            pltpu.VMEM((chunk, nqk), BF16),
            pltpu.VMEM((chunk, nv), BF16),
            pltpu.VMEM((chunk, nv), BF16),
            pltpu.VMEM((RET_HEADS, RET_DK, chunk), BF16),
            pltpu.VMEM((chunk, nv), BF16),
            pltpu.VMEM((RET_HEADS, chunk, RET_DV), F32),
            pltpu.VMEM((RET_HEADS, RET_DK, RET_DV), BF16),
            pltpu.VMEM((nb, RET_HEADS, RET_DK, RET_DV), BF16),
        ],
        compiler_params=_params(2),
        name="ret_layer",
    )(x, x, gamma, wq, wk, wv, wg, cos, sin, wo, qs, ks, vs, gss, s0, *to_cast)


def _ffn_block(x, g_ref, w1_ref, w3_ref, w2_ref, gf_ref, final_norm):
    h = _rms(x, g_ref[...]).astype(BF16)
    y = x
    for lo, hi in zip(FFN_SPLITS[:-1], FFN_SPLITS[1:]):
        cols = slice(lo, hi)
        a = jnp.dot(h, w1_ref[:, cols], preferred_element_type=F32)
        b = jnp.dot(h, w3_ref[:, cols], preferred_element_type=F32)
        act = (_silu(a) * b).astype(BF16)
        y = y + jnp.dot(act, w2_ref[cols, :], preferred_element_type=F32)
    return _rms(y, gf_ref[...]) if final_norm else y


def _ffn_kernel(x_ref, g_ref, w1_ref, w3_ref, w2_ref, gf_ref, y_ref, *, final_norm):
    y_ref[...] = _ffn_block(x_ref[...], g_ref, w1_ref, w3_ref, w2_ref, gf_ref, final_norm)


def _mixer_out_ffn_kernel(a_ref, wl_ref, bl_ref, res_ref, g_ref, w1_ref, w3_ref, w2_ref, gf_ref, y_ref,
                          *, final_norm):
    x = res_ref[...] + jnp.dot(a_ref[...].astype(BF16), wl_ref[...],
                               preferred_element_type=F32) + bl_ref[...]
    y_ref[...] = _ffn_block(x, g_ref, w1_ref, w3_ref, w2_ref, gf_ref, final_norm)


def _mixer_out_ffn(a, wl, bl, res, gamma, w1, w3, w2, layer, gamma_final, tm, final_norm):
    n, kdim = a.shape
    row = lambda i: (i, 0)
    return pl.pallas_call(
        functools.partial(_mixer_out_ffn_kernel, final_norm=final_norm),
        grid=(n // tm,),
        in_specs=[
            pl.BlockSpec((tm, kdim), row),
            _resident(wl.shape),
            _resident(bl.shape),
            pl.BlockSpec((tm, D_MODEL), row),
            _resident((1, D_MODEL)),
            _layer_resident(w1.shape, layer),
            _layer_resident(w3.shape, layer),
            _layer_resident(w2.shape, layer),
            _resident((1, D_MODEL)),
        ],
        out_specs=pl.BlockSpec((tm, D_MODEL), row),
        out_shape=jax.ShapeDtypeStruct((n, D_MODEL), F32),
        compiler_params=_params(1),
        name="mixer_out_ffn",
    )(a, wl, bl, res, gamma, w1, w3, w2, gamma_final)


def _ffn(x, gamma, w1, w3, w2, layer, gamma_final, tm, final_norm):
    n = x.shape[0]
    row = lambda i: (i, 0)
    return pl.pallas_call(
        functools.partial(_ffn_kernel, final_norm=final_norm),
        grid=(n // tm,),
        in_specs=[
            pl.BlockSpec((tm, D_MODEL), row),
            _resident((1, D_MODEL)),
            _layer_resident(w1.shape, layer),
            _layer_resident(w3.shape, layer),
            _layer_resident(w2.shape, layer),
            _resident((1, D_MODEL)),
        ],
        out_specs=pl.BlockSpec((tm, D_MODEL), row),
        out_shape=jax.ShapeDtypeStruct((n, D_MODEL), F32),
        compiler_params=_params(1),
        name="ffn",
    )(x, gamma, w1, w3, w2, gamma_final)


def _swa_proj_kernel(x_ref, g_ref, wq_ref, bq_ref, wkv_ref, bkv_ref, q_ref, kv_ref):
    h = _rms(x_ref[...], g_ref[...]).astype(BF16)
    q = jnp.dot(h, wq_ref[...], preferred_element_type=F32) + bq_ref[...]
    q_ref[...] = (q * (SWA_HEAD_DIM ** -0.5)).astype(q_ref.dtype)
    kv_ref[...] = jnp.dot(h, wkv_ref[...], preferred_element_type=F32) + bkv_ref[...]


def _swa_proj(x, gamma, wq, bq, wkv, bkv, tm, q_dtype):
    n = x.shape[0]
    nq = SWA_HEADS * SWA_HEAD_DIM
    nkv = 2 * SWA_KV_HEADS * SWA_HEAD_DIM
    row = lambda i: (i, 0)
    return pl.pallas_call(
        _swa_proj_kernel,
        grid=(n // tm,),
        in_specs=[
            pl.BlockSpec((tm, D_MODEL), row),
            _resident((1, D_MODEL)),
            _resident(wq.shape),
            _resident(bq.shape),
            _resident(wkv.shape),
            _resident(bkv.shape),
        ],
        out_specs=[pl.BlockSpec((tm, nq), row), pl.BlockSpec((tm, nkv), row)],
        out_shape=[
            jax.ShapeDtypeStruct((n, nq), q_dtype),
            jax.ShapeDtypeStruct((n, nkv), F32),
        ],
        compiler_params=_params(1),
        name="swa_proj",
    )(x, gamma, wq, bq, wkv, bkv)


def _swa_attend(q, k2, v2, prev_valid, sinks_ref):
    r = q.shape[0]
    hd = SWA_HEAD_DIM
    lane_lo = lax.broadcasted_iota(jnp.int32, (r, LANES), 1) < hd
    zero = jnp.zeros((r, LANES), q.dtype)
    pieces = []
    sink_pieces = []
    for j in range(SWA_GROUP):
        slab = q[:, j * LANES:(j + 1) * LANES]
        pieces.append(jnp.where(lane_lo, slab, zero))
        pieces.append(jnp.where(lane_lo, zero, slab))
        sink_pieces.append(jnp.full((r, 1), sinks_ref[j], F32))
        sink_pieces.append(jnp.full((r, 1), sinks_ref[SWA_GROUP + j], F32))
    qs = jnp.concatenate(pieces, axis=0).astype(BF16)
    sink = jnp.concatenate(sink_pieces, axis=0)
    rows = SWA_HEADS * r

    s2 = lax.dot_general(qs, k2, (((1,), (1,)), ((), ())), preferred_element_type=F32)
    qi = lax.broadcasted_iota(jnp.int32, (rows, WINDOW), 0) & (r - 1)
    kj = lax.broadcasted_iota(jnp.int32, (rows, WINDOW), 1)
    cur = kj <= qi
    prev_cap = jnp.where(prev_valid, jnp.float32(3.0e38), jnp.float32(NEG_INF))
    s = jnp.where(cur, s2[:, WINDOW:], jnp.minimum(s2[:, :WINDOW], prev_cap))
    m = jnp.maximum(jnp.max(s, axis=-1, keepdims=True), sink)
    p = jnp.exp(s - m)
    den = jnp.sum(p, axis=-1, keepdims=True) + jnp.exp(sink - m)
    p2 = jnp.concatenate([jnp.where(cur, 0.0, p), jnp.where(cur, p, 0.0)], axis=1).astype(BF16)
    o = jnp.dot(p2, v2, preferred_element_type=F32) / den
    outs = [jnp.where(lane_lo, o[(2 * j) * r:(2 * j + 1) * r], o[(2 * j + 1) * r:(2 * j + 2) * r])
            for j in range(SWA_GROUP)]
    return jnp.concatenate(outs, axis=1)


def _swa_prompt_kernel(sinks_ref, q_ref, kvp_ref, kvc_ref, x_ref, wo_ref, bo_ref, y_ref, o_ref,
                       *, n_sub):
    nk = SWA_KV_HEADS * SWA_HEAD_DIM
    r = WINDOW
    lane_lo = lax.broadcasted_iota(jnp.int32, (r, LANES), 1) < SWA_HEAD_DIM
    qi = lax.broadcasted_iota(jnp.int32, (r, WINDOW), 0)
    kj = lax.broadcasted_iota(jnp.int32, (r, WINDOW), 1)
    cur_mask = kj <= qi
    ones = jnp.ones((2 * WINDOW, LANES), BF16)
    nt = (((1,), (1,)), ((), ()))
    for c in range(n_sub):
        rows = slice(c * WINDOW, (c + 1) * WINDOW)
        prev = kvp_ref[...] if c == 0 else kvc_ref[(c - 1) * WINDOW:c * WINDOW, :]
        cur = kvc_ref[rows, :]
        k2 = jnp.concatenate([prev[:, :nk], cur[:, :nk]], axis=0).astype(BF16)
        v2 = jnp.concatenate([prev[:, nk:], cur[:, nk:]], axis=0).astype(BF16)
        v2_ones = jnp.concatenate([v2, ones], axis=1)
        prev_valid = (pl.program_id(1) > 0) if c == 0 else True
        prev_cap = jnp.where(prev_valid, jnp.float32(3.0e38), jnp.float32(NEG_INF))
        for j in range(SWA_GROUP):
            slab = q_ref[rows, j * LANES:(j + 1) * LANES]
            zero = jnp.zeros_like(slab)
            num = []
            den = []
            for side in range(2):
                qm = jnp.where(lane_lo, slab, zero) if side == 0 else jnp.where(lane_lo, zero, slab)
                s2 = lax.dot_general(qm, k2, nt, preferred_element_type=F32)
                s = jnp.where(cur_mask, s2[:, WINDOW:], jnp.minimum(s2[:, :WINDOW], prev_cap))
                sink = sinks_ref[j + side * SWA_GROUP]
                m = jnp.maximum(jnp.max(s, axis=-1, keepdims=True), sink)
                p = jnp.exp(s - m)
                p2 = jnp.concatenate([jnp.where(cur_mask, 0.0, p), jnp.where(cur_mask, p, 0.0)],
                                     axis=1).astype(BF16)
                o2 = jnp.dot(p2, v2_ones, preferred_element_type=F32)
                num.append(o2[:, :LANES])
                den.append(o2[:, LANES:] + jnp.exp(sink - m))
            o_ref[rows, j * LANES:(j + 1) * LANES] = (
                jnp.where(lane_lo, num[0], num[1]) / jnp.where(lane_lo, den[0], den[1])).astype(BF16)
    y = jnp.dot(o_ref[...], wo_ref[...], preferred_element_type=F32)
    y_ref[...] = x_ref[...] + y + bo_ref[...]


def _swa_prompt(q, kv, x, wo, bo, sinks, batch, n_sub):
    n = x.shape[0]
    tm = n_sub * WINDOW
    steps = n // batch // tm
    nq = SWA_HEADS * SWA_HEAD_DIM
    nkv = kv.shape[1]
    row = lambda b, i: (b * steps + i, 0)
    prev = lambda b, i: ((b * steps + i) * n_sub - jnp.minimum(i, 1), 0)
    return pl.pallas_call(
        functools.partial(_swa_prompt_kernel, n_sub=n_sub),
        grid=(batch, steps),
        in_specs=[
            pl.BlockSpec(memory_space=pltpu.SMEM),
            pl.BlockSpec((tm, nq), row),
            pl.BlockSpec((WINDOW, nkv), prev),
            pl.BlockSpec((tm, nkv), row),
            pl.BlockSpec((tm, D_MODEL), row),
            _resident(wo.shape),
            _resident(bo.shape),
        ],
        out_specs=pl.BlockSpec((tm, D_MODEL), row),
        out_shape=jax.ShapeDtypeStruct((n, D_MODEL), F32),
        scratch_shapes=[pltpu.VMEM((tm, nq), BF16)],
        compiler_params=_params(2),
        name="swa_prompt",
    )(sinks, q, kv, kv, x, wo, bo)


def _swa_sample_kernel(sinks_ref, q_ref, kvn_ref, ck_ref, cv_ref, o_ref, nk_ref, nv_ref, *, nb):
    nk = SWA_KV_HEADS * SWA_HEAD_DIM
    t = DEC_SEQ
    zpad = jnp.zeros((WINDOW - t, nk), F32)
    for b in range(nb):
        rows = slice(b * t, (b + 1) * t)
        k_new = kvn_ref[rows, :nk]
        v_new = kvn_ref[rows, nk:]
        ck = ck_ref[b]
        cv = cv_ref[b]
        k2 = jnp.concatenate([ck, k_new, zpad], axis=0).astype(BF16)
        v2 = jnp.concatenate([cv, v_new, zpad], axis=0).astype(BF16)
        o_ref[rows, :] = _swa_attend(q_ref[rows, :], k2, v2, True, sinks_ref)
        nk_ref[b, :WINDOW - t, :] = ck[t:, :]
        nk_ref[b, WINDOW - t:, :] = k_new
        nv_ref[b, :WINDOW - t, :] = cv[t:, :]
        nv_ref[b, WINDOW - t:, :] = v_new


def _swa_sample(q, kv_new, cache_k, cache_v, sinks, nb):
    batch = cache_k.shape[0]
    nq = SWA_HEADS * SWA_HEAD_DIM
    nk = SWA_KV_HEADS * SWA_HEAD_DIM
    row = lambda b: (b, 0)
    cache = pl.BlockSpec((nb, WINDOW, nk), lambda b: (b, 0, 0))
    return pl.pallas_call(
        functools.partial(_swa_sample_kernel, nb=nb),
        grid=(batch // nb,),
        in_specs=[
            pl.BlockSpec(memory_space=pltpu.SMEM),
            pl.BlockSpec((nb * DEC_SEQ, nq), row),
            pl.BlockSpec((nb * DEC_SEQ, 2 * nk), row),
            cache,
            cache,
        ],
        out_specs=[pl.BlockSpec((nb * DEC_SEQ, nq), row), cache, cache],
        out_shape=[
            jax.ShapeDtypeStruct((batch * DEC_SEQ, nq), F32),
            jax.ShapeDtypeStruct(cache_k.shape, F32),
            jax.ShapeDtypeStruct(cache_v.shape, F32),
        ],
        compiler_params=_params(1),
        name="swa_sample",
    )(sinks, q, kv_new, cache_k, cache_v)


def _rotation_tables(positions, repeat):
    half = RET_DK // 2
    inv = 1.0 / (ROPE_BASE ** (np.arange(half, dtype=np.float64) / half))
    ang = np.asarray(positions, np.float64)[:, None] * inv[None, :]
    cos = np.tile(np.cos(ang), (repeat, 1)).astype(np.float32)
    sin = np.tile(np.sin(ang), (repeat, 1)).astype(np.float32)
    return jnp.asarray(cos), jnp.asarray(sin)


TM_PROJ = 512
TM_SAMPLE = 512
TM_FFN = 1024
FFN_SPLITS = (0, 1536, D_FF)
TM_SAMPLE_FFN = 512


def kernel(x_prompt, x_sample, state_ret, cache_swa_k, cache_swa_v, ret_w_q, ret_w_k, ret_w_v, ret_w_g, ret_w_o, swa_w_qkv, swa_b_qkv, swa_w_o, swa_b_o, swa_sinks, norm_mix, norm_ffn, ffn_w1, ffn_w3, ffn_w2, norm_final):
    bp, tp, d = x_prompt.shape
    bs, ts, _ = x_sample.shape
    assert (tp, d, ts) == (SEQ, D_MODEL, DEC_SEQ)
    xp = x_prompt.reshape(bp * tp, d)
    xs = x_sample.reshape(bs * ts, d)
    nk = SWA_KV_HEADS * SWA_HEAD_DIM

    cos_p, sin_p = _rotation_tables(np.arange(tp), 1)
    cos_s, sin_s = _rotation_tables(PAST_LEN + np.arange(ts), TM_SAMPLE // ts)
    gamma_final = norm_final.reshape(1, d)
    zero_bias = jnp.zeros((1, d), F32)

    ret_w = [w.astype(BF16) for w in (ret_w_q, ret_w_k, ret_w_v, ret_w_g)]
    w_ro = ret_w_o.astype(BF16)
    g_mix = norm_mix[0].reshape(1, d)
    qs, ks, vs, gs = _ret_proj(xs, g_mix, *ret_w, 0, cos_s, sin_s, TM_SAMPLE, F32)
    ffn_f32 = [w.reshape(-1, w.shape[-1]) for w in (ffn_w1, ffn_w3, ffn_w2)]
    xp, state_p, gated_s, state_s, w1, w3, w2 = _ret_layer(
        xp, g_mix, *ret_w, w_ro, 0, cos_p, sin_p, qs, ks, vs, gs, state_ret[0], bp, RET_PROMPT_CHUNK,
        ffn_f32)
    w1 = w1.reshape(ffn_w1.shape)
    w3 = w3.reshape(ffn_w3.shape)
    w2 = w2.reshape(ffn_w2.shape)
    g_ffn = norm_ffn[0].reshape(1, d)
    xp = _ffn(xp, g_ffn, w1, w3, w2, 0, gamma_final, TM_FFN, False)
    xs = _mixer_out_ffn(gated_s, w_ro[0], zero_bias, xs, g_ffn, w1, w3, w2, 0, gamma_final,
                        TM_SAMPLE_FFN, False)

    nq = SWA_HEADS * SWA_HEAD_DIM
    heads = (SWA_KV_HEADS, SWA_GROUP, SWA_HEAD_DIM)
    w_q = swa_w_qkv[0][:, :nq].reshape(d, *heads).transpose(0, 2, 1, 3).reshape(d, nq).astype(BF16)
    b_q = swa_b_qkv[0][:nq].reshape(heads).transpose(1, 0, 2).reshape(1, nq)
    w_kv = swa_w_qkv[0][:, nq:].astype(BF16)
    b_kv = swa_b_qkv[0][nq:].reshape(1, 2 * nk)
    w_so = swa_w_o[0].reshape(*heads, d).transpose(1, 0, 2, 3).reshape(nq, d).astype(BF16)
    b_so = swa_b_o[0].reshape(1, d)
    sinks = swa_sinks[0]
    g_mix = norm_mix[1].reshape(1, d)
    qp, kvp = _swa_proj(xp, g_mix, w_q, b_q, w_kv, b_kv, 2 * TM_PROJ, BF16)
    xp = _swa_prompt(qp, kvp, xp, w_so, b_so, sinks, bp, 8)
    qs, kvs = _swa_proj(xs, g_mix, w_q, b_q, w_kv, b_kv, TM_SAMPLE, F32)
    att_s, new_k, new_v = _swa_sample(
        qs, kvs, cache_swa_k[0].reshape(bs, WINDOW, nk), cache_swa_v[0].reshape(bs, WINDOW, nk), sinks, 8)

    g_ffn = norm_ffn[1].reshape(1, d)
    yp = _ffn(xp, g_ffn, w1, w3, w2, 1, gamma_final, TM_FFN, True)
    ys = _mixer_out_ffn(att_s, w_so, b_so, xs, g_ffn, w1, w3, w2, 1, gamma_final, TM_SAMPLE_FFN, True)

    kv_tail = kvp.reshape(bp, tp, 2 * nk)[:, tp - WINDOW:, :]
    kv_shape = (1, bp, WINDOW, SWA_KV_HEADS, SWA_HEAD_DIM)
    cache_shape = (1, bs, WINDOW, SWA_KV_HEADS, SWA_HEAD_DIM)
    return (
        yp.reshape(bp, tp, d),
        ys.reshape(bs, ts, d),
        state_p[None],
        state_s[None],
        kv_tail[..., :nk].reshape(kv_shape),
        kv_tail[..., nk:].reshape(kv_shape),
        new_k.reshape(cache_shape),
        new_v.reshape(cache_shape),
    )
```

```python
import functools
import math
from typing import NamedTuple

import numpy as np
import jax
import jax.numpy as jnp
from jax import lax
from jax.experimental import pallas as pl
from jax.experimental.pallas import tpu as pltpu

F32 = jnp.float32
BF16 = jnp.bfloat16

D_MODEL = 1024
SEQ = 2048
DEC_SEQ = 8
PAST_LEN = 16384

RET_HEADS = 4
RET_DK = 256
RET_DV = 512
RET_CHUNK = 128
RET_PROMPT_CHUNK = 256
NORM_ROWS = 64
SAMPLE_KEY_ROWS = 16
ROPE_BASE = 10000.0

SWA_HEADS = 16
SWA_KV_HEADS = 2
SWA_HEAD_DIM = 64
SWA_GROUP = SWA_HEADS // SWA_KV_HEADS
WINDOW = 128

D_FF = 2816
RMS_EPS = 1e-6
GN_EPS = 1e-6
NEG_INF = -1e30

LANES = 128
BF16_SUBLANES = 16
VMEM_LIMIT = 56 * 1024 * 1024


def _params(n_axes):
    return pltpu.CompilerParams(
        dimension_semantics=("arbitrary",) * n_axes, vmem_limit_bytes=VMEM_LIMIT)


def _rms(x, g):
    ms = jnp.mean(x * x, axis=-1, keepdims=True)
    return x * lax.rsqrt(ms + RMS_EPS) * g


def _silu(x):
    return x * jax.nn.sigmoid(x)


def _resident(shape):
    zeros = (0,) * len(shape)
    return pl.BlockSpec(shape, lambda *_: zeros, pipeline_mode=pl.Buffered(1))


def _layer_resident(stacked_shape, layer):
    return pl.BlockSpec((None,) + tuple(stacked_shape[1:]), lambda *_: (layer, 0, 0),
                        pipeline_mode=pl.Buffered(1))


def _ret_proj_kernel(x_ref, g_ref, wq_ref, wk_ref, wv_ref, wg_ref, cos_ref, sin_ref,
                     q_ref, k_ref, v_ref, gs_ref):
    h = _rms(x_ref[...], g_ref[...]).astype(BF16)
    _ret_projections(h, wq_ref, wk_ref, wv_ref, wg_ref, cos_ref, sin_ref, q_ref, k_ref, v_ref, gs_ref)


def _ret_projections(h, wq_ref, wk_ref, wv_ref, wg_ref, cos_ref, sin_ref, q_ref, k_ref, v_ref, gs_ref):
    cos = cos_ref[...]
    sin = sin_ref[...]
    half = RET_DK // 2

    def rotated(w_ref, out_ref, scale):
        y = jnp.dot(h, w_ref[...], preferred_element_type=F32)
        for hd in range(RET_HEADS):
            lo = hd * RET_DK
            a = y[:, lo:lo + half]
            b = y[:, lo + half:lo + RET_DK]
            out_ref[:, lo:lo + half] = ((a * cos - b * sin) * scale).astype(out_ref.dtype)
            out_ref[:, lo + half:lo + RET_DK] = ((a * sin + b * cos) * scale).astype(out_ref.dtype)

    g = jnp.dot(h, wg_ref[...], preferred_element_type=F32)
    gs_ref[...] = _silu(g).astype(gs_ref.dtype)
    v_ref[...] = jnp.dot(h, wv_ref[...], preferred_element_type=F32).astype(v_ref.dtype)
    rotated(wq_ref, q_ref, 1.0)
    rotated(wk_ref, k_ref, RET_DK ** -0.5)


def _ret_proj(x, gamma, wq, wk, wv, wg, layer, cos, sin, tm, out_dtype):
    n = x.shape[0]
    nqk = RET_HEADS * RET_DK
    nv = RET_HEADS * RET_DV
    n_pos_blocks = cos.shape[0] // tm
    row = lambda i: (i, 0)
    return pl.pallas_call(
        _ret_proj_kernel,
        grid=(n // tm,),
        in_specs=[
            pl.BlockSpec((tm, D_MODEL), row),
            _resident((1, D_MODEL)),
            _layer_resident(wq.shape, layer),
            _layer_resident(wk.shape, layer),
            _layer_resident(wv.shape, layer),
            _layer_resident(wg.shape, layer),
            pl.BlockSpec((tm, LANES), lambda i: (i % n_pos_blocks, 0)),
            pl.BlockSpec((tm, LANES), lambda i: (i % n_pos_blocks, 0)),
        ],
        out_specs=[
            pl.BlockSpec((tm, nqk), row),
            pl.BlockSpec((tm, nqk), row),
            pl.BlockSpec((tm, nv), row),
            pl.BlockSpec((tm, nv), row),
        ],
        out_shape=[
            jax.ShapeDtypeStruct((n, nqk), out_dtype),
            jax.ShapeDtypeStruct((n, nqk), out_dtype),
            jax.ShapeDtypeStruct((n, nv), out_dtype),
            jax.ShapeDtypeStruct((n, nv), out_dtype),
        ],
        compiler_params=_params(1),
        name="ret_proj",
    )(x, gamma, wq, wk, wv, wg, cos, sin)


class _RetUnit(NamedTuple):
    qc: jax.Array
    kc: jax.Array
    vc: jax.Array
    gsc: jax.Array
    s_in: object
    s_out: object
    s_bf_in: object
    s_bf_out: object
    chunk: int


def _ret_head(u, hd):
    return (u.qc[:, hd * RET_DK:(hd + 1) * RET_DK], u.kc[:, hd * RET_DK:(hd + 1) * RET_DK],
            u.vc[:, hd * RET_DV:(hd + 1) * RET_DV])


def _ret_scores(u, hd):
    qh, kh, _ = _ret_head(u, hd)
    return lax.dot_general(qh, kh, (((1,), (1,)), ((), ())), preferred_element_type=F32)


def _ret_output(u, hd, scores):
    qh, _, vh = _ret_head(u, hd)
    cq, ck = scores.shape
    log_gamma = math.log1p(-(2.0 ** -(5.0 + hd)))
    ii = lax.broadcasted_iota(jnp.int32, (cq, ck), 0)
    jj = lax.broadcasted_iota(jnp.int32, (cq, ck), 1)
    diff = (ii - jj).astype(F32)
    intra = jnp.where(diff >= 0, jnp.exp(log_gamma * jnp.maximum(diff, 0.0)), 0.0)
    irow = lax.broadcasted_iota(jnp.int32, (cq, 1), 0).astype(F32)
    q_dec = (qh.astype(F32) * jnp.exp(log_gamma * (irow + 1.0))).astype(BF16)
    lhs = jnp.concatenate([q_dec, (scores * intra).astype(BF16)], axis=1)
    rhs = jnp.concatenate([u.s_bf_in[hd], vh], axis=0)
    return jnp.dot(lhs, rhs, preferred_element_type=F32)


def _ret_decayed_keys(u, hd):
    _, kh, _ = _ret_head(u, hd)
    log_gamma = math.log1p(-(2.0 ** -(5.0 + hd)))
    jrow = lax.broadcasted_iota(jnp.int32, (kh.shape[0], 1), 0).astype(F32)
    return kh.astype(F32) * jnp.exp(log_gamma * (u.chunk - 1.0 - jrow))


def _ret_update_state(u, hd, kd_t=None):
    _, _, vh = _ret_head(u, hd)
    log_gamma = math.log1p(-(2.0 ** -(5.0 + hd)))
    if kd_t is None:
        update = lax.dot_general(_ret_decayed_keys(u, hd).astype(BF16), vh, (((0,), (0,)), ((), ())),
                                 preferred_element_type=F32)
    else:
        update = jnp.dot(kd_t, vh, preferred_element_type=F32)
    s_new = u.s_in[hd] * math.exp(log_gamma * u.chunk) + update
    u.s_out[hd] = s_new
    if u.s_bf_out is not None:
        u.s_bf_out[hd] = s_new.astype(BF16)


def _ret_gate(o, gs):
    mu = jnp.mean(o, axis=-1, keepdims=True)
    oc = o - mu
    var = jnp.mean(oc * oc, axis=-1, keepdims=True)
    return oc * lax.rsqrt(var + GN_EPS) * gs.astype(F32)


def _ret_layer_kernel(*refs, chunk, nb, n_cast):
    n_in, n_out = 15, 4
    (x_ref, xn_ref, g_ref, wq_ref, wk_ref, wv_ref, wg_ref, cos_ref, sin_ref, wo_ref,
     qs_ref, ks_ref, vs_ref, gss_ref, s0_ref) = refs[:n_in]
    cast_in = refs[n_in:n_in + n_cast]
    y_ref, sp_ref, gated_s_ref, ss_ref = refs[n_in + n_cast:n_in + n_cast + n_out]
    cast_out = refs[n_in + n_cast + n_out:n_in + 2 * n_cast + n_out]
    (h_scr, q_scr, k_scr, v_scr, gs_scr, kdt_scr, gated_ref, o_scr, sp_bf,
     ss_bf) = refs[n_in + 2 * n_cast + n_out:]
    _ret_layer_body(x_ref, xn_ref, g_ref, wq_ref, wk_ref, wv_ref, wg_ref, cos_ref, sin_ref, wo_ref,
                    qs_ref, ks_ref, vs_ref, gss_ref, s0_ref, y_ref, sp_ref, gated_s_ref, ss_ref,
                    h_scr, q_scr, k_scr, v_scr, gs_scr, kdt_scr, gated_ref, o_scr, sp_bf, ss_bf,
                    chunk=chunk, nb=nb)
    for src_ref, dst_ref in zip(cast_in, cast_out):
        dst_ref[...] = src_ref[...].astype(dst_ref.dtype)


def _ret_layer_body(x_ref, xn_ref, g_ref, wq_ref, wk_ref, wv_ref, wg_ref, cos_ref, sin_ref, wo_ref,
                    qs_ref, ks_ref, vs_ref, gss_ref, s0_ref,
                    y_ref, sp_ref, gated_s_ref, ss_ref,
                    h_scr, q_scr, k_scr, v_scr, gs_scr, kdt_scr, gated_ref, o_scr, sp_bf, ss_bf,
                    *, chunk, nb):
    @pl.when(pl.program_id(1) == 0)
    def _():
        sp_ref[...] = jnp.zeros_like(sp_ref)
        sp_bf[...] = jnp.zeros_like(sp_bf)

    @pl.when(jnp.logical_and(pl.program_id(0) == 0, pl.program_id(1) == 0))
    def _():
        h_scr[...] = _rms(x_ref[...], g_ref[...]).astype(BF16)

    heads = range(RET_HEADS)
    _ret_projections(h_scr[...], wq_ref, wk_ref, wv_ref, wg_ref, cos_ref, sin_ref,
                     q_scr, k_scr, v_scr, gs_scr)
    prompt = _RetUnit(q_scr[...], k_scr[...], v_scr[...], gs_scr[...],
                      sp_ref.at[0], sp_ref.at[0], sp_bf, sp_bf, chunk)
    for hd in heads:
        kdt_scr[hd] = _ret_decayed_keys(prompt, hd).T.astype(BF16)

    pad = SAMPLE_KEY_ROWS - DEC_SEQ
    kpad = jnp.zeros((pad, ks_ref.shape[1]), F32)
    vpad = jnp.zeros((pad, vs_ref.shape[1]), F32)
    samples = []
    for b in range(nb):
        rows = slice(b * DEC_SEQ, (b + 1) * DEC_SEQ)
        samples.append(_RetUnit(
            qs_ref[rows, :].astype(BF16),
            jnp.concatenate([ks_ref[rows, :], kpad], axis=0).astype(BF16),
            jnp.concatenate([vs_ref[rows, :], vpad], axis=0).astype(BF16),
            gss_ref[rows, :], s0_ref.at[b], ss_ref.at[b], ss_bf.at[b], None, DEC_SEQ))
    scores_s = [[_ret_scores(u, hd) for hd in heads] for u in samples]
    for u in samples:
        for hd in heads:
            _ret_update_state(u, hd)
    for b in range(nb):
        for hd in heads:
            ss_bf[b, hd] = s0_ref[b, hd].astype(BF16)

    scores_p = [_ret_scores(prompt, hd) for hd in heads]
    o_s = [[_ret_output(u, hd, sc[hd]) for hd in heads] for u, sc in zip(samples, scores_s)]
    for hd in heads:
        o_scr[hd] = _ret_output(prompt, hd, scores_p[hd])
    for hd in heads:
        _ret_update_state(prompt, hd, kdt_scr[hd])

    for hd in heads:
        cols = slice(hd * RET_DV, (hd + 1) * RET_DV)
        for r0 in range(0, chunk, NORM_ROWS):
            rows = slice(r0, r0 + NORM_ROWS)
            gated_ref[rows, cols] = _ret_gate(o_scr[hd, rows, :], gs_scr[rows, cols]).astype(BF16)
        for b, u in enumerate(samples):
            rows = slice(b * DEC_SEQ, (b + 1) * DEC_SEQ)
            gated_s_ref[rows, cols] = _ret_gate(o_s[b][hd], u.gsc[:, cols])
    y_ref[...] = x_ref[...] + jnp.dot(gated_ref[...], wo_ref[...], preferred_element_type=F32)
    h_scr[...] = _rms(xn_ref[...], g_ref[...]).astype(BF16)


def _cast_slab_spec(rows, cols, n_steps):
    for span in (1, 2, 4):
        slab, rem = divmod(rows * span, n_steps)
        if rem == 0 and slab % BF16_SUBLANES == 0:
            return slab, span
    raise ValueError(f"no slab tiling for {rows} rows over {n_steps} steps")


def _ret_layer(x, gamma, wq, wk, wv, wg, wo, layer, cos, sin, qs, ks, vs, gss, s0, batch, chunk, to_cast):
    n = x.shape[0]
    steps = n // batch // chunk
    cast_specs = []
    for w in to_cast:
        slab, span = _cast_slab_spec(w.shape[0], w.shape[1], batch * steps)
        cast_specs.append(pl.BlockSpec(
            (slab, w.shape[1]), lambda b, c, span=span: ((b * steps + c) // span, 0)))
    assert cos.shape[0] == steps * chunk
    nb = s0.shape[0] // (batch * steps)
    assert nb * batch * steps == s0.shape[0]
    nqk = RET_HEADS * RET_DK
    nv = RET_HEADS * RET_DV
    ts = nb * DEC_SEQ
    row = lambda b, c: (b * steps + c, 0)
    next_row = lambda b, c: (jnp.minimum(b * steps + c + 1, batch * steps - 1), 0)
    pos = lambda b, c: (c, 0)
    state_s = pl.BlockSpec((nb, RET_HEADS, RET_DK, RET_DV), lambda b, c: (b * steps + c, 0, 0, 0))
    return pl.pallas_call(
        functools.partial(_ret_layer_kernel, chunk=chunk, nb=nb, n_cast=len(to_cast)),
        grid=(batch, steps),
        in_specs=[
            pl.BlockSpec((chunk, D_MODEL), row),
            pl.BlockSpec((chunk, D_MODEL), next_row),
            _resident((1, D_MODEL)),
            _layer_resident(wq.shape, layer),
            _layer_resident(wk.shape, layer),
            _layer_resident(wv.shape, layer),
            _layer_resident(wg.shape, layer),
            pl.BlockSpec((chunk, LANES), pos),
            pl.BlockSpec((chunk, LANES), pos),
            _layer_resident(wo.shape, layer),
            pl.BlockSpec((ts, nqk), row),
            pl.BlockSpec((ts, nqk), row),
            pl.BlockSpec((ts, nv), row),
            pl.BlockSpec((ts, nv), row),
            state_s,
        ] + cast_specs,
        out_specs=[
            pl.BlockSpec((chunk, D_MODEL), row),
            pl.BlockSpec((1, RET_HEADS, RET_DK, RET_DV), lambda b, c: (b, 0, 0, 0)),
            pl.BlockSpec((ts, nv), row),
            state_s,
        ] + cast_specs,
        out_shape=[
            jax.ShapeDtypeStruct((n, D_MODEL), F32),
            jax.ShapeDtypeStruct((batch, RET_HEADS, RET_DK, RET_DV), F32),
            jax.ShapeDtypeStruct((s0.shape[0] * DEC_SEQ, nv), F32),
            jax.ShapeDtypeStruct(s0.shape, F32),
        ] + [jax.ShapeDtypeStruct(w.shape, BF16) for w in to_cast],
        scratch_shapes=[
            pltpu.VMEM((chunk, D_MODEL), BF16),
            pltpu.VMEM((chunk, nqk), BF16),
            pltpu.VMEM((chunk, nqk), BF16),
            pltpu.VMEM((chunk, nv), BF16),
            pltpu.VMEM((chunk, nv), BF16),
            pltpu.VMEM((RET_HEADS, RET_DK, chunk), BF16),
            pltpu.VMEM((chunk, nv), BF16),
            pltpu.VMEM((RET_HEADS, chunk, RET_DV), F32),
            pltpu.VMEM((RET_HEADS, RET_DK, RET_DV), BF16),
            pltpu.VMEM((nb, RET_HEADS, RET_DK, RET_DV), BF16),
        ],
        compiler_params=_params(2),
        name="ret_layer",
    )(x, x, gamma, wq, wk, wv, wg, cos, sin, wo, qs, ks, vs, gss, s0, *to_cast)


def _ffn_block(x, g_ref, w1_ref, w3_ref, w2_ref, gf_ref, final_norm):
    h = _rms(x, g_ref[...]).astype(BF16)
    y = x
    for lo, hi in zip(FFN_SPLITS[:-1], FFN_SPLITS[1:]):
        cols = slice(lo, hi)
        a = jnp.dot(h, w1_ref[:, cols], preferred_element_type=F32)
        b = jnp.dot(h, w3_ref[:, cols], preferred_element_type=F32)
        act = (_silu(a) * b).astype(BF16)
        y = y + jnp.dot(act, w2_ref[cols, :], preferred_element_type=F32)
    return _rms(y, gf_ref[...]) if final_norm else y


def _ffn_kernel(x_ref, g_ref, w1_ref, w3_ref, w2_ref, gf_ref, y_ref, *, final_norm):
    y_ref[...] = _ffn_block(x_ref[...], g_ref, w1_ref, w3_ref, w2_ref, gf_ref, final_norm)


def _mixer_out_ffn_kernel(a_ref, wl_ref, bl_ref, res_ref, g_ref, w1_ref, w3_ref, w2_ref, gf_ref, y_ref,
                          *, final_norm):
    x = res_ref[...] + jnp.dot(a_ref[...].astype(BF16), wl_ref[...],
                               preferred_element_type=F32) + bl_ref[...]
    y_ref[...] = _ffn_block(x, g_ref, w1_ref, w3_ref, w2_ref, gf_ref, final_norm)


def _mixer_out_ffn(a, wl, bl, res, gamma, w1, w3, w2, layer, gamma_final, tm, final_norm):
    n, kdim = a.shape
    row = lambda i: (i, 0)
    return pl.pallas_call(
        functools.partial(_mixer_out_ffn_kernel, final_norm=final_norm),
        grid=(n // tm,),
        in_specs=[
            pl.BlockSpec((tm, kdim), row),
            _resident(wl.shape),
            _resident(bl.shape),
            pl.BlockSpec((tm, D_MODEL), row),
            _resident((1, D_MODEL)),
            _layer_resident(w1.shape, layer),
            _layer_resident(w3.shape, layer),
            _layer_resident(w2.shape, layer),
            _resident((1, D_MODEL)),
        ],
        out_specs=pl.BlockSpec((tm, D_MODEL), row),
        out_shape=jax.ShapeDtypeStruct((n, D_MODEL), F32),
        compiler_params=_params(1),
        name="mixer_out_ffn",
    )(a, wl, bl, res, gamma, w1, w3, w2, gamma_final)


def _ffn(x, gamma, w1, w3, w2, layer, gamma_final, tm, final_norm):
    n = x.shape[0]
    row = lambda i: (i, 0)
    return pl.pallas_call(
        functools.partial(_ffn_kernel, final_norm=final_norm),
        grid=(n // tm,),
        in_specs=[
            pl.BlockSpec((tm, D_MODEL), row),
            _resident((1, D_MODEL)),
            _layer_resident(w1.shape, layer),
            _layer_resident(w3.shape, layer),
            _layer_resident(w2.shape, layer),
            _resident((1, D_MODEL)),
        ],
        out_specs=pl.BlockSpec((tm, D_MODEL), row),
        out_shape=jax.ShapeDtypeStruct((n, D_MODEL), F32),
        compiler_params=_params(1),
        name="ffn",
    )(x, gamma, w1, w3, w2, gamma_final)


def _swa_proj_kernel(x_ref, g_ref, wq_ref, bq_ref, wkv_ref, bkv_ref, q_ref, kv_ref):
    h = _rms(x_ref[...], g_ref[...]).astype(BF16)
    q = jnp.dot(h, wq_ref[...], preferred_element_type=F32) + bq_ref[...]
    q_ref[...] = (q * (SWA_HEAD_DIM ** -0.5)).astype(q_ref.dtype)
    kv_ref[...] = jnp.dot(h, wkv_ref[...], preferred_element_type=F32) + bkv_ref[...]


def _swa_proj(x, gamma, wq, bq, wkv, bkv, tm, q_dtype):
    n = x.shape[0]
    nq = SWA_HEADS * SWA_HEAD_DIM
    nkv = 2 * SWA_KV_HEADS * SWA_HEAD_DIM
    row = lambda i: (i, 0)
    return pl.pallas_call(
        _swa_proj_kernel,
        grid=(n // tm,),
        in_specs=[
            pl.BlockSpec((tm, D_MODEL), row),
            _resident((1, D_MODEL)),
            _resident(wq.shape),
            _resident(bq.shape),
            _resident(wkv.shape),
            _resident(bkv.shape),
        ],
        out_specs=[pl.BlockSpec((tm, nq), row), pl.BlockSpec((tm, nkv), row)],
        out_shape=[
            jax.ShapeDtypeStruct((n, nq), q_dtype),
            jax.ShapeDtypeStruct((n, nkv), F32),
        ],
        compiler_params=_params(1),
        name="swa_proj",
    )(x, gamma, wq, bq, wkv, bkv)


def _swa_attend(q, k2, v2, prev_valid, sinks_ref):
    r = q.shape[0]
    hd = SWA_HEAD_DIM
    lane_lo = lax.broadcasted_iota(jnp.int32, (r, LANES), 1) < hd
    zero = jnp.zeros((r, LANES), q.dtype)
    pieces = []
    sink_pieces = []
    for j in range(SWA_GROUP):
        slab = q[:, j * LANES:(j + 1) * LANES]
        pieces.append(jnp.where(lane_lo, slab, zero))
        pieces.append(jnp.where(lane_lo, zero, slab))
        sink_pieces.append(jnp.full((r, 1), sinks_ref[j], F32))
        sink_pieces.append(jnp.full((r, 1), sinks_ref[SWA_GROUP + j], F32))
    qs = jnp.concatenate(pieces, axis=0).astype(BF16)
    sink = jnp.concatenate(sink_pieces, axis=0)
    rows = SWA_HEADS * r

    s2 = lax.dot_general(qs, k2, (((1,), (1,)), ((), ())), preferred_element_type=F32)
    qi = lax.broadcasted_iota(jnp.int32, (rows, WINDOW), 0) & (r - 1)
    kj = lax.broadcasted_iota(jnp.int32, (rows, WINDOW), 1)
    cur = kj <= qi
    prev_cap = jnp.where(prev_valid, jnp.float32(3.0e38), jnp.float32(NEG_INF))
    s = jnp.where(cur, s2[:, WINDOW:], jnp.minimum(s2[:, :WINDOW], prev_cap))
    m = jnp.maximum(jnp.max(s, axis=-1, keepdims=True), sink)
    p = jnp.exp(s - m)
    den = jnp.sum(p, axis=-1, keepdims=True) + jnp.exp(sink - m)
    p2 = jnp.concatenate([jnp.where(cur, 0.0, p), jnp.where(cur, p, 0.0)], axis=1).astype(BF16)
    o = jnp.dot(p2, v2, preferred_element_type=F32) / den
    outs = [jnp.where(lane_lo, o[(2 * j) * r:(2 * j + 1) * r], o[(2 * j + 1) * r:(2 * j + 2) * r])
            for j in range(SWA_GROUP)]
    return jnp.concatenate(outs, axis=1)


def _swa_prompt_kernel(sinks_ref, q_ref, kvp_ref, kvc_ref, x_ref, wo_ref, bo_ref, y_ref, o_ref,
                       *, n_sub):
    nk = SWA_KV_HEADS * SWA_HEAD_DIM
    r = WINDOW
    lane_lo = lax.broadcasted_iota(jnp.int32, (r, LANES), 1) < SWA_HEAD_DIM
    qi = lax.broadcasted_iota(jnp.int32, (r, WINDOW), 0)
    kj = lax.broadcasted_iota(jnp.int32, (r, WINDOW), 1)
    cur_mask = kj <= qi
    ones = jnp.ones((2 * WINDOW, LANES), BF16)
    nt = (((1,), (1,)), ((), ()))
    for c in range(n_sub):
        rows = slice(c * WINDOW, (c + 1) * WINDOW)
        prev = kvp_ref[...] if c == 0 else kvc_ref[(c - 1) * WINDOW:c * WINDOW, :]
        cur = kvc_ref[rows, :]
        k2 = jnp.concatenate([prev[:, :nk], cur[:, :nk]], axis=0).astype(BF16)
        v2 = jnp.concatenate([prev[:, nk:], cur[:, nk:]], axis=0).astype(BF16)
        v2_ones = jnp.concatenate([v2, ones], axis=1)
        prev_valid = (pl.program_id(1) > 0) if c == 0 else True
        prev_cap = jnp.where(prev_valid, jnp.float32(3.0e38), jnp.float32(NEG_INF))
        for j in range(SWA_GROUP):
            slab = q_ref[rows, j * LANES:(j + 1) * LANES]
            zero = jnp.zeros_like(slab)
            num = []
            den = []
            for side in range(2):
                qm = jnp.where(lane_lo, slab, zero) if side == 0 else jnp.where(lane_lo, zero, slab)
                s2 = lax.dot_general(qm, k2, nt, preferred_element_type=F32)
                s = jnp.where(cur_mask, s2[:, WINDOW:], jnp.minimum(s2[:, :WINDOW], prev_cap))
                sink = sinks_ref[j + side * SWA_GROUP]
                m = jnp.maximum(jnp.max(s, axis=-1, keepdims=True), sink)
                p = jnp.exp(s - m)
                p2 = jnp.concatenate([jnp.where(cur_mask, 0.0, p), jnp.where(cur_mask, p, 0.0)],
                                     axis=1).astype(BF16)
                o2 = jnp.dot(p2, v2_ones, preferred_element_type=F32)
                num.append(o2[:, :LANES])
                den.append(o2[:, LANES:] + jnp.exp(sink - m))
            o_ref[rows, j * LANES:(j + 1) * LANES] = (
                jnp.where(lane_lo, num[0], num[1]) / jnp.where(lane_lo, den[0], den[1])).astype(BF16)
    y = jnp.dot(o_ref[...], wo_ref[...], preferred_element_type=F32)
    y_ref[...] = x_ref[...] + y + bo_ref[...]


def _swa_sample_element(b, sinks_ref, q_ref, kvn_ref, ck_ref, cv_ref, o_ref, nk_ref, nv_ref):
    nk = SWA_KV_HEADS * SWA_HEAD_DIM
    t = DEC_SEQ
    zpad = jnp.zeros((WINDOW - t, nk), F32)
    rows = slice(b * t, (b + 1) * t)
    k_new = kvn_ref[rows, :nk]
    v_new = kvn_ref[rows, nk:]
    ck = ck_ref[b]
    cv = cv_ref[b]
    k2 = jnp.concatenate([ck, k_new, zpad], axis=0).astype(BF16)
    v2 = jnp.concatenate([cv, v_new, zpad], axis=0).astype(BF16)
    o_ref[rows, :] = _swa_attend(q_ref[rows, :], k2, v2, True, sinks_ref)
    nk_ref[b, :WINDOW - t, :] = ck[t:, :]
    nk_ref[b, WINDOW - t:, :] = k_new
    nv_ref[b, :WINDOW - t, :] = cv[t:, :]
    nv_ref[b, WINDOW - t:, :] = v_new


def _swa_both_kernel(sinks_ref, q_ref, kvp_ref, kvc_ref, x_ref, wo_ref, bo_ref,
                     qs_ref, kvn_ref, ck_ref, cv_ref,
                     y_ref, os_ref, nk_ref, nv_ref, o_scr, *, n_sub, nb):
    for b in range(nb):
        _swa_sample_element(b, sinks_ref, qs_ref, kvn_ref, ck_ref, cv_ref, os_ref, nk_ref, nv_ref)
    _swa_prompt_kernel(sinks_ref, q_ref, kvp_ref, kvc_ref, x_ref, wo_ref, bo_ref, y_ref, o_scr,
                       n_sub=n_sub)


def _swa_both(q, kv, x, wo, bo, sinks, batch, n_sub, qs, kv_new, cache_k, cache_v):
    n = x.shape[0]
    tm = n_sub * WINDOW
    steps = n // batch // tm
    nb = cache_k.shape[0] // (batch * steps)
    assert nb * batch * steps == cache_k.shape[0]
    nq = SWA_HEADS * SWA_HEAD_DIM
    nk = SWA_KV_HEADS * SWA_HEAD_DIM
    nkv = kv.shape[1]
    row = lambda b, i: (b * steps + i, 0)
    prev = lambda b, i: ((b * steps + i) * n_sub - jnp.minimum(i, 1), 0)
    cache = pl.BlockSpec((nb, WINDOW, nk), lambda b, i: (b * steps + i, 0, 0))
    return pl.pallas_call(
        functools.partial(_swa_both_kernel, n_sub=n_sub, nb=nb),
        grid=(batch, steps),
        in_specs=[
            pl.BlockSpec(memory_space=pltpu.SMEM),
            pl.BlockSpec((tm, nq), row),
            pl.BlockSpec((WINDOW, nkv), prev),
            pl.BlockSpec((tm, nkv), row),
            pl.BlockSpec((tm, D_MODEL), row),
            _resident(wo.shape),
            _resident(bo.shape),
            pl.BlockSpec((nb * DEC_SEQ, nq), row),
            pl.BlockSpec((nb * DEC_SEQ, 2 * nk), row),
            cache,
            cache,
        ],
        out_specs=[
            pl.BlockSpec((tm, D_MODEL), row),
            pl.BlockSpec((nb * DEC_SEQ, nq), row),
            cache,
            cache,
        ],
        out_shape=[
            jax.ShapeDtypeStruct((n, D_MODEL), F32),
            jax.ShapeDtypeStruct((cache_k.shape[0] * DEC_SEQ, nq), F32),
            jax.ShapeDtypeStruct(cache_k.shape, F32),
            jax.ShapeDtypeStruct(cache_v.shape, F32),
        ],
        scratch_shapes=[pltpu.VMEM((tm, nq), BF16)],
        compiler_params=_params(2),
        name="swa_both",
    )(sinks, q, kv, kv, x, wo, bo, qs, kv_new, cache_k, cache_v)


def _rotation_tables(positions, repeat):
    half = RET_DK // 2
    inv = 1.0 / (ROPE_BASE ** (np.arange(half, dtype=np.float64) / half))
    ang = np.asarray(positions, np.float64)[:, None] * inv[None, :]
    cos = np.tile(np.cos(ang), (repeat, 1)).astype(np.float32)
    sin = np.tile(np.sin(ang), (repeat, 1)).astype(np.float32)
    return jnp.asarray(cos), jnp.asarray(sin)


TM_PROJ = 512
TM_SAMPLE = 512
TM_FFN = 1024
FFN_SPLITS = (0, 1536, D_FF)
TM_SAMPLE_FFN = 512


def kernel(x_prompt, x_sample, state_ret, cache_swa_k, cache_swa_v, ret_w_q, ret_w_k, ret_w_v, ret_w_g, ret_w_o, swa_w_qkv, swa_b_qkv, swa_w_o, swa_b_o, swa_sinks, norm_mix, norm_ffn, ffn_w1, ffn_w3, ffn_w2, norm_final):
    bp, tp, d = x_prompt.shape
    bs, ts, _ = x_sample.shape
    assert (tp, d, ts) == (SEQ, D_MODEL, DEC_SEQ)
    xp = x_prompt.reshape(bp * tp, d)
    xs = x_sample.reshape(bs * ts, d)
    nk = SWA_KV_HEADS * SWA_HEAD_DIM

    cos_p, sin_p = _rotation_tables(np.arange(tp), 1)
    cos_s, sin_s = _rotation_tables(PAST_LEN + np.arange(ts), TM_SAMPLE // ts)
    gamma_final = norm_final.reshape(1, d)
    zero_bias = jnp.zeros((1, d), F32)

    ret_w = [w.astype(BF16) for w in (ret_w_q, ret_w_k, ret_w_v, ret_w_g)]
    w_ro = ret_w_o.astype(BF16)
    g_mix = norm_mix[0].reshape(1, d)
    qs, ks, vs, gs = _ret_proj(xs, g_mix, *ret_w, 0, cos_s, sin_s, TM_SAMPLE, F32)
    ffn_f32 = [w.reshape(-1, w.shape[-1]) for w in (ffn_w1, ffn_w3, ffn_w2)]
    xp, state_p, gated_s, state_s, w1, w3, w2 = _ret_layer(
        xp, g_mix, *ret_w, w_ro, 0, cos_p, sin_p, qs, ks, vs, gs, state_ret[0], bp, RET_PROMPT_CHUNK,
        ffn_f32)
    w1 = w1.reshape(ffn_w1.shape)
    w3 = w3.reshape(ffn_w3.shape)
    w2 = w2.reshape(ffn_w2.shape)
    g_ffn = norm_ffn[0].reshape(1, d)
    xp = _ffn(xp, g_ffn, w1, w3, w2, 0, gamma_final, TM_FFN, False)
    xs = _mixer_out_ffn(gated_s, w_ro[0], zero_bias, xs, g_ffn, w1, w3, w2, 0, gamma_final,
                        TM_SAMPLE_FFN, False)

    nq = SWA_HEADS * SWA_HEAD_DIM
    heads = (SWA_KV_HEADS, SWA_GROUP, SWA_HEAD_DIM)
    w_q = swa_w_qkv[0][:, :nq].reshape(d, *heads).transpose(0, 2, 1, 3).reshape(d, nq).astype(BF16)
    b_q = swa_b_qkv[0][:nq].reshape(heads).transpose(1, 0, 2).reshape(1, nq)
    w_kv = swa_w_qkv[0][:, nq:].astype(BF16)
    b_kv = swa_b_qkv[0][nq:].reshape(1, 2 * nk)
    w_so = swa_w_o[0].reshape(*heads, d).transpose(1, 0, 2, 3).reshape(nq, d).astype(BF16)
    b_so = swa_b_o[0].reshape(1, d)
    sinks = swa_sinks[0]
    g_mix = norm_mix[1].reshape(1, d)
    qs, kvs = _swa_proj(xs, g_mix, w_q, b_q, w_kv, b_kv, TM_SAMPLE, F32)
    qp, kvp = _swa_proj(xp, g_mix, w_q, b_q, w_kv, b_kv, 2 * TM_PROJ, BF16)
    xp, att_s, new_k, new_v = _swa_both(
        qp, kvp, xp, w_so, b_so, sinks, bp, 8, qs, kvs,
        cache_swa_k[0].reshape(bs, WINDOW, nk), cache_swa_v[0].reshape(bs, WINDOW, nk))

    g_ffn = norm_ffn[1].reshape(1, d)
    yp = _ffn(xp, g_ffn, w1, w3, w2, 1, gamma_final, TM_FFN, True)
    ys = _mixer_out_ffn(att_s, w_so, b_so, xs, g_ffn, w1, w3, w2, 1, gamma_final, TM_SAMPLE_FFN, True)

    kv_tail = kvp.reshape(bp, tp, 2 * nk)[:, tp - WINDOW:, :]
    kv_shape = (1, bp, WINDOW, SWA_KV_HEADS, SWA_HEAD_DIM)
    cache_shape = (1, bs, WINDOW, SWA_KV_HEADS, SWA_HEAD_DIM)
    return (
        yp.reshape(bp, tp, d),
        ys.reshape(bs, ts, d),
        state_p[None],
        state_s[None],
        kv_tail[..., :nk].reshape(kv_shape),
        kv_tail[..., nk:].reshape(kv_shape),
        new_k.reshape(cache_shape),
        new_v.reshape(cache_shape),
    )
```

```python
import functools
import math
from typing import NamedTuple

import numpy as np
import jax
import jax.numpy as jnp
from jax import lax
from jax.experimental import pallas as pl
from jax.experimental.pallas import tpu as pltpu

F32 = jnp.float32
BF16 = jnp.bfloat16

D_MODEL = 1024
SEQ = 2048
DEC_SEQ = 8
PAST_LEN = 16384

RET_HEADS = 4
RET_DK = 256
RET_DV = 512
RET_CHUNK = 128
RET_PROMPT_CHUNK = 256
NORM_ROWS = 64
SAMPLE_KEY_ROWS = 16
ROPE_BASE = 10000.0

SWA_HEADS = 16
SWA_KV_HEADS = 2
SWA_HEAD_DIM = 64
SWA_GROUP = SWA_HEADS // SWA_KV_HEADS
WINDOW = 128

D_FF = 2816
RMS_EPS = 1e-6
GN_EPS = 1e-6
NEG_INF = -1e30

LANES = 128
BF16_SUBLANES = 16
VMEM_LIMIT = 56 * 1024 * 1024


def _params(n_axes):
    return pltpu.CompilerParams(
        dimension_semantics=("arbitrary",) * n_axes, vmem_limit_bytes=VMEM_LIMIT)


def _rms(x, g):
    ms = jnp.mean(x * x, axis=-1, keepdims=True)
    return x * lax.rsqrt(ms + RMS_EPS) * g


def _silu(x):
    return x * jax.nn.sigmoid(x)


def _resident(shape):
    zeros = (0,) * len(shape)
    return pl.BlockSpec(shape, lambda *_: zeros, pipeline_mode=pl.Buffered(1))


def _layer_resident(stacked_shape, layer):
    return pl.BlockSpec((None,) + tuple(stacked_shape[1:]), lambda *_: (layer, 0, 0),
                        pipeline_mode=pl.Buffered(1))


def _ret_proj_kernel(x_ref, g_ref, wq_ref, wk_ref, wv_ref, wg_ref, cos_ref, sin_ref,
                     q_ref, k_ref, v_ref, gs_ref):
    h = _rms(x_ref[...], g_ref[...]).astype(BF16)
    _ret_projections(h, wq_ref, wk_ref, wv_ref, wg_ref, cos_ref, sin_ref, q_ref, k_ref, v_ref, gs_ref)


def _ret_projections(h, wq_ref, wk_ref, wv_ref, wg_ref, cos_ref, sin_ref, q_ref, k_ref, v_ref, gs_ref):
    cos = cos_ref[...]
    sin = sin_ref[...]
    half = RET_DK // 2

    def rotated(w_ref, out_ref, scale):
        y = jnp.dot(h, w_ref[...], preferred_element_type=F32)
        for hd in range(RET_HEADS):
            lo = hd * RET_DK
            a = y[:, lo:lo + half]
            b = y[:, lo + half:lo + RET_DK]
            out_ref[:, lo:lo + half] = ((a * cos - b * sin) * scale).astype(out_ref.dtype)
            out_ref[:, lo + half:lo + RET_DK] = ((a * sin + b * cos) * scale).astype(out_ref.dtype)

    g = jnp.dot(h, wg_ref[...], preferred_element_type=F32)
    gs_ref[...] = _silu(g).astype(gs_ref.dtype)
    v_ref[...] = jnp.dot(h, wv_ref[...], preferred_element_type=F32).astype(v_ref.dtype)
    rotated(wq_ref, q_ref, 1.0)
    rotated(wk_ref, k_ref, RET_DK ** -0.5)


def _ret_proj(x, gamma, wq, wk, wv, wg, layer, cos, sin, tm, out_dtype):
    n = x.shape[0]
    nqk = RET_HEADS * RET_DK
    nv = RET_HEADS * RET_DV
    n_pos_blocks = cos.shape[0] // tm
    row = lambda i: (i, 0)
    return pl.pallas_call(
        _ret_proj_kernel,
        grid=(n // tm,),
        in_specs=[
            pl.BlockSpec((tm, D_MODEL), row),
            _resident((1, D_MODEL)),
            _layer_resident(wq.shape, layer),
            _layer_resident(wk.shape, layer),
            _layer_resident(wv.shape, layer),
            _layer_resident(wg.shape, layer),
            pl.BlockSpec((tm, LANES), lambda i: (i % n_pos_blocks, 0)),
            pl.BlockSpec((tm, LANES), lambda i: (i % n_pos_blocks, 0)),
        ],
        out_specs=[
            pl.BlockSpec((tm, nqk), row),
            pl.BlockSpec((tm, nqk), row),
            pl.BlockSpec((tm, nv), row),
            pl.BlockSpec((tm, nv), row),
        ],
        out_shape=[
            jax.ShapeDtypeStruct((n, nqk), out_dtype),
            jax.ShapeDtypeStruct((n, nqk), out_dtype),
            jax.ShapeDtypeStruct((n, nv), out_dtype),
            jax.ShapeDtypeStruct((n, nv), out_dtype),
        ],
        compiler_params=_params(1),
        name="ret_proj",
    )(x, gamma, wq, wk, wv, wg, cos, sin)


class _RetUnit(NamedTuple):
    qc: jax.Array
    kc: jax.Array
    vc: jax.Array
    gsc: jax.Array
    s_in: object
    s_out: object
    s_bf_in: object
    s_bf_out: object
    chunk: int


def _ret_head(u, hd):
    return (u.qc[:, hd * RET_DK:(hd + 1) * RET_DK], u.kc[:, hd * RET_DK:(hd + 1) * RET_DK],
            u.vc[:, hd * RET_DV:(hd + 1) * RET_DV])


def _ret_scores(u, hd):
    qh, kh, _ = _ret_head(u, hd)
    return lax.dot_general(qh, kh, (((1,), (1,)), ((), ())), preferred_element_type=F32)


def _ret_output(u, hd, scores):
    qh, _, vh = _ret_head(u, hd)
    cq, ck = scores.shape
    log_gamma = math.log1p(-(2.0 ** -(5.0 + hd)))
    ii = lax.broadcasted_iota(jnp.int32, (cq, ck), 0)
    jj = lax.broadcasted_iota(jnp.int32, (cq, ck), 1)
    diff = (ii - jj).astype(F32)
    intra = jnp.where(diff >= 0, jnp.exp(log_gamma * jnp.maximum(diff, 0.0)), 0.0)
    irow = lax.broadcasted_iota(jnp.int32, (cq, 1), 0).astype(F32)
    q_dec = (qh.astype(F32) * jnp.exp(log_gamma * (irow + 1.0))).astype(BF16)
    lhs = jnp.concatenate([q_dec, (scores * intra).astype(BF16)], axis=1)
    rhs = jnp.concatenate([u.s_bf_in[hd], vh], axis=0)
    return jnp.dot(lhs, rhs, preferred_element_type=F32)


def _ret_decayed_keys(u, hd):
    _, kh, _ = _ret_head(u, hd)
    log_gamma = math.log1p(-(2.0 ** -(5.0 + hd)))
    jrow = lax.broadcasted_iota(jnp.int32, (kh.shape[0], 1), 0).astype(F32)
    return kh.astype(F32) * jnp.exp(log_gamma * (u.chunk - 1.0 - jrow))


def _ret_update_state(u, hd, kd_t=None):
    _, _, vh = _ret_head(u, hd)
    log_gamma = math.log1p(-(2.0 ** -(5.0 + hd)))
    if kd_t is None:
        update = lax.dot_general(_ret_decayed_keys(u, hd).astype(BF16), vh, (((0,), (0,)), ((), ())),
                                 preferred_element_type=F32)
    else:
        update = jnp.dot(kd_t, vh, preferred_element_type=F32)
    s_new = u.s_in[hd] * math.exp(log_gamma * u.chunk) + update
    u.s_out[hd] = s_new
    if u.s_bf_out is not None:
        u.s_bf_out[hd] = s_new.astype(BF16)


def _ret_gate(o, gs):
    mu = jnp.mean(o, axis=-1, keepdims=True)
    oc = o - mu
    var = jnp.mean(oc * oc, axis=-1, keepdims=True)
    return oc * lax.rsqrt(var + GN_EPS) * gs.astype(F32)


def _ret_layer_kernel(*refs, chunk, nb, n_cast):
    n_in, n_out = 15, 4
    (x_ref, xn_ref, g_ref, wq_ref, wk_ref, wv_ref, wg_ref, cos_ref, sin_ref, wo_ref,
     qs_ref, ks_ref, vs_ref, gss_ref, s0_ref) = refs[:n_in]
    cast_in = refs[n_in:n_in + n_cast]
    y_ref, sp_ref, gated_s_ref, ss_ref = refs[n_in + n_cast:n_in + n_cast + n_out]
    cast_out = refs[n_in + n_cast + n_out:n_in + 2 * n_cast + n_out]
    (h_scr, q_scr, k_scr, v_scr, gs_scr, kdt_scr, gated_ref, o_scr, sp_bf,
     ss_bf) = refs[n_in + 2 * n_cast + n_out:]
    _ret_layer_body(x_ref, xn_ref, g_ref, wq_ref, wk_ref, wv_ref, wg_ref, cos_ref, sin_ref, wo_ref,
                    qs_ref, ks_ref, vs_ref, gss_ref, s0_ref, y_ref, sp_ref, gated_s_ref, ss_ref,
                    h_scr, q_scr, k_scr, v_scr, gs_scr, kdt_scr, gated_ref, o_scr, sp_bf, ss_bf,
                    chunk=chunk, nb=nb)
    for src_ref, dst_ref in zip(cast_in, cast_out):
        dst_ref[...] = src_ref[...].astype(dst_ref.dtype)


def _ret_layer_body(x_ref, xn_ref, g_ref, wq_ref, wk_ref, wv_ref, wg_ref, cos_ref, sin_ref, wo_ref,
                    qs_ref, ks_ref, vs_ref, gss_ref, s0_ref,
                    y_ref, sp_ref, gated_s_ref, ss_ref,
                    h_scr, q_scr, k_scr, v_scr, gs_scr, kdt_scr, gated_ref, o_scr, sp_bf, ss_bf,
                    *, chunk, nb):
    @pl.when(pl.program_id(1) == 0)
    def _():
        sp_ref[...] = jnp.zeros_like(sp_ref)
        sp_bf[...] = jnp.zeros_like(sp_bf)

    @pl.when(jnp.logical_and(pl.program_id(0) == 0, pl.program_id(1) == 0))
    def _():
        h_scr[...] = _rms(x_ref[...], g_ref[...]).astype(BF16)

    heads = range(RET_HEADS)
    _ret_projections(h_scr[...], wq_ref, wk_ref, wv_ref, wg_ref, cos_ref, sin_ref,
                     q_scr, k_scr, v_scr, gs_scr)
    prompt = _RetUnit(q_scr[...], k_scr[...], v_scr[...], gs_scr[...],
                      sp_ref.at[0], sp_ref.at[0], sp_bf, sp_bf, chunk)
    for hd in heads:
        kdt_scr[hd] = _ret_decayed_keys(prompt, hd).T.astype(BF16)

    pad = SAMPLE_KEY_ROWS - DEC_SEQ
    kpad = jnp.zeros((pad, ks_ref.shape[1]), F32)
    vpad = jnp.zeros((pad, vs_ref.shape[1]), F32)
    samples = []
    for b in range(nb):
        rows = slice(b * DEC_SEQ, (b + 1) * DEC_SEQ)
        samples.append(_RetUnit(
            qs_ref[rows, :].astype(BF16),
            jnp.concatenate([ks_ref[rows, :], kpad], axis=0).astype(BF16),
            jnp.concatenate([vs_ref[rows, :], vpad], axis=0).astype(BF16),
            gss_ref[rows, :], s0_ref.at[b], ss_ref.at[b], ss_bf.at[b], None, DEC_SEQ))
    scores_s = [[_ret_scores(u, hd) for hd in heads] for u in samples]
    for u in samples:
        for hd in heads:
            _ret_update_state(u, hd)
    for b in range(nb):
        for hd in heads:
            ss_bf[b, hd] = s0_ref[b, hd].astype(BF16)

    scores_p = [_ret_scores(prompt, hd) for hd in heads]
    o_s = [[_ret_output(u, hd, sc[hd]) for hd in heads] for u, sc in zip(samples, scores_s)]
    for hd in heads:
        o_scr[hd] = _ret_output(prompt, hd, scores_p[hd])
    for hd in heads:
        _ret_update_state(prompt, hd, kdt_scr[hd])

    for hd in heads:
        cols = slice(hd * RET_DV, (hd + 1) * RET_DV)
        for r0 in range(0, chunk, NORM_ROWS):
            rows = slice(r0, r0 + NORM_ROWS)
            gated_ref[rows, cols] = _ret_gate(o_scr[hd, rows, :], gs_scr[rows, cols]).astype(BF16)
        for b, u in enumerate(samples):
            rows = slice(b * DEC_SEQ, (b + 1) * DEC_SEQ)
            gated_s_ref[rows, cols] = _ret_gate(o_s[b][hd], u.gsc[:, cols])
    y_ref[...] = x_ref[...] + jnp.dot(gated_ref[...], wo_ref[...], preferred_element_type=F32)
    h_scr[...] = _rms(xn_ref[...], g_ref[...]).astype(BF16)


def _cast_slab_spec(rows, cols, n_steps):
    for span in (1, 2, 4):
        slab, rem = divmod(rows * span, n_steps)
        if rem == 0 and slab % BF16_SUBLANES == 0:
            return slab, span
    raise ValueError(f"no slab tiling for {rows} rows over {n_steps} steps")


def _ret_layer(x, gamma, wq, wk, wv, wg, wo, layer, cos, sin, qs, ks, vs, gss, s0, batch, chunk, to_cast):
    n = x.shape[0]
    steps = n // batch // chunk
    cast_specs = []
    for w in to_cast:
        slab, span = _cast_slab_spec(w.shape[0], w.shape[1], batch * steps)
        cast_specs.append(pl.BlockSpec(
            (slab, w.shape[1]), lambda b, c, span=span: ((b * steps + c) // span, 0)))
    assert cos.shape[0] == steps * chunk
    nb = s0.shape[0] // (batch * steps)
    assert nb * batch * steps == s0.shape[0]
    nqk = RET_HEADS * RET_DK
    nv = RET_HEADS * RET_DV
    ts = nb * DEC_SEQ
    row = lambda b, c: (b * steps + c, 0)
    next_row = lambda b, c: (jnp.minimum(b * steps + c + 1, batch * steps - 1), 0)
    pos = lambda b, c: (c, 0)
    state_s = pl.BlockSpec((nb, RET_HEADS, RET_DK, RET_DV), lambda b, c: (b * steps + c, 0, 0, 0))
    return pl.pallas_call(
        functools.partial(_ret_layer_kernel, chunk=chunk, nb=nb, n_cast=len(to_cast)),
        grid=(batch, steps),
        in_specs=[
            pl.BlockSpec((chunk, D_MODEL), row),
            pl.BlockSpec((chunk, D_MODEL), next_row),
            _resident((1, D_MODEL)),
            _layer_resident(wq.shape, layer),
            _layer_resident(wk.shape, layer),
            _layer_resident(wv.shape, layer),
            _layer_resident(wg.shape, layer),
            pl.BlockSpec((chunk, LANES), pos),
            pl.BlockSpec((chunk, LANES), pos),
            _layer_resident(wo.shape, layer),
            pl.BlockSpec((ts, nqk), row),
            pl.BlockSpec((ts, nqk), row),
            pl.BlockSpec((ts, nv), row),
            pl.BlockSpec((ts, nv), row),
            state_s,
        ] + cast_specs,
        out_specs=[
            pl.BlockSpec((chunk, D_MODEL), row),
            pl.BlockSpec((1, RET_HEADS, RET_DK, RET_DV), lambda b, c: (b, 0, 0, 0)),
            pl.BlockSpec((ts, nv), row),
            state_s,
        ] + cast_specs,
        out_shape=[
            jax.ShapeDtypeStruct((n, D_MODEL), F32),
            jax.ShapeDtypeStruct((batch, RET_HEADS, RET_DK, RET_DV), F32),
            jax.ShapeDtypeStruct((s0.shape[0] * DEC_SEQ, nv), F32),
            jax.ShapeDtypeStruct(s0.shape, F32),
        ] + [jax.ShapeDtypeStruct(w.shape, BF16) for w in to_cast],
        scratch_shapes=[
            pltpu.VMEM((chunk, D_MODEL), BF16),
            pltpu.VMEM((chunk, nqk), BF16),
            pltpu.VMEM((chunk, nqk), BF16),
            pltpu.VMEM((chunk, nv), BF16),
            pltpu.VMEM((chunk, nv), BF16),
            pltpu.VMEM((RET_HEADS, RET_DK, chunk), BF16),
            pltpu.VMEM((chunk, nv), BF16),
            pltpu.VMEM((RET_HEADS, chunk, RET_DV), F32),
            pltpu.VMEM((RET_HEADS, RET_DK, RET_DV), BF16),
            pltpu.VMEM((nb, RET_HEADS, RET_DK, RET_DV), BF16),
        ],
        compiler_params=_params(2),
        name="ret_layer",
    )(x, x, gamma, wq, wk, wv, wg, cos, sin, wo, qs, ks, vs, gss, s0, *to_cast)


def _ffn_block(x, g_ref, w1_ref, w3_ref, w2_ref, gf_ref, final_norm):
    h = _rms(x, g_ref[...]).astype(BF16)
    y = x
    for lo, hi in zip(FFN_SPLITS[:-1], FFN_SPLITS[1:]):
        cols = slice(lo, hi)
        a = jnp.dot(h, w1_ref[:, cols], preferred_element_type=F32)
        b = jnp.dot(h, w3_ref[:, cols], preferred_element_type=F32)
        act = (_silu(a) * b).astype(BF16)
        y = y + jnp.dot(act, w2_ref[cols, :], preferred_element_type=F32)
    return _rms(y, gf_ref[...]) if final_norm else y


def _ffn_kernel(x_ref, g_ref, w1_ref, w3_ref, w2_ref, gf_ref, y_ref, *, final_norm):
    y_ref[...] = _ffn_block(x_ref[...], g_ref, w1_ref, w3_ref, w2_ref, gf_ref, final_norm)


def _mixer_out_ffn_kernel(a_ref, wl_ref, bl_ref, res_ref, g_ref, w1_ref, w3_ref, w2_ref, gf_ref, y_ref,
                          *, final_norm):
    x = res_ref[...] + jnp.dot(a_ref[...].astype(BF16), wl_ref[...],
                               preferred_element_type=F32) + bl_ref[...]
    y_ref[...] = _ffn_block(x, g_ref, w1_ref, w3_ref, w2_ref, gf_ref, final_norm)


def _mixer_out_ffn(a, wl, bl, res, gamma, w1, w3, w2, layer, gamma_final, tm, final_norm):
    n, kdim = a.shape
    row = lambda i: (i, 0)
    return pl.pallas_call(
        functools.partial(_mixer_out_ffn_kernel, final_norm=final_norm),
        grid=(n // tm,),
        in_specs=[
            pl.BlockSpec((tm, kdim), row),
            _resident(wl.shape),
            _resident(bl.shape),
            pl.BlockSpec((tm, D_MODEL), row),
            _resident((1, D_MODEL)),
            _layer_resident(w1.shape, layer),
            _layer_resident(w3.shape, layer),
            _layer_resident(w2.shape, layer),
            _resident((1, D_MODEL)),
        ],
        out_specs=pl.BlockSpec((tm, D_MODEL), row),
        out_shape=jax.ShapeDtypeStruct((n, D_MODEL), F32),
        compiler_params=_params(1),
        name="mixer_out_ffn",
    )(a, wl, bl, res, gamma, w1, w3, w2, gamma_final)


def _ffn(x, gamma, w1, w3, w2, layer, gamma_final, tm, final_norm):
    n = x.shape[0]
    row = lambda i: (i, 0)
    return pl.pallas_call(
        functools.partial(_ffn_kernel, final_norm=final_norm),
        grid=(n // tm,),
        in_specs=[
            pl.BlockSpec((tm, D_MODEL), row),
            _resident((1, D_MODEL)),
            _layer_resident(w1.shape, layer),
            _layer_resident(w3.shape, layer),
            _layer_resident(w2.shape, layer),
            _resident((1, D_MODEL)),
        ],
        out_specs=pl.BlockSpec((tm, D_MODEL), row),
        out_shape=jax.ShapeDtypeStruct((n, D_MODEL), F32),
        compiler_params=_params(1),
        name="ffn",
    )(x, gamma, w1, w3, w2, gamma_final)


def _swa_proj_kernel(x_ref, g_ref, wq_ref, bq_ref, wkv_ref, bkv_ref, q_ref, kv_ref):
    h = _rms(x_ref[...], g_ref[...]).astype(BF16)
    q = jnp.dot(h, wq_ref[...], preferred_element_type=F32) + bq_ref[...]
    q_ref[...] = (q * (SWA_HEAD_DIM ** -0.5)).astype(q_ref.dtype)
    kv_ref[...] = jnp.dot(h, wkv_ref[...], preferred_element_type=F32) + bkv_ref[...]


def _swa_proj(x, gamma, wq, bq, wkv, bkv, tm, q_dtype):
    n = x.shape[0]
    nq = SWA_HEADS * SWA_HEAD_DIM
    nkv = 2 * SWA_KV_HEADS * SWA_HEAD_DIM
    row = lambda i: (i, 0)
    return pl.pallas_call(
        _swa_proj_kernel,
        grid=(n // tm,),
        in_specs=[
            pl.BlockSpec((tm, D_MODEL), row),
            _resident((1, D_MODEL)),
            _resident(wq.shape),
            _resident(bq.shape),
            _resident(wkv.shape),
            _resident(bkv.shape),
        ],
        out_specs=[pl.BlockSpec((tm, nq), row), pl.BlockSpec((tm, nkv), row)],
        out_shape=[
            jax.ShapeDtypeStruct((n, nq), q_dtype),
            jax.ShapeDtypeStruct((n, nkv), F32),
        ],
        compiler_params=_params(1),
        name="swa_proj",
    )(x, gamma, wq, bq, wkv, bkv)


def _swa_attend(qs, k2s, v2s, prev_valid, sinks_ref):
    r = qs[0].shape[0]
    hd = SWA_HEAD_DIM
    lane_lo = lax.broadcasted_iota(jnp.int32, (r, LANES), 1) < hd
    zero = jnp.zeros((r, LANES), qs[0].dtype)
    sink_pieces = []
    for j in range(SWA_GROUP):
        sink_pieces.append(jnp.full((r, 1), sinks_ref[j], F32))
        sink_pieces.append(jnp.full((r, 1), sinks_ref[SWA_GROUP + j], F32))
    sink = jnp.concatenate(sink_pieces * len(qs), axis=0)
    rows = SWA_HEADS * r

    s2_parts = []
    for q, k2 in zip(qs, k2s):
        pieces = []
        for j in range(SWA_GROUP):
            slab = q[:, j * LANES:(j + 1) * LANES]
            pieces.append(jnp.where(lane_lo, slab, zero))
            pieces.append(jnp.where(lane_lo, zero, slab))
        stacked = jnp.concatenate(pieces, axis=0).astype(BF16)
        s2_parts.append(lax.dot_general(stacked, k2, (((1,), (1,)), ((), ())),
                                        preferred_element_type=F32))
    s2 = jnp.concatenate(s2_parts, axis=0)
    total = rows * len(qs)
    qi = lax.broadcasted_iota(jnp.int32, (total, WINDOW), 0) & (r - 1)
    kj = lax.broadcasted_iota(jnp.int32, (total, WINDOW), 1)
    cur = kj <= qi
    prev_cap = jnp.where(prev_valid, jnp.float32(3.0e38), jnp.float32(NEG_INF))
    s = jnp.where(cur, s2[:, WINDOW:], jnp.minimum(s2[:, :WINDOW], prev_cap))
    m = jnp.maximum(jnp.max(s, axis=-1, keepdims=True), sink)
    p = jnp.exp(s - m)
    den = jnp.sum(p, axis=-1, keepdims=True) + jnp.exp(sink - m)
    p2 = jnp.concatenate([jnp.where(cur, 0.0, p), jnp.where(cur, p, 0.0)], axis=1).astype(BF16)
    results = []
    for e, v2 in enumerate(v2s):
        blk = slice(e * rows, (e + 1) * rows)
        o = jnp.dot(p2[blk], v2, preferred_element_type=F32) / den[blk]
        outs = [jnp.where(lane_lo, o[(2 * j) * r:(2 * j + 1) * r], o[(2 * j + 1) * r:(2 * j + 2) * r])
                for j in range(SWA_GROUP)]
        results.append(jnp.concatenate(outs, axis=1))
    return results


def _swa_prompt_kernel(sinks_ref, q_ref, kvp_ref, kvc_ref, x_ref, wo_ref, bo_ref, y_ref, o_ref,
                       *, n_sub):
    nk = SWA_KV_HEADS * SWA_HEAD_DIM
    r = WINDOW
    lane_lo = lax.broadcasted_iota(jnp.int32, (r, LANES), 1) < SWA_HEAD_DIM
    qi = lax.broadcasted_iota(jnp.int32, (r, WINDOW), 0)
    kj = lax.broadcasted_iota(jnp.int32, (r, WINDOW), 1)
    cur_mask = kj <= qi
    ones = jnp.ones((2 * WINDOW, LANES), BF16)
    nt = (((1,), (1,)), ((), ()))
    for c in range(n_sub):
        rows = slice(c * WINDOW, (c + 1) * WINDOW)
        prev = kvp_ref[...] if c == 0 else kvc_ref[(c - 1) * WINDOW:c * WINDOW, :]
        cur = kvc_ref[rows, :]
        k2 = jnp.concatenate([prev[:, :nk], cur[:, :nk]], axis=0).astype(BF16)
        v2 = jnp.concatenate([prev[:, nk:], cur[:, nk:]], axis=0).astype(BF16)
        v2_ones = jnp.concatenate([v2, ones], axis=1)
        prev_valid = (pl.program_id(1) > 0) if c == 0 else True
        prev_cap = jnp.where(prev_valid, jnp.float32(3.0e38), jnp.float32(NEG_INF))
        for j in range(SWA_GROUP):
            slab = q_ref[rows, j * LANES:(j + 1) * LANES]
            zero = jnp.zeros_like(slab)
            num = []
            den = []
            for side in range(2):
                qm = jnp.where(lane_lo, slab, zero) if side == 0 else jnp.where(lane_lo, zero, slab)
                s2 = lax.dot_general(qm, k2, nt, preferred_element_type=F32)
                s = jnp.where(cur_mask, s2[:, WINDOW:], jnp.minimum(s2[:, :WINDOW], prev_cap))
                sink = sinks_ref[j + side * SWA_GROUP]
                m = jnp.maximum(jnp.max(s, axis=-1, keepdims=True), sink)
                p = jnp.exp(s - m)
                p2 = jnp.concatenate([jnp.where(cur_mask, 0.0, p), jnp.where(cur_mask, p, 0.0)],
                                     axis=1).astype(BF16)
                o2 = jnp.dot(p2, v2_ones, preferred_element_type=F32)
                num.append(o2[:, :LANES])
                den.append(o2[:, LANES:] + jnp.exp(sink - m))
            o_ref[rows, j * LANES:(j + 1) * LANES] = (
                jnp.where(lane_lo, num[0], num[1]) / jnp.where(lane_lo, den[0], den[1])).astype(BF16)
    y = jnp.dot(o_ref[...], wo_ref[...], preferred_element_type=F32)
    y_ref[...] = x_ref[...] + y + bo_ref[...]


def _swa_prompt(q, kv, x, wo, bo, sinks, batch, n_sub):
    n = x.shape[0]
    tm = n_sub * WINDOW
    steps = n // batch // tm
    nq = SWA_HEADS * SWA_HEAD_DIM
    nkv = kv.shape[1]
    row = lambda b, i: (b * steps + i, 0)
    prev = lambda b, i: ((b * steps + i) * n_sub - jnp.minimum(i, 1), 0)
    return pl.pallas_call(
        functools.partial(_swa_prompt_kernel, n_sub=n_sub),
        grid=(batch, steps),
        in_specs=[
            pl.BlockSpec(memory_space=pltpu.SMEM),
            pl.BlockSpec((tm, nq), row),
            pl.BlockSpec((WINDOW, nkv), prev),
            pl.BlockSpec((tm, nkv), row),
            pl.BlockSpec((tm, D_MODEL), row),
            _resident(wo.shape),
            _resident(bo.shape),
        ],
        out_specs=pl.BlockSpec((tm, D_MODEL), row),
        out_shape=jax.ShapeDtypeStruct((n, D_MODEL), F32),
        scratch_shapes=[pltpu.VMEM((tm, nq), BF16)],
        compiler_params=_params(2),
        name="swa_prompt",
    )(sinks, q, kv, kv, x, wo, bo)


def _swa_sample_kernel(sinks_ref, q_ref, kvn_ref, ck_ref, cv_ref, o_ref, nk_ref, nv_ref, *, nb):
    nk = SWA_KV_HEADS * SWA_HEAD_DIM
    t = DEC_SEQ
    zpad = jnp.zeros((WINDOW - t, nk), F32)
    for g0 in range(0, nb, SWA_SAMPLE_GROUP):
        group = range(g0, g0 + SWA_SAMPLE_GROUP)
        qs, k2s, v2s = [], [], []
        for b in group:
            rows = slice(b * t, (b + 1) * t)
            k_new = kvn_ref[rows, :nk]
            v_new = kvn_ref[rows, nk:]
            ck = ck_ref[b]
            cv = cv_ref[b]
            qs.append(q_ref[rows, :])
            k2s.append(jnp.concatenate([ck, k_new, zpad], axis=0).astype(BF16))
            v2s.append(jnp.concatenate([cv, v_new, zpad], axis=0).astype(BF16))
            nk_ref[b, :WINDOW - t, :] = ck[t:, :]
            nk_ref[b, WINDOW - t:, :] = k_new
            nv_ref[b, :WINDOW - t, :] = cv[t:, :]
            nv_ref[b, WINDOW - t:, :] = v_new
        for b, o in zip(group, _swa_attend(qs, k2s, v2s, True, sinks_ref)):
            o_ref[b * t:(b + 1) * t, :] = o


def _swa_sample(q, kv_new, cache_k, cache_v, sinks, nb):
    batch = cache_k.shape[0]
    nq = SWA_HEADS * SWA_HEAD_DIM
    nk = SWA_KV_HEADS * SWA_HEAD_DIM
    row = lambda b: (b, 0)
    cache = pl.BlockSpec((nb, WINDOW, nk), lambda b: (b, 0, 0))
    return pl.pallas_call(
        functools.partial(_swa_sample_kernel, nb=nb),
        grid=(batch // nb,),
        in_specs=[
            pl.BlockSpec(memory_space=pltpu.SMEM),
            pl.BlockSpec((nb * DEC_SEQ, nq), row),
            pl.BlockSpec((nb * DEC_SEQ, 2 * nk), row),
            cache,
            cache,
        ],
        out_specs=[pl.BlockSpec((nb * DEC_SEQ, nq), row), cache, cache],
        out_shape=[
            jax.ShapeDtypeStruct((batch * DEC_SEQ, nq), F32),
            jax.ShapeDtypeStruct(cache_k.shape, F32),
            jax.ShapeDtypeStruct(cache_v.shape, F32),
        ],
        compiler_params=_params(1),
        name="swa_sample",
    )(sinks, q, kv_new, cache_k, cache_v)


def _rotation_tables(positions, repeat):
    half = RET_DK // 2
    inv = 1.0 / (ROPE_BASE ** (np.arange(half, dtype=np.float64) / half))
    ang = np.asarray(positions, np.float64)[:, None] * inv[None, :]
    cos = np.tile(np.cos(ang), (repeat, 1)).astype(np.float32)
    sin = np.tile(np.sin(ang), (repeat, 1)).astype(np.float32)
    return jnp.asarray(cos), jnp.asarray(sin)


TM_PROJ = 512
TM_SAMPLE = 512
TM_FFN = 1024
FFN_SPLITS = (0, 1536, D_FF)
SWA_SAMPLE_GROUP = 4
TM_SAMPLE_FFN = 512


def kernel(x_prompt, x_sample, state_ret, cache_swa_k, cache_swa_v, ret_w_q, ret_w_k, ret_w_v, ret_w_g, ret_w_o, swa_w_qkv, swa_b_qkv, swa_w_o, swa_b_o, swa_sinks, norm_mix, norm_ffn, ffn_w1, ffn_w3, ffn_w2, norm_final):
    bp, tp, d = x_prompt.shape
    bs, ts, _ = x_sample.shape
    assert (tp, d, ts) == (SEQ, D_MODEL, DEC_SEQ)
    xp = x_prompt.reshape(bp * tp, d)
    xs = x_sample.reshape(bs * ts, d)
    nk = SWA_KV_HEADS * SWA_HEAD_DIM

    cos_p, sin_p = _rotation_tables(np.arange(tp), 1)
    cos_s, sin_s = _rotation_tables(PAST_LEN + np.arange(ts), TM_SAMPLE // ts)
    gamma_final = norm_final.reshape(1, d)
    zero_bias = jnp.zeros((1, d), F32)

    ret_w = [w.astype(BF16) for w in (ret_w_q, ret_w_k, ret_w_v, ret_w_g)]
    w_ro = ret_w_o.astype(BF16)
    g_mix = norm_mix[0].reshape(1, d)
    qs, ks, vs, gs = _ret_proj(xs, g_mix, *ret_w, 0, cos_s, sin_s, TM_SAMPLE, F32)
    ffn_f32 = [w.reshape(-1, w.shape[-1]) for w in (ffn_w1, ffn_w3, ffn_w2)]
    xp, state_p, gated_s, state_s, w1, w3, w2 = _ret_layer(
        xp, g_mix, *ret_w, w_ro, 0, cos_p, sin_p, qs, ks, vs, gs, state_ret[0], bp, RET_PROMPT_CHUNK,
        ffn_f32)
    w1 = w1.reshape(ffn_w1.shape)
    w3 = w3.reshape(ffn_w3.shape)
    w2 = w2.reshape(ffn_w2.shape)
    g_ffn = norm_ffn[0].reshape(1, d)
    xp = _ffn(xp, g_ffn, w1, w3, w2, 0, gamma_final, TM_FFN, False)
    xs = _mixer_out_ffn(gated_s, w_ro[0], zero_bias, xs, g_ffn, w1, w3, w2, 0, gamma_final,
                        TM_SAMPLE_FFN, False)

    nq = SWA_HEADS * SWA_HEAD_DIM
    heads = (SWA_KV_HEADS, SWA_GROUP, SWA_HEAD_DIM)
    w_q = swa_w_qkv[0][:, :nq].reshape(d, *heads).transpose(0, 2, 1, 3).reshape(d, nq).astype(BF16)
    b_q = swa_b_qkv[0][:nq].reshape(heads).transpose(1, 0, 2).reshape(1, nq)
    w_kv = swa_w_qkv[0][:, nq:].astype(BF16)
    b_kv = swa_b_qkv[0][nq:].reshape(1, 2 * nk)
    w_so = swa_w_o[0].reshape(*heads, d).transpose(1, 0, 2, 3).reshape(nq, d).astype(BF16)
    b_so = swa_b_o[0].reshape(1, d)
    sinks = swa_sinks[0]
    g_mix = norm_mix[1].reshape(1, d)
    qp, kvp = _swa_proj(xp, g_mix, w_q, b_q, w_kv, b_kv, 2 * TM_PROJ, BF16)
    xp = _swa_prompt(qp, kvp, xp, w_so, b_so, sinks, bp, 8)
    qs, kvs = _swa_proj(xs, g_mix, w_q, b_q, w_kv, b_kv, TM_SAMPLE, F32)
    att_s, new_k, new_v = _swa_sample(
        qs, kvs, cache_swa_k[0].reshape(bs, WINDOW, nk), cache_swa_v[0].reshape(bs, WINDOW, nk), sinks, 16)

    g_ffn = norm_ffn[1].reshape(1, d)
    yp = _ffn(xp, g_ffn, w1, w3, w2, 1, gamma_final, TM_FFN, True)
    ys = _mixer_out_ffn(att_s, w_so, b_so, xs, g_ffn, w1, w3, w2, 1, gamma_final, TM_SAMPLE_FFN, True)

    kv_tail = kvp.reshape(bp, tp, 2 * nk)[:, tp - WINDOW:, :]
    kv_shape = (1, bp, WINDOW, SWA_KV_HEADS, SWA_HEAD_DIM)
    cache_shape = (1, bs, WINDOW, SWA_KV_HEADS, SWA_HEAD_DIM)
    return (
        yp.reshape(bp, tp, d),
        ys.reshape(bs, ts, d),
        state_p[None],
        state_s[None],
        kv_tail[..., :nk].reshape(kv_shape),
        kv_tail[..., nk:].reshape(kv_shape),
        new_k.reshape(cache_shape),
        new_v.reshape(cache_shape),
    )
```

```python
import functools
import math
from typing import NamedTuple

import numpy as np
import jax
import jax.numpy as jnp
from jax import lax
from jax.experimental import pallas as pl
from jax.experimental.pallas import tpu as pltpu

F32 = jnp.float32
BF16 = jnp.bfloat16

D_MODEL = 1024
SEQ = 2048
DEC_SEQ = 8
PAST_LEN = 16384

RET_HEADS = 4
RET_DK = 256
RET_DV = 512
RET_CHUNK = 128
RET_PROMPT_CHUNK = 256
NORM_ROWS = 64
SAMPLE_KEY_ROWS = 16
ROPE_BASE = 10000.0

SWA_HEADS = 16
SWA_KV_HEADS = 2
SWA_HEAD_DIM = 64
SWA_GROUP = SWA_HEADS // SWA_KV_HEADS
WINDOW = 128

D_FF = 2816
RMS_EPS = 1e-6
GN_EPS = 1e-6
NEG_INF = -1e30

LANES = 128
BF16_SUBLANES = 16
VMEM_LIMIT = 56 * 1024 * 1024


def _params(n_axes):
    return pltpu.CompilerParams(
        dimension_semantics=("arbitrary",) * n_axes, vmem_limit_bytes=VMEM_LIMIT)


def _rms(x, g):
    ms = jnp.mean(x * x, axis=-1, keepdims=True)
    return x * lax.rsqrt(ms + RMS_EPS) * g


def _silu(x):
    return x * jax.nn.sigmoid(x)


def _resident(shape):
    zeros = (0,) * len(shape)
    return pl.BlockSpec(shape, lambda *_: zeros, pipeline_mode=pl.Buffered(1))


def _layer_resident(stacked_shape, layer):
    return pl.BlockSpec((None,) + tuple(stacked_shape[1:]), lambda *_: (layer, 0, 0),
                        pipeline_mode=pl.Buffered(1))


def _ret_proj_kernel(x_ref, g_ref, wq_ref, wk_ref, wv_ref, wg_ref, cos_ref, sin_ref,
                     q_ref, k_ref, v_ref, gs_ref):
    h = _rms(x_ref[...], g_ref[...]).astype(BF16)
    _ret_projections(h, wq_ref, wk_ref, wv_ref, wg_ref, cos_ref, sin_ref, q_ref, k_ref, v_ref, gs_ref)


def _ret_projections(h, wq_ref, wk_ref, wv_ref, wg_ref, cos_ref, sin_ref, q_ref, k_ref, v_ref, gs_ref):
    cos = cos_ref[...]
    sin = sin_ref[...]
    half = RET_DK // 2

    def rotated(w_ref, out_ref, scale):
        y = jnp.dot(h, w_ref[...], preferred_element_type=F32)
        for hd in range(RET_HEADS):
            lo = hd * RET_DK
            a = y[:, lo:lo + half]
            b = y[:, lo + half:lo + RET_DK]
            out_ref[:, lo:lo + half] = ((a * cos - b * sin) * scale).astype(out_ref.dtype)
            out_ref[:, lo + half:lo + RET_DK] = ((a * sin + b * cos) * scale).astype(out_ref.dtype)

    g = jnp.dot(h, wg_ref[...], preferred_element_type=F32)
    gs_ref[...] = _silu(g).astype(gs_ref.dtype)
    v_ref[...] = jnp.dot(h, wv_ref[...], preferred_element_type=F32).astype(v_ref.dtype)
    rotated(wq_ref, q_ref, 1.0)
    rotated(wk_ref, k_ref, RET_DK ** -0.5)


def _ret_proj(x, gamma, wq, wk, wv, wg, layer, cos, sin, tm, out_dtype):
    n = x.shape[0]
    nqk = RET_HEADS * RET_DK
    nv = RET_HEADS * RET_DV
    n_pos_blocks = cos.shape[0] // tm
    row = lambda i: (i, 0)
    return pl.pallas_call(
        _ret_proj_kernel,
        grid=(n // tm,),
        in_specs=[
            pl.BlockSpec((tm, D_MODEL), row),
            _resident((1, D_MODEL)),
            _layer_resident(wq.shape, layer),
            _layer_resident(wk.shape, layer),
            _layer_resident(wv.shape, layer),
            _layer_resident(wg.shape, layer),
            pl.BlockSpec((tm, LANES), lambda i: (i % n_pos_blocks, 0)),
            pl.BlockSpec((tm, LANES), lambda i: (i % n_pos_blocks, 0)),
        ],
        out_specs=[
            pl.BlockSpec((tm, nqk), row),
            pl.BlockSpec((tm, nqk), row),
            pl.BlockSpec((tm, nv), row),
            pl.BlockSpec((tm, nv), row),
        ],
        out_shape=[
            jax.ShapeDtypeStruct((n, nqk), out_dtype),
            jax.ShapeDtypeStruct((n, nqk), out_dtype),
            jax.ShapeDtypeStruct((n, nv), out_dtype),
            jax.ShapeDtypeStruct((n, nv), out_dtype),
        ],
        compiler_params=_params(1),
        name="ret_proj",
    )(x, gamma, wq, wk, wv, wg, cos, sin)


class _RetUnit(NamedTuple):
    qc: jax.Array
    kc: jax.Array
    vc: jax.Array
    gsc: jax.Array
    s_in: object
    s_out: object
    s_bf_in: object
    s_bf_out: object
    chunk: int


class _Bf16View:
    def __init__(self, ref):
        self._ref = ref

    def __getitem__(self, i):
        return self._ref[i].astype(BF16)


def _ret_head(u, hd):
    return (u.qc[:, hd * RET_DK:(hd + 1) * RET_DK], u.kc[:, hd * RET_DK:(hd + 1) * RET_DK],
            u.vc[:, hd * RET_DV:(hd + 1) * RET_DV])


def _ret_scores(u, hd):
    qh, kh, _ = _ret_head(u, hd)
    return lax.dot_general(qh, kh, (((1,), (1,)), ((), ())), preferred_element_type=F32)


def _ret_output(u, hd, scores):
    qh, _, vh = _ret_head(u, hd)
    cq, ck = scores.shape
    log_gamma = math.log1p(-(2.0 ** -(5.0 + hd)))
    ii = lax.broadcasted_iota(jnp.int32, (cq, ck), 0)
    jj = lax.broadcasted_iota(jnp.int32, (cq, ck), 1)
    diff = (ii - jj).astype(F32)
    intra = jnp.where(diff >= 0, jnp.exp(log_gamma * jnp.maximum(diff, 0.0)), 0.0)
    irow = lax.broadcasted_iota(jnp.int32, (cq, 1), 0).astype(F32)
    q_dec = (qh.astype(F32) * jnp.exp(log_gamma * (irow + 1.0))).astype(BF16)
    lhs = jnp.concatenate([q_dec, (scores * intra).astype(BF16)], axis=1)
    rhs = jnp.concatenate([u.s_bf_in[hd], vh], axis=0)
    return jnp.dot(lhs, rhs, preferred_element_type=F32)


def _ret_decayed_keys(u, hd):
    _, kh, _ = _ret_head(u, hd)
    log_gamma = math.log1p(-(2.0 ** -(5.0 + hd)))
    jrow = lax.broadcasted_iota(jnp.int32, (kh.shape[0], 1), 0).astype(F32)
    return kh.astype(F32) * jnp.exp(log_gamma * (u.chunk - 1.0 - jrow))


def _ret_update_state(u, hd, kd_t=None):
    _, _, vh = _ret_head(u, hd)
    log_gamma = math.log1p(-(2.0 ** -(5.0 + hd)))
    if kd_t is None:
        update = lax.dot_general(_ret_decayed_keys(u, hd).astype(BF16), vh, (((0,), (0,)), ((), ())),
                                 preferred_element_type=F32)
    else:
        update = jnp.dot(kd_t, vh, preferred_element_type=F32)
    s_new = u.s_in[hd] * math.exp(log_gamma * u.chunk) + update
    u.s_out[hd] = s_new
    if u.s_bf_out is not None:
        u.s_bf_out[hd] = s_new.astype(BF16)


def _ret_gate(o, gs):
    mu = jnp.mean(o, axis=-1, keepdims=True)
    oc = o - mu
    var = jnp.mean(oc * oc, axis=-1, keepdims=True)
    return oc * lax.rsqrt(var + GN_EPS) * gs.astype(F32)


def _ret_layer_kernel(*refs, chunk, nb, n_cast):
    n_in, n_out = 15, 4
    (x_ref, xn_ref, g_ref, wq_ref, wk_ref, wv_ref, wg_ref, cos_ref, sin_ref, wo_ref,
     qs_ref, ks_ref, vs_ref, gss_ref, s0_ref) = refs[:n_in]
    cast_in = refs[n_in:n_in + n_cast]
    y_ref, sp_ref, gated_s_ref, ss_ref = refs[n_in + n_cast:n_in + n_cast + n_out]
    cast_out = refs[n_in + n_cast + n_out:n_in + 2 * n_cast + n_out]
    (h_scr, q_scr, k_scr, v_scr, gs_scr, kdt_scr, gated_ref, o_scr,
     sp_bf) = refs[n_in + 2 * n_cast + n_out:]
    _ret_layer_body(x_ref, xn_ref, g_ref, wq_ref, wk_ref, wv_ref, wg_ref, cos_ref, sin_ref, wo_ref,
                    qs_ref, ks_ref, vs_ref, gss_ref, s0_ref, y_ref, sp_ref, gated_s_ref, ss_ref,
                    h_scr, q_scr, k_scr, v_scr, gs_scr, kdt_scr, gated_ref, o_scr, sp_bf,
                    chunk=chunk, nb=nb)
    for src_ref, dst_ref in zip(cast_in, cast_out):
        dst_ref[...] = src_ref[...].astype(dst_ref.dtype)


def _ret_layer_body(x_ref, xn_ref, g_ref, wq_ref, wk_ref, wv_ref, wg_ref, cos_ref, sin_ref, wo_ref,
                    qs_ref, ks_ref, vs_ref, gss_ref, s0_ref,
                    y_ref, sp_ref, gated_s_ref, ss_ref,
                    h_scr, q_scr, k_scr, v_scr, gs_scr, kdt_scr, gated_ref, o_scr, sp_bf,
                    *, chunk, nb):
    @pl.when(pl.program_id(1) == 0)
    def _():
        sp_ref[...] = jnp.zeros_like(sp_ref)
        sp_bf[...] = jnp.zeros_like(sp_bf)

    @pl.when(jnp.logical_and(pl.program_id(0) == 0, pl.program_id(1) == 0))
    def _():
        h_scr[...] = _rms(x_ref[...], g_ref[...]).astype(BF16)

    heads = range(RET_HEADS)
    _ret_projections(h_scr[...], wq_ref, wk_ref, wv_ref, wg_ref, cos_ref, sin_ref,
                     q_scr, k_scr, v_scr, gs_scr)
    prompt = _RetUnit(q_scr[...], k_scr[...], v_scr[...], gs_scr[...],
                      sp_ref.at[0], sp_ref.at[0], sp_bf, sp_bf, chunk)
    for hd in heads:
        kdt_scr[hd] = _ret_decayed_keys(prompt, hd).T.astype(BF16)

    pad = SAMPLE_KEY_ROWS - DEC_SEQ
    kpad = jnp.zeros((pad, ks_ref.shape[1]), F32)
    vpad = jnp.zeros((pad, vs_ref.shape[1]), F32)
    samples = []
    for b in range(nb):
        rows = slice(b * DEC_SEQ, (b + 1) * DEC_SEQ)
        samples.append(_RetUnit(
            qs_ref[rows, :].astype(BF16),
            jnp.concatenate([ks_ref[rows, :], kpad], axis=0).astype(BF16),
            jnp.concatenate([vs_ref[rows, :], vpad], axis=0).astype(BF16),
            gss_ref[rows, :], s0_ref.at[b], ss_ref.at[b], _Bf16View(s0_ref.at[b]), None, DEC_SEQ))
    scores_s = [[_ret_scores(u, hd) for hd in heads] for u in samples]
    for u in samples:
        for hd in heads:
            _ret_update_state(u, hd)

    scores_p = [_ret_scores(prompt, hd) for hd in heads]
    o_s = [[_ret_output(u, hd, sc[hd]) for hd in heads] for u, sc in zip(samples, scores_s)]
    for hd in heads:
        o_scr[hd] = _ret_output(prompt, hd, scores_p[hd])
    for hd in heads:
        _ret_update_state(prompt, hd, kdt_scr[hd])

    for hd in heads:
        cols = slice(hd * RET_DV, (hd + 1) * RET_DV)
        for r0 in range(0, chunk, NORM_ROWS):
            rows = slice(r0, r0 + NORM_ROWS)
            gated_ref[rows, cols] = _ret_gate(o_scr[hd, rows, :], gs_scr[rows, cols]).astype(BF16)
        for b, u in enumerate(samples):
            rows = slice(b * DEC_SEQ, (b + 1) * DEC_SEQ)
            gated_s_ref[rows, cols] = _ret_gate(o_s[b][hd], u.gsc[:, cols])
    y_ref[...] = x_ref[...] + jnp.dot(gated_ref[...], wo_ref[...], preferred_element_type=F32)
    h_scr[...] = _rms(xn_ref[...], g_ref[...]).astype(BF16)


def _cast_slab_spec(rows, cols, n_steps):
    for span in (1, 2, 4):
        slab, rem = divmod(rows * span, n_steps)
        if rem == 0 and slab % BF16_SUBLANES == 0:
            return slab, span
    raise ValueError(f"no slab tiling for {rows} rows over {n_steps} steps")


def _ret_layer(x, gamma, wq, wk, wv, wg, wo, layer, cos, sin, qs, ks, vs, gss, s0, batch, chunk, to_cast):
    n = x.shape[0]
    steps = n // batch // chunk
    cast_specs = []
    for w in to_cast:
        slab, span = _cast_slab_spec(w.shape[0], w.shape[1], batch * steps)
        cast_specs.append(pl.BlockSpec(
            (slab, w.shape[1]), lambda b, c, span=span: ((b * steps + c) // span, 0)))
    assert cos.shape[0] == steps * chunk
    nb = s0.shape[0] // (batch * steps)
    assert nb * batch * steps == s0.shape[0]
    nqk = RET_HEADS * RET_DK
    nv = RET_HEADS * RET_DV
    ts = nb * DEC_SEQ
    row = lambda b, c: (b * steps + c, 0)
    next_row = lambda b, c: (jnp.minimum(b * steps + c + 1, batch * steps - 1), 0)
    pos = lambda b, c: (c, 0)
    state_s = pl.BlockSpec((nb, RET_HEADS, RET_DK, RET_DV), lambda b, c: (b * steps + c, 0, 0, 0))
    return pl.pallas_call(
        functools.partial(_ret_layer_kernel, chunk=chunk, nb=nb, n_cast=len(to_cast)),
        grid=(batch, steps),
        in_specs=[
            pl.BlockSpec((chunk, D_MODEL), row),
            pl.BlockSpec((chunk, D_MODEL), next_row),
            _resident((1, D_MODEL)),
            _layer_resident(wq.shape, layer),
            _layer_resident(wk.shape, layer),
            _layer_resident(wv.shape, layer),
            _layer_resident(wg.shape, layer),
            pl.BlockSpec((chunk, LANES), pos),
            pl.BlockSpec((chunk, LANES), pos),
            _layer_resident(wo.shape, layer),
            pl.BlockSpec((ts, nqk), row),
            pl.BlockSpec((ts, nqk), row),
            pl.BlockSpec((ts, nv), row),
            pl.BlockSpec((ts, nv), row),
            state_s,
        ] + cast_specs,
        out_specs=[
            pl.BlockSpec((chunk, D_MODEL), row),
            pl.BlockSpec((1, RET_HEADS, RET_DK, RET_DV), lambda b, c: (b, 0, 0, 0)),
            pl.BlockSpec((ts, nv), row),
            state_s,
        ] + cast_specs,
        out_shape=[
            jax.ShapeDtypeStruct((n, D_MODEL), F32),
            jax.ShapeDtypeStruct((batch, RET_HEADS, RET_DK, RET_DV), F32),
            jax.ShapeDtypeStruct((s0.shape[0] * DEC_SEQ, nv), F32),
            jax.ShapeDtypeStruct(s0.shape, F32),
        ] + [jax.ShapeDtypeStruct(w.shape, BF16) for w in to_cast],
        scratch_shapes=[
            pltpu.VMEM((chunk, D_MODEL), BF16),
            pltpu.VMEM((chunk, nqk), BF16),
            pltpu.VMEM((chunk, nqk), BF16),
            pltpu.VMEM((chunk, nv), BF16),
            pltpu.VMEM((chunk, nv), BF16),
            pltpu.VMEM((RET_HEADS, RET_DK, chunk), BF16),
            pltpu.VMEM((chunk, nv), BF16),
            pltpu.VMEM((RET_HEADS, chunk, RET_DV), F32),
            pltpu.VMEM((RET_HEADS, RET_DK, RET_DV), BF16),
        ],
        compiler_params=_params(2),
        name="ret_layer",
    )(x, x, gamma, wq, wk, wv, wg, cos, sin, wo, qs, ks, vs, gss, s0, *to_cast)


def _ffn_block(x, g_ref, w1_ref, w3_ref, w2_ref, gf_ref, final_norm):
    h = _rms(x, g_ref[...]).astype(BF16)
    y = x
    for lo, hi in zip(FFN_SPLITS[:-1], FFN_SPLITS[1:]):
        cols = slice(lo, hi)
        a = jnp.dot(h, w1_ref[:, cols], preferred_element_type=F32)
        b = jnp.dot(h, w3_ref[:, cols], preferred_element_type=F32)
        act = (_silu(a) * b).astype(BF16)
        y = y + jnp.dot(act, w2_ref[cols, :], preferred_element_type=F32)
    return _rms(y, gf_ref[...]) if final_norm else y


def _ffn_kernel(x_ref, g_ref, w1_ref, w3_ref, w2_ref, gf_ref, y_ref, *, final_norm):
    y_ref[...] = _ffn_block(x_ref[...], g_ref, w1_ref, w3_ref, w2_ref, gf_ref, final_norm)


def _mixer_out_ffn_kernel(a_ref, wl_ref, bl_ref, res_ref, g_ref, w1_ref, w3_ref, w2_ref, gf_ref, y_ref,
                          *, final_norm):
    x = res_ref[...] + jnp.dot(a_ref[...].astype(BF16), wl_ref[...],
                               preferred_element_type=F32) + bl_ref[...]
    y_ref[...] = _ffn_block(x, g_ref, w1_ref, w3_ref, w2_ref, gf_ref, final_norm)


def _mixer_out_ffn(a, wl, bl, res, gamma, w1, w3, w2, layer, gamma_final, tm, final_norm):
    n, kdim = a.shape
    row = lambda i: (i, 0)
    return pl.pallas_call(
        functools.partial(_mixer_out_ffn_kernel, final_norm=final_norm),
        grid=(n // tm,),
        in_specs=[
            pl.BlockSpec((tm, kdim), row),
            _resident(wl.shape),
            _resident(bl.shape),
            pl.BlockSpec((tm, D_MODEL), row),
            _resident((1, D_MODEL)),
            _layer_resident(w1.shape, layer),
            _layer_resident(w3.shape, layer),
            _layer_resident(w2.shape, layer),
            _resident((1, D_MODEL)),
        ],
        out_specs=pl.BlockSpec((tm, D_MODEL), row),
        out_shape=jax.ShapeDtypeStruct((n, D_MODEL), F32),
        compiler_params=_params(1),
        name="mixer_out_ffn",
    )(a, wl, bl, res, gamma, w1, w3, w2, gamma_final)


def _ffn(x, gamma, w1, w3, w2, layer, gamma_final, tm, final_norm):
    n = x.shape[0]
    row = lambda i: (i, 0)
    return pl.pallas_call(
        functools.partial(_ffn_kernel, final_norm=final_norm),
        grid=(n // tm,),
        in_specs=[
            pl.BlockSpec((tm, D_MODEL), row),
            _resident((1, D_MODEL)),
            _layer_resident(w1.shape, layer),
            _layer_resident(w3.shape, layer),
            _layer_resident(w2.shape, layer),
            _resident((1, D_MODEL)),
        ],
        out_specs=pl.BlockSpec((tm, D_MODEL), row),
        out_shape=jax.ShapeDtypeStruct((n, D_MODEL), F32),
        compiler_params=_params(1),
        name="ffn",
    )(x, gamma, w1, w3, w2, gamma_final)


def _swa_proj_kernel(x_ref, g_ref, wq_ref, bq_ref, wkv_ref, bkv_ref, q_ref, kv_ref):
    h = _rms(x_ref[...], g_ref[...]).astype(BF16)
    q = jnp.dot(h, wq_ref[...], preferred_element_type=F32) + bq_ref[...]
    q_ref[...] = (q * (SWA_HEAD_DIM ** -0.5)).astype(q_ref.dtype)
    kv_ref[...] = jnp.dot(h, wkv_ref[...], preferred_element_type=F32) + bkv_ref[...]


def _swa_proj(x, gamma, wq, bq, wkv, bkv, tm, q_dtype):
    n = x.shape[0]
    nq = SWA_HEADS * SWA_HEAD_DIM
    nkv = 2 * SWA_KV_HEADS * SWA_HEAD_DIM
    row = lambda i: (i, 0)
    return pl.pallas_call(
        _swa_proj_kernel,
        grid=(n // tm,),
        in_specs=[
            pl.BlockSpec((tm, D_MODEL), row),
            _resident((1, D_MODEL)),
            _resident(wq.shape),
            _resident(bq.shape),
            _resident(wkv.shape),
            _resident(bkv.shape),
        ],
        out_specs=[pl.BlockSpec((tm, nq), row), pl.BlockSpec((tm, nkv), row)],
        out_shape=[
            jax.ShapeDtypeStruct((n, nq), q_dtype),
            jax.ShapeDtypeStruct((n, nkv), F32),
        ],
        compiler_params=_params(1),
        name="swa_proj",
    )(x, gamma, wq, bq, wkv, bkv)


def _swa_attend(qs, k2s, v2s, prev_valid, sinks_ref):
    r = qs[0].shape[0]
    hd = SWA_HEAD_DIM
    lane_lo = lax.broadcasted_iota(jnp.int32, (r, LANES), 1) < hd
    zero = jnp.zeros((r, LANES), qs[0].dtype)
    sink_pieces = []
    for j in range(SWA_GROUP):
        sink_pieces.append(jnp.full((r, 1), sinks_ref[j], F32))
        sink_pieces.append(jnp.full((r, 1), sinks_ref[SWA_GROUP + j], F32))
    sink = jnp.concatenate(sink_pieces * len(qs), axis=0)
    rows = SWA_HEADS * r

    s2_parts = []
    for q, k2 in zip(qs, k2s):
        pieces = []
        for j in range(SWA_GROUP):
            slab = q[:, j * LANES:(j + 1) * LANES]
            pieces.append(jnp.where(lane_lo, slab, zero))
            pieces.append(jnp.where(lane_lo, zero, slab))
        stacked = jnp.concatenate(pieces, axis=0).astype(BF16)
        s2_parts.append(lax.dot_general(stacked, k2, (((1,), (1,)), ((), ())),
                                        preferred_element_type=F32))
    s2 = jnp.concatenate(s2_parts, axis=0)
    total = rows * len(qs)
    qi = lax.broadcasted_iota(jnp.int32, (total, WINDOW), 0) & (r - 1)
    kj = lax.broadcasted_iota(jnp.int32, (total, WINDOW), 1)
    cur = kj <= qi
    prev_cap = jnp.where(prev_valid, jnp.float32(3.0e38), jnp.float32(NEG_INF))
    s = jnp.where(cur, s2[:, WINDOW:], jnp.minimum(s2[:, :WINDOW], prev_cap))
    m = jnp.maximum(jnp.max(s, axis=-1, keepdims=True), sink)
    p = jnp.exp(s - m)
    den = jnp.sum(p, axis=-1, keepdims=True) + jnp.exp(sink - m)
    p2 = jnp.concatenate([jnp.where(cur, 0.0, p), jnp.where(cur, p, 0.0)], axis=1).astype(BF16)
    results = []
    for e, v2 in enumerate(v2s):
        blk = slice(e * rows, (e + 1) * rows)
        o = jnp.dot(p2[blk], v2, preferred_element_type=F32) / den[blk]
        outs = [jnp.where(lane_lo, o[(2 * j) * r:(2 * j + 1) * r], o[(2 * j + 1) * r:(2 * j + 2) * r])
                for j in range(SWA_GROUP)]
        results.append(jnp.concatenate(outs, axis=1))
    return results


def _swa_prompt_kernel(sinks_ref, q_ref, kvp_ref, kvc_ref, x_ref, wo_ref, bo_ref, y_ref, o_ref,
                       *, n_sub):
    nk = SWA_KV_HEADS * SWA_HEAD_DIM
    r = WINDOW
    lane_lo = lax.broadcasted_iota(jnp.int32, (r, LANES), 1) < SWA_HEAD_DIM
    qi = lax.broadcasted_iota(jnp.int32, (r, WINDOW), 0)
    kj = lax.broadcasted_iota(jnp.int32, (r, WINDOW), 1)
    cur_mask = kj <= qi
    ones = jnp.ones((2 * WINDOW, LANES), BF16)
    nt = (((1,), (1,)), ((), ()))
    for c in range(n_sub):
        rows = slice(c * WINDOW, (c + 1) * WINDOW)
        prev = kvp_ref[...] if c == 0 else kvc_ref[(c - 1) * WINDOW:c * WINDOW, :]
        cur = kvc_ref[rows, :]
        k2 = jnp.concatenate([prev[:, :nk], cur[:, :nk]], axis=0).astype(BF16)
        v2 = jnp.concatenate([prev[:, nk:], cur[:, nk:]], axis=0).astype(BF16)
        v2_ones = jnp.concatenate([v2, ones], axis=1)
        prev_valid = (pl.program_id(1) > 0) if c == 0 else True
        prev_cap = jnp.where(prev_valid, jnp.float32(3.0e38), jnp.float32(NEG_INF))
        for j in range(SWA_GROUP):
            slab = q_ref[rows, j * LANES:(j + 1) * LANES]
            zero = jnp.zeros_like(slab)
            num = []
            den = []
            for side in range(2):
                qm = jnp.where(lane_lo, slab, zero) if side == 0 else jnp.where(lane_lo, zero, slab)
                s2 = lax.dot_general(qm, k2, nt, preferred_element_type=F32)
                s = jnp.where(cur_mask, s2[:, WINDOW:], jnp.minimum(s2[:, :WINDOW], prev_cap))
                sink = sinks_ref[j + side * SWA_GROUP]
                m = jnp.maximum(jnp.max(s, axis=-1, keepdims=True), sink)
                p = jnp.exp(s - m)
                p2 = jnp.concatenate([jnp.where(cur_mask, 0.0, p), jnp.where(cur_mask, p, 0.0)],
                                     axis=1).astype(BF16)
                o2 = jnp.dot(p2, v2_ones, preferred_element_type=F32)
                num.append(o2[:, :LANES])
                den.append(o2[:, LANES:] + jnp.exp(sink - m))
            o_ref[rows, j * LANES:(j + 1) * LANES] = (
                jnp.where(lane_lo, num[0], num[1]) / jnp.where(lane_lo, den[0], den[1])).astype(BF16)
    y = jnp.dot(o_ref[...], wo_ref[...], preferred_element_type=F32)
    y_ref[...] = x_ref[...] + y + bo_ref[...]


def _swa_prompt(q, kv, x, wo, bo, sinks, batch, n_sub):
    n = x.shape[0]
    tm = n_sub * WINDOW
    steps = n // batch // tm
    nq = SWA_HEADS * SWA_HEAD_DIM
    nkv = kv.shape[1]
    row = lambda b, i: (b * steps + i, 0)
    prev = lambda b, i: ((b * steps + i) * n_sub - jnp.minimum(i, 1), 0)
    return pl.pallas_call(
        functools.partial(_swa_prompt_kernel, n_sub=n_sub),
        grid=(batch, steps),
        in_specs=[
            pl.BlockSpec(memory_space=pltpu.SMEM),
            pl.BlockSpec((tm, nq), row),
            pl.BlockSpec((WINDOW, nkv), prev),
            pl.BlockSpec((tm, nkv), row),
            pl.BlockSpec((tm, D_MODEL), row),
            _resident(wo.shape),
            _resident(bo.shape),
        ],
        out_specs=pl.BlockSpec((tm, D_MODEL), row),
        out_shape=jax.ShapeDtypeStruct((n, D_MODEL), F32),
        scratch_shapes=[pltpu.VMEM((tm, nq), BF16)],
        compiler_params=_params(2),
        name="swa_prompt",
    )(sinks, q, kv, kv, x, wo, bo)


def _swa_sample_kernel(sinks_ref, q_ref, kvn_ref, ck_ref, cv_ref, o_ref, nk_ref, nv_ref, *, nb):
    nk = SWA_KV_HEADS * SWA_HEAD_DIM
    t = DEC_SEQ
    zpad = jnp.zeros((WINDOW - t, nk), F32)
    for g0 in range(0, nb, SWA_SAMPLE_GROUP):
        group = range(g0, g0 + SWA_SAMPLE_GROUP)
        qs, k2s, v2s = [], [], []
        for b in group:
            rows = slice(b * t, (b + 1) * t)
            k_new = kvn_ref[rows, :nk]
            v_new = kvn_ref[rows, nk:]
            ck = ck_ref[b]
            cv = cv_ref[b]
            qs.append(q_ref[rows, :])
            k2s.append(jnp.concatenate([ck, k_new, zpad], axis=0).astype(BF16))
            v2s.append(jnp.concatenate([cv, v_new, zpad], axis=0).astype(BF16))
            nk_ref[b, :WINDOW - t, :] = ck[t:, :]
            nk_ref[b, WINDOW - t:, :] = k_new
            nv_ref[b, :WINDOW - t, :] = cv[t:, :]
            nv_ref[b, WINDOW - t:, :] = v_new
        for b, o in zip(group, _swa_attend(qs, k2s, v2s, True, sinks_ref)):
            o_ref[b * t:(b + 1) * t, :] = o


def _swa_sample(q, kv_new, cache_k, cache_v, sinks, nb):
    batch = cache_k.shape[0]
    nq = SWA_HEADS * SWA_HEAD_DIM
    nk = SWA_KV_HEADS * SWA_HEAD_DIM
    row = lambda b: (b, 0)
    cache = pl.BlockSpec((nb, WINDOW, nk), lambda b: (b, 0, 0))
    return pl.pallas_call(
        functools.partial(_swa_sample_kernel, nb=nb),
        grid=(batch // nb,),
        in_specs=[
            pl.BlockSpec(memory_space=pltpu.SMEM),
            pl.BlockSpec((nb * DEC_SEQ, nq), row),
            pl.BlockSpec((nb * DEC_SEQ, 2 * nk), row),
            cache,
            cache,
        ],
        out_specs=[pl.BlockSpec((nb * DEC_SEQ, nq), row), cache, cache],
        out_shape=[
            jax.ShapeDtypeStruct((batch * DEC_SEQ, nq), F32),
            jax.ShapeDtypeStruct(cache_k.shape, F32),
            jax.ShapeDtypeStruct(cache_v.shape, F32),
        ],
        compiler_params=_params(1),
        name="swa_sample",
    )(sinks, q, kv_new, cache_k, cache_v)


def _rotation_tables(positions, repeat):
    half = RET_DK // 2
    inv = 1.0 / (ROPE_BASE ** (np.arange(half, dtype=np.float64) / half))
    ang = np.asarray(positions, np.float64)[:, None] * inv[None, :]
    cos = np.tile(np.cos(ang), (repeat, 1)).astype(np.float32)
    sin = np.tile(np.sin(ang), (repeat, 1)).astype(np.float32)
    return jnp.asarray(cos), jnp.asarray(sin)


TM_PROJ = 512
TM_SAMPLE = 512
TM_FFN = 1024
FFN_SPLITS = (0, 1536, D_FF)
SWA_SAMPLE_GROUP = 4
TM_SAMPLE_FFN = 512


def kernel(x_prompt, x_sample, state_ret, cache_swa_k, cache_swa_v, ret_w_q, ret_w_k, ret_w_v, ret_w_g, ret_w_o, swa_w_qkv, swa_b_qkv, swa_w_o, swa_b_o, swa_sinks, norm_mix, norm_ffn, ffn_w1, ffn_w3, ffn_w2, norm_final):
    bp, tp, d = x_prompt.shape
    bs, ts, _ = x_sample.shape
    assert (tp, d, ts) == (SEQ, D_MODEL, DEC_SEQ)
    xp = x_prompt.reshape(bp * tp, d)
    xs = x_sample.reshape(bs * ts, d)
    nk = SWA_KV_HEADS * SWA_HEAD_DIM

    cos_p, sin_p = _rotation_tables(np.arange(tp), 1)
    cos_s, sin_s = _rotation_tables(PAST_LEN + np.arange(ts), TM_SAMPLE // ts)
    gamma_final = norm_final.reshape(1, d)
    zero_bias = jnp.zeros((1, d), F32)

    ret_w = [w.astype(BF16) for w in (ret_w_q, ret_w_k, ret_w_v, ret_w_g)]
    w_ro = ret_w_o.astype(BF16)
    g_mix = norm_mix[0].reshape(1, d)
    qs, ks, vs, gs = _ret_proj(xs, g_mix, *ret_w, 0, cos_s, sin_s, TM_SAMPLE, F32)
    ffn_f32 = [w.reshape(-1, w.shape[-1]) for w in (ffn_w1, ffn_w3, ffn_w2)]
    xp, state_p, gated_s, state_s, w1, w3, w2 = _ret_layer(
        xp, g_mix, *ret_w, w_ro, 0, cos_p, sin_p, qs, ks, vs, gs, state_ret[0], bp, RET_PROMPT_CHUNK,
        ffn_f32)
    w1 = w1.reshape(ffn_w1.shape)
    w3 = w3.reshape(ffn_w3.shape)
    w2 = w2.reshape(ffn_w2.shape)
    g_ffn = norm_ffn[0].reshape(1, d)
    xp = _ffn(xp, g_ffn, w1, w3, w2, 0, gamma_final, TM_FFN, False)
    xs = _mixer_out_ffn(gated_s, w_ro[0], zero_bias, xs, g_ffn, w1, w3, w2, 0, gamma_final,
                        TM_SAMPLE_FFN, False)

    nq = SWA_HEADS * SWA_HEAD_DIM
    heads = (SWA_KV_HEADS, SWA_GROUP, SWA_HEAD_DIM)
    w_q = swa_w_qkv[0][:, :nq].reshape(d, *heads).transpose(0, 2, 1, 3).reshape(d, nq).astype(BF16)
    b_q = swa_b_qkv[0][:nq].reshape(heads).transpose(1, 0, 2).reshape(1, nq)
    w_kv = swa_w_qkv[0][:, nq:].astype(BF16)
    b_kv = swa_b_qkv[0][nq:].reshape(1, 2 * nk)
    w_so = swa_w_o[0].reshape(*heads, d).transpose(1, 0, 2, 3).reshape(nq, d).astype(BF16)
    b_so = swa_b_o[0].reshape(1, d)
    sinks = swa_sinks[0]
    g_mix = norm_mix[1].reshape(1, d)
    qp, kvp = _swa_proj(xp, g_mix, w_q, b_q, w_kv, b_kv, 2 * TM_PROJ, BF16)
    xp = _swa_prompt(qp, kvp, xp, w_so, b_so, sinks, bp, 8)
    qs, kvs = _swa_proj(xs, g_mix, w_q, b_q, w_kv, b_kv, TM_SAMPLE, F32)
    att_s, new_k, new_v = _swa_sample(
        qs, kvs, cache_swa_k[0].reshape(bs, WINDOW, nk), cache_swa_v[0].reshape(bs, WINDOW, nk), sinks, 16)

    g_ffn = norm_ffn[1].reshape(1, d)
    yp = _ffn(xp, g_ffn, w1, w3, w2, 1, gamma_final, TM_FFN, True)
    ys = _mixer_out_ffn(att_s, w_so, b_so, xs, g_ffn, w1, w3, w2, 1, gamma_final, TM_SAMPLE_FFN, True)

    kv_tail = kvp.reshape(bp, tp, 2 * nk)[:, tp - WINDOW:, :]
    kv_shape = (1, bp, WINDOW, SWA_KV_HEADS, SWA_HEAD_DIM)
    cache_shape = (1, bs, WINDOW, SWA_KV_HEADS, SWA_HEAD_DIM)
    return (
        yp.reshape(bp, tp, d),
        ys.reshape(bs, ts, d),
        state_p[None],
        state_s[None],
        kv_tail[..., :nk].reshape(kv_shape),
        kv_tail[..., nk:].reshape(kv_shape),
        new_k.reshape(cache_shape),
        new_v.reshape(cache_shape),
    )
```

```python
import functools
import math
from typing import NamedTuple

import numpy as np
import jax
import jax.numpy as jnp
from jax import lax
from jax.experimental import pallas as pl
from jax.experimental.pallas import tpu as pltpu

F32 = jnp.float32
BF16 = jnp.bfloat16

D_MODEL = 1024
SEQ = 2048
DEC_SEQ = 8
PAST_LEN = 16384

RET_HEADS = 4
RET_DK = 256
RET_DV = 512
RET_CHUNK = 128
RET_PROMPT_CHUNK = 256
NORM_ROWS = 64
SAMPLE_KEY_ROWS = 16
ROPE_BASE = 10000.0

SWA_HEADS = 16
SWA_KV_HEADS = 2
SWA_HEAD_DIM = 64
SWA_GROUP = SWA_HEADS // SWA_KV_HEADS
WINDOW = 128

D_FF = 2816
RMS_EPS = 1e-6
GN_EPS = 1e-6
NEG_INF = -1e30

LANES = 128
BF16_SUBLANES = 16
VMEM_LIMIT = 56 * 1024 * 1024


def _params(n_axes):
    return pltpu.CompilerParams(
        dimension_semantics=("arbitrary",) * n_axes, vmem_limit_bytes=VMEM_LIMIT)


def _rms(x, g):
    ms = jnp.mean(x * x, axis=-1, keepdims=True)
    return x * lax.rsqrt(ms + RMS_EPS) * g


def _silu(x):
    return x * jax.nn.sigmoid(x)


def _resident(shape):
    zeros = (0,) * len(shape)
    return pl.BlockSpec(shape, lambda *_: zeros, pipeline_mode=pl.Buffered(1))


def _layer_resident(stacked_shape, layer):
    return pl.BlockSpec((None,) + tuple(stacked_shape[1:]), lambda *_: (layer, 0, 0),
                        pipeline_mode=pl.Buffered(1))


def _ret_proj_kernel(x_ref, g_ref, wq_ref, wk_ref, wv_ref, wg_ref, cos_ref, sin_ref,
                     q_ref, k_ref, v_ref, gs_ref):
    h = _rms(x_ref[...], g_ref[...]).astype(BF16)
    _ret_projections(h, wq_ref, wk_ref, wv_ref, wg_ref, cos_ref, sin_ref, q_ref, k_ref, v_ref, gs_ref)


def _ret_projections(h, wq_ref, wk_ref, wv_ref, wg_ref, cos_ref, sin_ref, q_ref, k_ref, v_ref, gs_ref):
    cos = cos_ref[...]
    sin = sin_ref[...]
    half = RET_DK // 2

    def rotated(w_ref, out_ref, scale):
        y = jnp.dot(h, w_ref[...], preferred_element_type=F32)
        for hd in range(RET_HEADS):
            lo = hd * RET_DK
            a = y[:, lo:lo + half]
            b = y[:, lo + half:lo + RET_DK]
            out_ref[:, lo:lo + half] = ((a * cos - b * sin) * scale).astype(out_ref.dtype)
            out_ref[:, lo + half:lo + RET_DK] = ((a * sin + b * cos) * scale).astype(out_ref.dtype)

    g = jnp.dot(h, wg_ref[...], preferred_element_type=F32)
    gs_ref[...] = _silu(g).astype(gs_ref.dtype)
    v_ref[...] = jnp.dot(h, wv_ref[...], preferred_element_type=F32).astype(v_ref.dtype)
    rotated(wq_ref, q_ref, 1.0)
    rotated(wk_ref, k_ref, RET_DK ** -0.5)


def _ret_proj(x, gamma, wq, wk, wv, wg, layer, cos, sin, tm, out_dtype):
    n = x.shape[0]
    nqk = RET_HEADS * RET_DK
    nv = RET_HEADS * RET_DV
    n_pos_blocks = cos.shape[0] // tm
    row = lambda i: (i, 0)
    return pl.pallas_call(
        _ret_proj_kernel,
        grid=(n // tm,),
        in_specs=[
            pl.BlockSpec((tm, D_MODEL), row),
            _resident((1, D_MODEL)),
            _layer_resident(wq.shape, layer),
            _layer_resident(wk.shape, layer),
            _layer_resident(wv.shape, layer),
            _layer_resident(wg.shape, layer),
            pl.BlockSpec((tm, LANES), lambda i: (i % n_pos_blocks, 0)),
            pl.BlockSpec((tm, LANES), lambda i: (i % n_pos_blocks, 0)),
        ],
        out_specs=[
            pl.BlockSpec((tm, nqk), row),
            pl.BlockSpec((tm, nqk), row),
            pl.BlockSpec((tm, nv), row),
            pl.BlockSpec((tm, nv), row),
        ],
        out_shape=[
            jax.ShapeDtypeStruct((n, nqk), out_dtype),
            jax.ShapeDtypeStruct((n, nqk), out_dtype),
            jax.ShapeDtypeStruct((n, nv), out_dtype),
            jax.ShapeDtypeStruct((n, nv), out_dtype),
        ],
        compiler_params=_params(1),
        name="ret_proj",
    )(x, gamma, wq, wk, wv, wg, cos, sin)


class _RetUnit(NamedTuple):
    qc: jax.Array
    kc: jax.Array
    vc: jax.Array
    gsc: jax.Array
    s_in: object
    s_out: object
    s_bf_in: object
    s_bf_out: object
    chunk: int


def _ret_head(u, hd):
    return (u.qc[:, hd * RET_DK:(hd + 1) * RET_DK], u.kc[:, hd * RET_DK:(hd + 1) * RET_DK],
            u.vc[:, hd * RET_DV:(hd + 1) * RET_DV])


def _ret_scores(u, hd):
    qh, kh, _ = _ret_head(u, hd)
    return lax.dot_general(qh, kh, (((1,), (1,)), ((), ())), preferred_element_type=F32)


def _ret_output(u, hd, scores):
    qh, _, vh = _ret_head(u, hd)
    cq, ck = scores.shape
    log_gamma = math.log1p(-(2.0 ** -(5.0 + hd)))
    ii = lax.broadcasted_iota(jnp.int32, (cq, ck), 0)
    jj = lax.broadcasted_iota(jnp.int32, (cq, ck), 1)
    diff = (ii - jj).astype(F32)
    intra = jnp.where(diff >= 0, jnp.exp(log_gamma * jnp.maximum(diff, 0.0)), 0.0)
    irow = lax.broadcasted_iota(jnp.int32, (cq, 1), 0).astype(F32)
    q_dec = (qh.astype(F32) * jnp.exp(log_gamma * (irow + 1.0))).astype(BF16)
    lhs = jnp.concatenate([q_dec, (scores * intra).astype(BF16)], axis=1)
    rhs = jnp.concatenate([u.s_bf_in[hd], vh], axis=0)
    return jnp.dot(lhs, rhs, preferred_element_type=F32)


def _ret_decayed_keys(u, hd):
    _, kh, _ = _ret_head(u, hd)
    log_gamma = math.log1p(-(2.0 ** -(5.0 + hd)))
    jrow = lax.broadcasted_iota(jnp.int32, (kh.shape[0], 1), 0).astype(F32)
    return kh.astype(F32) * jnp.exp(log_gamma * (u.chunk - 1.0 - jrow))


def _ret_update_state(u, hd, kd_t=None):
    _, _, vh = _ret_head(u, hd)
    log_gamma = math.log1p(-(2.0 ** -(5.0 + hd)))
    if kd_t is None:
        update = lax.dot_general(_ret_decayed_keys(u, hd).astype(BF16), vh, (((0,), (0,)), ((), ())),
                                 preferred_element_type=F32)
    else:
        update = jnp.dot(kd_t, vh, preferred_element_type=F32)
    s_new = u.s_in[hd] * math.exp(log_gamma * u.chunk) + update
    u.s_out[hd] = s_new
    if u.s_bf_out is not None:
        u.s_bf_out[hd] = s_new.astype(BF16)


def _ret_gate(o, gs):
    mu = jnp.mean(o, axis=-1, keepdims=True)
    oc = o - mu
    var = jnp.mean(oc * oc, axis=-1, keepdims=True)
    return oc * lax.rsqrt(var + GN_EPS) * gs.astype(F32)


def _ret_layer_kernel(*refs, chunk, nb, n_cast):
    n_in, n_out = 15, 4
    (x_ref, xn_ref, g_ref, wq_ref, wk_ref, wv_ref, wg_ref, cos_ref, sin_ref, wo_ref,
     qs_ref, ks_ref, vs_ref, gss_ref, s0_ref) = refs[:n_in]
    cast_in = refs[n_in:n_in + n_cast]
    y_ref, sp_ref, gated_s_ref, ss_ref = refs[n_in + n_cast:n_in + n_cast + n_out]
    cast_out = refs[n_in + n_cast + n_out:n_in + 2 * n_cast + n_out]
    (h_scr, q_scr, k_scr, v_scr, gs_scr, kdt_scr, gated_ref, o_scr, sp_bf,
     ss_bf) = refs[n_in + 2 * n_cast + n_out:]
    _ret_layer_body(x_ref, xn_ref, g_ref, wq_ref, wk_ref, wv_ref, wg_ref, cos_ref, sin_ref, wo_ref,
                    qs_ref, ks_ref, vs_ref, gss_ref, s0_ref, y_ref, sp_ref, gated_s_ref, ss_ref,
                    h_scr, q_scr, k_scr, v_scr, gs_scr, kdt_scr, gated_ref, o_scr, sp_bf, ss_bf,
                    chunk=chunk, nb=nb)
    for src_ref, dst_ref in zip(cast_in, cast_out):
        dst_ref[...] = src_ref[...].astype(dst_ref.dtype)


def _ret_layer_body(x_ref, xn_ref, g_ref, wq_ref, wk_ref, wv_ref, wg_ref, cos_ref, sin_ref, wo_ref,
                    qs_ref, ks_ref, vs_ref, gss_ref, s0_ref,
                    y_ref, sp_ref, gated_s_ref, ss_ref,
                    h_scr, q_scr, k_scr, v_scr, gs_scr, kdt_scr, gated_ref, o_scr, sp_bf, ss_bf,
                    *, chunk, nb):
    @pl.when(pl.program_id(1) == 0)
    def _():
        sp_ref[...] = jnp.zeros_like(sp_ref)
        sp_bf[...] = jnp.zeros_like(sp_bf)

    @pl.when(jnp.logical_and(pl.program_id(0) == 0, pl.program_id(1) == 0))
    def _():
        h_scr[...] = _rms(x_ref[...], g_ref[...]).astype(BF16)

    heads = range(RET_HEADS)
    _ret_projections(h_scr[...], wq_ref, wk_ref, wv_ref, wg_ref, cos_ref, sin_ref,
                     q_scr, k_scr, v_scr, gs_scr)
    prompt = _RetUnit(q_scr[...], k_scr[...], v_scr[...], gs_scr[...],
                      sp_ref.at[0], sp_ref.at[0], sp_bf, sp_bf, chunk)
    for hd in heads:
        kdt_scr[hd] = _ret_decayed_keys(prompt, hd).T.astype(BF16)

    pad = SAMPLE_KEY_ROWS - DEC_SEQ
    kpad = jnp.zeros((pad, ks_ref.shape[1]), F32)
    vpad = jnp.zeros((pad, vs_ref.shape[1]), F32)
    samples = []
    for b in range(nb):
        rows = slice(b * DEC_SEQ, (b + 1) * DEC_SEQ)
        samples.append(_RetUnit(
            qs_ref[rows, :].astype(BF16),
            jnp.concatenate([ks_ref[rows, :], kpad], axis=0).astype(BF16),
            jnp.concatenate([vs_ref[rows, :], vpad], axis=0).astype(BF16),
            gss_ref[rows, :], s0_ref.at[b], ss_ref.at[b], ss_bf.at[b], None, DEC_SEQ))
    scores_s = [[_ret_scores(u, hd) for hd in heads] for u in samples]
    for u in samples:
        for hd in heads:
            _ret_update_state(u, hd)
    for b in range(nb):
        for hd in heads:
            ss_bf[b, hd] = s0_ref[b, hd].astype(BF16)

    scores_p = [_ret_scores(prompt, hd) for hd in heads]
    o_s = [[_ret_output(u, hd, sc[hd]) for hd in heads] for u, sc in zip(samples, scores_s)]
    for hd in heads:
        o_scr[hd] = _ret_output(prompt, hd, scores_p[hd])
    for hd in heads:
        _ret_update_state(prompt, hd, kdt_scr[hd])

    for hd in heads:
        cols = slice(hd * RET_DV, (hd + 1) * RET_DV)
        for r0 in range(0, chunk, NORM_ROWS):
            rows = slice(r0, r0 + NORM_ROWS)
            gated_ref[rows, cols] = _ret_gate(o_scr[hd, rows, :], gs_scr[rows, cols]).astype(BF16)
        for b, u in enumerate(samples):
            rows = slice(b * DEC_SEQ, (b + 1) * DEC_SEQ)
            gated_s_ref[rows, cols] = _ret_gate(o_s[b][hd], u.gsc[:, cols])
    y_ref[...] = x_ref[...] + jnp.dot(gated_ref[...], wo_ref[...], preferred_element_type=F32)
    h_scr[...] = _rms(xn_ref[...], g_ref[...]).astype(BF16)


def _cast_slab_spec(rows, cols, n_steps):
    for span in (1, 2, 4):
        slab, rem = divmod(rows * span, n_steps)
        if rem == 0 and slab % BF16_SUBLANES == 0:
            return slab, span
    raise ValueError(f"no slab tiling for {rows} rows over {n_steps} steps")


def _ret_layer(x, gamma, wq, wk, wv, wg, wo, layer, cos, sin, qs, ks, vs, gss, s0, batch, chunk, to_cast):
    n = x.shape[0]
    steps = n // batch // chunk
    cast_specs = []
    for w in to_cast:
        slab, span = _cast_slab_spec(w.shape[0], w.shape[1], batch * steps)
        cast_specs.append(pl.BlockSpec(
            (slab, w.shape[1]), lambda b, c, span=span: ((b * steps + c) // span, 0)))
    assert cos.shape[0] == steps * chunk
    nb = s0.shape[0] // (batch * steps)
    assert nb * batch * steps == s0.shape[0]
    nqk = RET_HEADS * RET_DK
    nv = RET_HEADS * RET_DV
    ts = nb * DEC_SEQ
    row = lambda b, c: (b * steps + c, 0)
    next_row = lambda b, c: (jnp.minimum(b * steps + c + 1, batch * steps - 1), 0)
    pos = lambda b, c: (c, 0)
    state_s = pl.BlockSpec((nb, RET_HEADS, RET_DK, RET_DV), lambda b, c: (b * steps + c, 0, 0, 0))
    return pl.pallas_call(
        functools.partial(_ret_layer_kernel, chunk=chunk, nb=nb, n_cast=len(to_cast)),
        grid=(batch, steps),
        in_specs=[
            pl.BlockSpec((chunk, D_MODEL), row),
            pl.BlockSpec((chunk, D_MODEL), next_row),
            _resident((1, D_MODEL)),
            _layer_resident(wq.shape, layer),
            _layer_resident(wk.shape, layer),
            _layer_resident(wv.shape, layer),
            _layer_resident(wg.shape, layer),
            pl.BlockSpec((chunk, LANES), pos),
            pl.BlockSpec((chunk, LANES), pos),
            _layer_resident(wo.shape, layer),
            pl.BlockSpec((ts, nqk), row),
            pl.BlockSpec((ts, nqk), row),
            pl.BlockSpec((ts, nv), row),
            pl.BlockSpec((ts, nv), row),
            state_s,
        ] + cast_specs,
        out_specs=[
            pl.BlockSpec((chunk, D_MODEL), row),
            pl.BlockSpec((1, RET_HEADS, RET_DK, RET_DV), lambda b, c: (b, 0, 0, 0)),
            pl.BlockSpec((ts, nv), row),
            state_s,
        ] + cast_specs,
        out_shape=[
            jax.ShapeDtypeStruct((n, D_MODEL), F32),
            jax.ShapeDtypeStruct((batch, RET_HEADS, RET_DK, RET_DV), F32),
            jax.ShapeDtypeStruct((s0.shape[0] * DEC_SEQ, nv), F32),
            jax.ShapeDtypeStruct(s0.shape, F32),
        ] + [jax.ShapeDtypeStruct(w.shape, BF16) for w in to_cast],
        scratch_shapes=[
            pltpu.VMEM((chunk, D_MODEL), BF16),
            pltpu.VMEM((chunk, nqk), BF16),
            pltpu.VMEM((chunk, nqk), BF16),
            pltpu.VMEM((chunk, nv), BF16),
            pltpu.VMEM((chunk, nv), BF16),
            pltpu.VMEM((RET_HEADS, RET_DK, chunk), BF16),
            pltpu.VMEM((chunk, nv), BF16),
            pltpu.VMEM((RET_HEADS, chunk, RET_DV), F32),
            pltpu.VMEM((RET_HEADS, RET_DK, RET_DV), BF16),
            pltpu.VMEM((nb, RET_HEADS, RET_DK, RET_DV), BF16),
        ],
        compiler_params=_params(2),
        name="ret_layer",
    )(x, x, gamma, wq, wk, wv, wg, cos, sin, wo, qs, ks, vs, gss, s0, *to_cast)


def _ffn_block(x, g_ref, w1_ref, w3_ref, w2_ref, gf_ref, final_norm):
    h = _rms(x, g_ref[...]).astype(BF16)
    y = x
    for lo, hi in zip(FFN_SPLITS[:-1], FFN_SPLITS[1:]):
        cols = slice(lo, hi)
        a = jnp.dot(h, w1_ref[:, cols], preferred_element_type=F32)
        b = jnp.dot(h, w3_ref[:, cols], preferred_element_type=F32)
        act = (_silu(a) * b).astype(BF16)
        y = y + jnp.dot(act, w2_ref[cols, :], preferred_element_type=F32)
    return _rms(y, gf_ref[...]) if final_norm else y


def _ffn_kernel(x_ref, g_ref, w1_ref, w3_ref, w2_ref, gf_ref, y_ref, *, final_norm):
    y_ref[...] = _ffn_block(x_ref[...], g_ref, w1_ref, w3_ref, w2_ref, gf_ref, final_norm)


def _mixer_out_ffn_kernel(a_ref, wl_ref, bl_ref, res_ref, g_ref, w1_ref, w3_ref, w2_ref, gf_ref, y_ref,
                          *, final_norm):
    x = res_ref[...] + jnp.dot(a_ref[...].astype(BF16), wl_ref[...],
                               preferred_element_type=F32) + bl_ref[...]
    y_ref[...] = _ffn_block(x, g_ref, w1_ref, w3_ref, w2_ref, gf_ref, final_norm)


def _mixer_out_ffn(a, wl, bl, res, gamma, w1, w3, w2, layer, gamma_final, tm, final_norm):
    n, kdim = a.shape
    row = lambda i: (i, 0)
    return pl.pallas_call(
        functools.partial(_mixer_out_ffn_kernel, final_norm=final_norm),
        grid=(n // tm,),
        in_specs=[
            pl.BlockSpec((tm, kdim), row),
            _resident(wl.shape),
            _resident(bl.shape),
            pl.BlockSpec((tm, D_MODEL), row),
            _resident((1, D_MODEL)),
            _layer_resident(w1.shape, layer),
            _layer_resident(w3.shape, layer),
            _layer_resident(w2.shape, layer),
            _resident((1, D_MODEL)),
        ],
        out_specs=pl.BlockSpec((tm, D_MODEL), row),
        out_shape=jax.ShapeDtypeStruct((n, D_MODEL), F32),
        compiler_params=_params(1),
        name="mixer_out_ffn",
    )(a, wl, bl, res, gamma, w1, w3, w2, gamma_final)


def _ffn(x, gamma, w1, w3, w2, layer, gamma_final, tm, final_norm):
    n = x.shape[0]
    row = lambda i: (i, 0)
    return pl.pallas_call(
        functools.partial(_ffn_kernel, final_norm=final_norm),
        grid=(n // tm,),
        in_specs=[
            pl.BlockSpec((tm, D_MODEL), row),
            _resident((1, D_MODEL)),
            _layer_resident(w1.shape, layer),
            _layer_resident(w3.shape, layer),
            _layer_resident(w2.shape, layer),
            _resident((1, D_MODEL)),
        ],
        out_specs=pl.BlockSpec((tm, D_MODEL), row),
        out_shape=jax.ShapeDtypeStruct((n, D_MODEL), F32),
        compiler_params=_params(1),
        name="ffn",
    )(x, gamma, w1, w3, w2, gamma_final)


def _swa_proj_kernel(x_ref, g_ref, wq_ref, bq_ref, wkv_ref, bkv_ref, q_ref, kv_ref):
    h = _rms(x_ref[...], g_ref[...]).astype(BF16)
    q = jnp.dot(h, wq_ref[...], preferred_element_type=F32) + bq_ref[...]
    q_ref[...] = (q * (SWA_HEAD_DIM ** -0.5)).astype(q_ref.dtype)
    kv_ref[...] = jnp.dot(h, wkv_ref[...], preferred_element_type=F32) + bkv_ref[...]


def _swa_proj(x, gamma, wq, bq, wkv, bkv, tm, q_dtype):
    n = x.shape[0]
    nq = SWA_HEADS * SWA_HEAD_DIM
    nkv = 2 * SWA_KV_HEADS * SWA_HEAD_DIM
    row = lambda i: (i, 0)
    return pl.pallas_call(
        _swa_proj_kernel,
        grid=(n // tm,),
        in_specs=[
            pl.BlockSpec((tm, D_MODEL), row),
            _resident((1, D_MODEL)),
            _resident(wq.shape),
            _resident(bq.shape),
            _resident(wkv.shape),
            _resident(bkv.shape),
        ],
        out_specs=[pl.BlockSpec((tm, nq), row), pl.BlockSpec((tm, nkv), row)],
        out_shape=[
            jax.ShapeDtypeStruct((n, nq), q_dtype),
            jax.ShapeDtypeStruct((n, nkv), F32),
        ],
        compiler_params=_params(1),
        name="swa_proj",
    )(x, gamma, wq, bq, wkv, bkv)


def _swa_attend(qs, k2s, v2s, prev_valid, sinks_ref):
    r = qs[0].shape[0]
    hd = SWA_HEAD_DIM
    lane_lo = lax.broadcasted_iota(jnp.int32, (r, LANES), 1) < hd
    zero = jnp.zeros((r, LANES), qs[0].dtype)
    sink_pieces = []
    for j in range(SWA_GROUP):
        sink_pieces.append(jnp.full((r, 1), sinks_ref[j], F32))
        sink_pieces.append(jnp.full((r, 1), sinks_ref[SWA_GROUP + j], F32))
    sink = jnp.concatenate(sink_pieces * len(qs), axis=0)
    rows = SWA_HEADS * r

    s2_parts = []
    for q, k2 in zip(qs, k2s):
        pieces = []
        for j in range(SWA_GROUP):
            slab = q[:, j * LANES:(j + 1) * LANES]
            pieces.append(jnp.where(lane_lo, slab, zero))
            pieces.append(jnp.where(lane_lo, zero, slab))
        stacked = jnp.concatenate(pieces, axis=0).astype(BF16)
        s2_parts.append(lax.dot_general(stacked, k2, (((1,), (1,)), ((), ())),
                                        preferred_element_type=F32))
    s2 = jnp.concatenate(s2_parts, axis=0)
    total = rows * len(qs)
    qi = lax.broadcasted_iota(jnp.int32, (total, WINDOW), 0) & (r - 1)
    kj = lax.broadcasted_iota(jnp.int32, (total, WINDOW), 1)
    cur = kj <= qi
    prev_cap = jnp.where(prev_valid, jnp.float32(3.0e38), jnp.float32(NEG_INF))
    s = jnp.where(cur, s2[:, WINDOW:], jnp.minimum(s2[:, :WINDOW], prev_cap))
    m = jnp.maximum(jnp.max(s, axis=-1, keepdims=True), sink)
    p = jnp.exp(s - m)
    den = jnp.sum(p, axis=-1, keepdims=True) + jnp.exp(sink - m)
    p2 = jnp.concatenate([jnp.where(cur, 0.0, p), jnp.where(cur, p, 0.0)], axis=1).astype(BF16)
    results = []
    for e, v2 in enumerate(v2s):
        blk = slice(e * rows, (e + 1) * rows)
        o = jnp.dot(p2[blk], v2, preferred_element_type=F32) / den[blk]
        outs = [jnp.where(lane_lo, o[(2 * j) * r:(2 * j + 1) * r], o[(2 * j + 1) * r:(2 * j + 2) * r])
                for j in range(SWA_GROUP)]
        results.append(jnp.concatenate(outs, axis=1))
    return results


def _swa_prompt_kernel(sinks_ref, q_ref, kvp_ref, kvc_ref, x_ref, wo_ref, bo_ref, y_ref, o_ref,
                       *, n_sub):
    nk = SWA_KV_HEADS * SWA_HEAD_DIM
    r = WINDOW
    lane_lo = lax.broadcasted_iota(jnp.int32, (r, LANES), 1) < SWA_HEAD_DIM
    qi = lax.broadcasted_iota(jnp.int32, (r, WINDOW), 0)
    kj = lax.broadcasted_iota(jnp.int32, (r, WINDOW), 1)
    cur_mask = kj <= qi
    ones = jnp.ones((2 * WINDOW, LANES), BF16)
    nt = (((1,), (1,)), ((), ()))
    for c in range(n_sub):
        rows = slice(c * WINDOW, (c + 1) * WINDOW)
        prev = kvp_ref[...] if c == 0 else kvc_ref[(c - 1) * WINDOW:c * WINDOW, :]
        cur = kvc_ref[rows, :]
        k2 = jnp.concatenate([prev[:, :nk], cur[:, :nk]], axis=0).astype(BF16)
        v2 = jnp.concatenate([prev[:, nk:], cur[:, nk:]], axis=0).astype(BF16)
        v2_ones = jnp.concatenate([v2, ones], axis=1)
        prev_valid = (pl.program_id(1) > 0) if c == 0 else True
        prev_cap = jnp.where(prev_valid, jnp.float32(3.0e38), jnp.float32(NEG_INF))
        for j in range(SWA_GROUP):
            slab = q_ref[rows, j * LANES:(j + 1) * LANES]
            zero = jnp.zeros_like(slab)
            num = []
            den = []
            for side in range(2):
                qm = jnp.where(lane_lo, slab, zero) if side == 0 else jnp.where(lane_lo, zero, slab)
                s2 = lax.dot_general(qm, k2, nt, preferred_element_type=F32)
                s = jnp.where(cur_mask, s2[:, WINDOW:], jnp.minimum(s2[:, :WINDOW], prev_cap))
                sink = sinks_ref[j + side * SWA_GROUP]
                m = jnp.maximum(jnp.max(s, axis=-1, keepdims=True), sink)
                p = jnp.exp(s - m)
                p2 = jnp.concatenate([jnp.where(cur_mask, 0.0, p), jnp.where(cur_mask, p, 0.0)],
                                     axis=1).astype(BF16)
                o2 = jnp.dot(p2, v2_ones, preferred_element_type=F32)
                num.append(o2[:, :LANES])
                den.append(o2[:, LANES:] + jnp.exp(sink - m))
            o_ref[rows, j * LANES:(j + 1) * LANES] = (
                jnp.where(lane_lo, num[0], num[1]) / jnp.where(lane_lo, den[0], den[1])).astype(BF16)
    y = jnp.dot(o_ref[...], wo_ref[...], preferred_element_type=F32)
    y_ref[...] = x_ref[...] + y + bo_ref[...]


def _swa_prompt(q, kv, x, wo, bo, sinks, batch, n_sub):
    n = x.shape[0]
    tm = n_sub * WINDOW
    steps = n // batch // tm
    nq = SWA_HEADS * SWA_HEAD_DIM
    nkv = kv.shape[1]
    row = lambda b, i: (b * steps + i, 0)
    prev = lambda b, i: ((b * steps + i) * n_sub - jnp.minimum(i, 1), 0)
    return pl.pallas_call(
        functools.partial(_swa_prompt_kernel, n_sub=n_sub),
        grid=(batch, steps),
        in_specs=[
            pl.BlockSpec(memory_space=pltpu.SMEM),
            pl.BlockSpec((tm, nq), row),
            pl.BlockSpec((WINDOW, nkv), prev),
            pl.BlockSpec((tm, nkv), row),
            pl.BlockSpec((tm, D_MODEL), row),
            _resident(wo.shape),
            _resident(bo.shape),
        ],
        out_specs=pl.BlockSpec((tm, D_MODEL), row),
        out_shape=jax.ShapeDtypeStruct((n, D_MODEL), F32),
        scratch_shapes=[pltpu.VMEM((tm, nq), BF16)],
        compiler_params=_params(2),
        name="swa_prompt",
    )(sinks, q, kv, kv, x, wo, bo)


def _swa_sample_kernel(sinks_ref, q_ref, kvn_ref, ck_ref, cv_ref, o_ref, nk_ref, nv_ref, *, nb):
    nk = SWA_KV_HEADS * SWA_HEAD_DIM
    t = DEC_SEQ
    zpad = jnp.zeros((WINDOW - t, nk), F32)
    for g0 in range(0, nb, SWA_SAMPLE_GROUP):
        group = range(g0, g0 + SWA_SAMPLE_GROUP)
        qs, k2s, v2s = [], [], []
        for b in group:
            rows = slice(b * t, (b + 1) * t)
            k_new = kvn_ref[rows, :nk]
            v_new = kvn_ref[rows, nk:]
            ck = ck_ref[b]
            cv = cv_ref[b]
            qs.append(q_ref[rows, :])
            k2s.append(jnp.concatenate([ck, k_new, zpad], axis=0).astype(BF16))
            v2s.append(jnp.concatenate([cv, v_new, zpad], axis=0).astype(BF16))
            nk_ref[b, :WINDOW - t, :] = ck[t:, :]
            nk_ref[b, WINDOW - t:, :] = k_new
            nv_ref[b, :WINDOW - t, :] = cv[t:, :]
            nv_ref[b, WINDOW - t:, :] = v_new
        for b, o in zip(group, _swa_attend(qs, k2s, v2s, True, sinks_ref)):
            o_ref[b * t:(b + 1) * t, :] = o


def _swa_sample(q, kv_new, cache_k, cache_v, sinks, nb):
    batch = cache_k.shape[0]
    nq = SWA_HEADS * SWA_HEAD_DIM
    nk = SWA_KV_HEADS * SWA_HEAD_DIM
    row = lambda b: (b, 0)
    cache = pl.BlockSpec((nb, WINDOW, nk), lambda b: (b, 0, 0))
    return pl.pallas_call(
        functools.partial(_swa_sample_kernel, nb=nb),
        grid=(batch // nb,),
        in_specs=[
            pl.BlockSpec(memory_space=pltpu.SMEM),
            pl.BlockSpec((nb * DEC_SEQ, nq), row),
            pl.BlockSpec((nb * DEC_SEQ, 2 * nk), row),
            cache,
            cache,
        ],
        out_specs=[pl.BlockSpec((nb * DEC_SEQ, nq), row), cache, cache],
        out_shape=[
            jax.ShapeDtypeStruct((batch * DEC_SEQ, nq), F32),
            jax.ShapeDtypeStruct(cache_k.shape, F32),
            jax.ShapeDtypeStruct(cache_v.shape, F32),
        ],
        compiler_params=_params(1),
        name="swa_sample",
    )(sinks, q, kv_new, cache_k, cache_v)


def _rotation_tables(positions, repeat):
    half = RET_DK // 2
    inv = 1.0 / (ROPE_BASE ** (np.arange(half, dtype=np.float64) / half))
    ang = np.asarray(positions, np.float64)[:, None] * inv[None, :]
    cos = np.tile(np.cos(ang), (repeat, 1)).astype(np.float32)
    sin = np.tile(np.sin(ang), (repeat, 1)).astype(np.float32)
    return jnp.asarray(cos), jnp.asarray(sin)


TM_PROJ = 512
TM_SAMPLE = 512
TM_FFN = 1024
FFN_SPLITS = (0, 1536, D_FF)
SWA_SAMPLE_GROUP = 4
TM_SAMPLE_FFN = 512


def kernel(x_prompt, x_sample, state_ret, cache_swa_k, cache_swa_v, ret_w_q, ret_w_k, ret_w_v, ret_w_g, ret_w_o, swa_w_qkv, swa_b_qkv, swa_w_o, swa_b_o, swa_sinks, norm_mix, norm_ffn, ffn_w1, ffn_w3, ffn_w2, norm_final):
    bp, tp, d = x_prompt.shape
    bs, ts, _ = x_sample.shape
    assert (tp, d, ts) == (SEQ, D_MODEL, DEC_SEQ)
    xp = x_prompt.reshape(bp * tp, d)
    xs = x_sample.reshape(bs * ts, d)
    nk = SWA_KV_HEADS * SWA_HEAD_DIM

    cos_p, sin_p = _rotation_tables(np.arange(tp), 1)
    cos_s, sin_s = _rotation_tables(PAST_LEN + np.arange(ts), TM_SAMPLE // ts)
    gamma_final = norm_final.reshape(1, d)
    zero_bias = jnp.zeros((1, d), F32)

    ret_w = [w.astype(BF16) for w in (ret_w_q, ret_w_k, ret_w_v, ret_w_g)]
    w_ro = ret_w_o.astype(BF16)
    g_mix = norm_mix[0].reshape(1, d)
    qs, ks, vs, gs = _ret_proj(xs, g_mix, *ret_w, 0, cos_s, sin_s, TM_SAMPLE, F32)
    later_f32 = [w.reshape(-1, w.shape[-1]) for w in (ffn_w1, ffn_w3, ffn_w2, swa_w_qkv, swa_w_o)]
    xp, state_p, gated_s, state_s, w1, w3, w2, swa_qkv_bf, swa_o_bf = _ret_layer(
        xp, g_mix, *ret_w, w_ro, 0, cos_p, sin_p, qs, ks, vs, gs, state_ret[0], bp, RET_PROMPT_CHUNK,
        later_f32)
    w1 = w1.reshape(ffn_w1.shape)
    w3 = w3.reshape(ffn_w3.shape)
    w2 = w2.reshape(ffn_w2.shape)
    g_ffn = norm_ffn[0].reshape(1, d)
    xp = _ffn(xp, g_ffn, w1, w3, w2, 0, gamma_final, TM_FFN, False)
    xs = _mixer_out_ffn(gated_s, w_ro[0], zero_bias, xs, g_ffn, w1, w3, w2, 0, gamma_final,
                        TM_SAMPLE_FFN, False)

    nq = SWA_HEADS * SWA_HEAD_DIM
    heads = (SWA_KV_HEADS, SWA_GROUP, SWA_HEAD_DIM)
    w_q = swa_qkv_bf[:, :nq].reshape(d, *heads).transpose(0, 2, 1, 3).reshape(d, nq)
    b_q = swa_b_qkv[0][:nq].reshape(heads).transpose(1, 0, 2).reshape(1, nq)
    w_kv = swa_qkv_bf[:, nq:]
    b_kv = swa_b_qkv[0][nq:].reshape(1, 2 * nk)
    w_so = swa_o_bf.reshape(*heads, d).transpose(1, 0, 2, 3).reshape(nq, d)
    b_so = swa_b_o[0].reshape(1, d)
    sinks = swa_sinks[0]
    g_mix = norm_mix[1].reshape(1, d)
    qp, kvp = _swa_proj(xp, g_mix, w_q, b_q, w_kv, b_kv, 4 * TM_PROJ, BF16)
    xp = _swa_prompt(qp, kvp, xp, w_so, b_so, sinks, bp, 8)
    qs, kvs = _swa_proj(xs, g_mix, w_q, b_q, w_kv, b_kv, TM_SAMPLE, F32)
    att_s, new_k, new_v = _swa_sample(
        qs, kvs, cache_swa_k[0].reshape(bs, WINDOW, nk), cache_swa_v[0].reshape(bs, WINDOW, nk), sinks, 16)

    g_ffn = norm_ffn[1].reshape(1, d)
    yp = _ffn(xp, g_ffn, w1, w3, w2, 1, gamma_final, TM_FFN, True)
    ys = _mixer_out_ffn(att_s, w_so, b_so, xs, g_ffn, w1, w3, w2, 1, gamma_final, TM_SAMPLE_FFN, True)

    kv_tail = kvp.reshape(bp, tp, 2 * nk)[:, tp - WINDOW:, :]
    kv_shape = (1, bp, WINDOW, SWA_KV_HEADS, SWA_HEAD_DIM)
    cache_shape = (1, bs, WINDOW, SWA_KV_HEADS, SWA_HEAD_DIM)
    return (
        yp.reshape(bp, tp, d),
        ys.reshape(bs, ts, d),
        state_p[None],
        state_s[None],
        kv_tail[..., :nk].reshape(kv_shape),
        kv_tail[..., nk:].reshape(kv_shape),
        new_k.reshape(cache_shape),
        new_v.reshape(cache_shape),
    )
```
